```python
import jax, jax.numpy as jnp
from jax import lax
import numpy as np

D_MODEL = 1024
BATCH = 16
SEQ = 2048
DEPTH = 4
DEC_BATCH = 32
DEC_SEQ = 64
PAST_LEN = 4096

CHUNK = 64
MIX = D_MODEL
A_HEADS = 8
A_KV_HEADS = 2
A_HEAD_DIM = (MIX // 2) // A_HEADS
WINDOW = 128
WINDOW_CHUNKS = WINDOW // CHUNK
ROPE_THETA = 500000.0
ROPE_DIM = A_HEAD_DIM // 4
M_HEADS = 4
M_HEAD_DIM = (MIX // 2) // M_HEADS
D_FF = 4 * D_MODEL
NORM_EPS = 1e-6
NEG = -1e30
A_Q = A_HEADS * A_HEAD_DIM
A_KV = A_KV_HEADS * A_HEAD_DIM
M_W = M_HEADS * M_HEAD_DIM
IN_COLS = A_Q + 2 * A_KV + 4 * M_W + 2 * M_HEADS

kernel_name = 'hymba_mlstm_swa_sink_streaming_step'


def rmsnorm(x, g):
    xf = x.astype(jnp.float32)
    r = lax.rsqrt(jnp.mean(xf * xf, axis=-1, keepdims=True) + NORM_EPS)
    return (xf * r * g.astype(jnp.float32)).astype(x.dtype)


def partial_rotary(x, pos):
    half = ROPE_DIM // 2
    inv = ROPE_THETA ** (-(jnp.arange(half, dtype=jnp.float32) * 2.0 / ROPE_DIM))
    ang = pos[:, None] * inv[None, :]
    cos = jnp.cos(ang)[None, :, None, :]
    sin = jnp.sin(ang)[None, :, None, :]
    xf = x.astype(jnp.float32)
    x1 = xf[..., :half]
    x2 = xf[..., half:ROPE_DIM]
    out = jnp.concatenate([x1 * cos - x2 * sin, x2 * cos + x1 * sin, xf[..., ROPE_DIM:]], axis=-1)
    return out.astype(x.dtype)


def sink_attention(q, k, v, valid, sink):
    B, N, Lq, H, d = q.shape
    G = k.shape[3]
    R = H // G
    qf = q.astype(jnp.float32).reshape(B, N, Lq, G, R, d)
    s = jnp.einsum('bnqgrd,bnkgd->bngrqk', qf, k.astype(jnp.float32)) * (d ** -0.5)
    s = jnp.where(valid[None, :, None, None, None, :], s, NEG)
    sk = sink.astype(jnp.float32).reshape(1, 1, G, R, 1, 1)
    mx = jnp.maximum(jnp.max(s, axis=-1, keepdims=True), sk)
    p = jnp.exp(s - mx)
    p = p / (jnp.sum(p, axis=-1, keepdims=True) + jnp.exp(sk - mx))
    o = jnp.einsum('bngrqk,bnkgd->bnqgrd', p, v.astype(jnp.float32))
    return o.reshape(B, N, Lq, H * d)


def banded_prompt_attention(q, k, v, sink):
    B, T, H, d = q.shape
    NC = T // CHUNK
    P = WINDOW_CHUNKS * CHUNK
    pad = ((0, 0), (P, 0), (0, 0), (0, 0))
    kc = jnp.pad(k, pad).reshape(B, NC + WINDOW_CHUNKS, CHUNK, A_KV_HEADS, d)
    vc = jnp.pad(v, pad).reshape(B, NC + WINDOW_CHUNKS, CHUNK, A_KV_HEADS, d)
    kb = jnp.concatenate([kc[:, j:j + NC] for j in range(WINDOW_CHUNKS + 1)], axis=2)
    vb = jnp.concatenate([vc[:, j:j + NC] for j in range(WINDOW_CHUNKS + 1)], axis=2)
    key_pos = (jnp.arange(NC)[:, None] - WINDOW_CHUNKS) * CHUNK + jnp.arange((WINDOW_CHUNKS + 1) * CHUNK)[None, :]
    valid = key_pos >= 0
    o = sink_attention(q.reshape(B, NC, CHUNK, H, d), kb, vb, valid, sink)
    return o.reshape(B, T, H * d)


def sample_attention(q, k_new, v_new, k_cache, v_cache, sink):
    B, S, H, d = q.shape
    kk = jnp.concatenate([k_cache.astype(k_new.dtype), k_new], axis=1)[:, None]
    vv = jnp.concatenate([v_cache.astype(v_new.dtype), v_new], axis=1)[:, None]
    valid = jnp.ones((1, kk.shape[2]), dtype=bool)
    o = sink_attention(q[:, None], kk, vv, valid, sink)
    return o.reshape(B, S, H * d)


def mlstm_chunkwise(q, k, v, ig, lf, C0, n0, m0, L):
    B, T, H, d = q.shape
    NC = T // L

    def blk(a):
        return a.astype(jnp.float32).reshape(B, NC, L, H, d).transpose(1, 0, 3, 2, 4)

    def gblk(a):
        return a.astype(jnp.float32).reshape(B, NC, L, H).transpose(1, 0, 3, 2)

    causal = jnp.tril(jnp.ones((L, L), dtype=bool))

    def step(carry, xs):
        C, n, m = carry
        qc, kc, vc, ic, fc = xs
        b = jnp.cumsum(fc, axis=-1)
        D = b[..., :, None] - b[..., None, :] + ic[..., None, :]
        D = jnp.where(causal, D, NEG)
        inter = b + m[..., None]
        mt = jnp.maximum(inter, jnp.max(D, axis=-1))
        S = jnp.einsum('bhtd,bhsd->bhts', qc, kc) * jnp.exp(D - mt[..., None])
        w_inter = jnp.exp(inter - mt)
        num = jnp.einsum('bhts,bhsd->bhtd', S, vc) + w_inter[..., None] * jnp.einsum('bhvk,bhtk->bhtv', C, qc)
        den = jnp.sum(S, axis=-1) + w_inter * jnp.einsum('bhk,bhtk->bht', n, qc)
        h = num / jnp.maximum(jnp.abs(den), jnp.exp(-mt))[..., None]
        mL = mt[..., -1]
        ws = jnp.exp(b[..., -1:] - b + ic - mL[..., None])
        decay = jnp.exp(b[..., -1] + m - mL)
        C_new = decay[..., None, None] * C + jnp.einsum('bhs,bhsv,bhsk->bhvk', ws, vc, kc)
        n_new = decay[..., None] * n + jnp.einsum('bhs,bhsk->bhk', ws, kc)
        return (C_new, n_new, mL), h

    carry0 = (C0.astype(jnp.float32), n0.astype(jnp.float32), m0.astype(jnp.float32))
    (C, n, m), hs = lax.scan(step, carry0, (blk(q), blk(k), blk(v), gblk(ig), gblk(lf)))
    h = hs.transpose(1, 0, 3, 2, 4).reshape(B, T, H, d)
    return h, C, n, m


def layer(x, pos, C0, n0, m0, k_cache, v_cache, chunk_len,
          g1, w_in, gate_b, sink, mnorm_g, w_out, g2, w_up, w_down):
    B, T, _ = x.shape
    h = rmsnorm(x, g1)
    z = h @ w_in
    sizes = [A_Q, A_KV, A_KV, M_W, M_W, M_W, M_W, M_HEADS]
    idx = [int(s) for s in np.cumsum(sizes)]
    aq, ak, av, mq, mk, mv, mo, mi, mf = jnp.split(z, idx, axis=-1)
    aq = partial_rotary(aq.reshape(B, T, A_HEADS, A_HEAD_DIM), pos)
    ak = partial_rotary(ak.reshape(B, T, A_KV_HEADS, A_HEAD_DIM), pos)
    av = av.reshape(B, T, A_KV_HEADS, A_HEAD_DIM)
    if k_cache is None:
        attn = banded_prompt_attention(aq, ak, av, sink)
        k_rows = ak[:, -WINDOW:]
        v_rows = av[:, -WINDOW:]
    else:
        attn = sample_attention(aq, ak, av, k_cache, v_cache, sink)
        k_rows = ak
        v_rows = av
    gb = gate_b.astype(jnp.float32)
    ig = mi.astype(jnp.float32) + gb[:M_HEADS]
    lf = jax.nn.log_sigmoid(mf.astype(jnp.float32) + gb[M_HEADS:])
    hm, C, n, m = mlstm_chunkwise(
        mq.reshape(B, T, M_HEADS, M_HEAD_DIM),
        mk.reshape(B, T, M_HEADS, M_HEAD_DIM) * (M_HEAD_DIM ** -0.5),
        mv.reshape(B, T, M_HEADS, M_HEAD_DIM),
        ig, lf, C0, n0, m0, chunk_len)
    hm = rmsnorm(hm, mnorm_g.reshape(M_HEADS, M_HEAD_DIM)).reshape(B, T, M_W)
    hm = hm * jax.nn.sigmoid(mo.astype(jnp.float32))
    mix = jnp.concatenate([attn.astype(x.dtype), hm.astype(x.dtype)], axis=-1)
    x = x + mix @ w_out
    u = rmsnorm(x, g2) @ w_up
    x = x + jnp.square(jax.nn.relu(u)) @ w_down
    return x, k_rows, v_rows, C, n, m


def setup_inputs(seed: int = 0) -> dict:
    key = jax.random.key(seed)
    ks = jax.random.split(key, 20)
    f32 = jnp.float32
    nrm = lambda k, s: jax.random.normal(k, s, dtype=f32)
    gate_b = jnp.concatenate([
        0.1 * nrm(ks[9], (DEPTH, M_HEADS)),
        jnp.linspace(3.0, 6.0, M_HEADS, dtype=f32)[None, :] + 0.1 * nrm(ks[10], (DEPTH, M_HEADS))], axis=-1)
    return {
        'x_prompt': nrm(ks[0], (BATCH, SEQ, D_MODEL)),
        'x_sample': nrm(ks[1], (DEC_BATCH, DEC_SEQ, D_MODEL)),
        'cache_k': nrm(ks[2], (DEPTH, DEC_BATCH, WINDOW, A_KV_HEADS, A_HEAD_DIM)),
        'cache_v': nrm(ks[3], (DEPTH, DEC_BATCH, WINDOW, A_KV_HEADS, A_HEAD_DIM)),
        'state_C': 0.1 * nrm(ks[4], (DEPTH, DEC_BATCH, M_HEADS, M_HEAD_DIM, M_HEAD_DIM)),
        'state_n': 0.1 * nrm(ks[5], (DEPTH, DEC_BATCH, M_HEADS, M_HEAD_DIM)),
        'state_m': 0.5 * nrm(ks[6], (DEPTH, DEC_BATCH, M_HEADS)),
        'norm1_g': 1.0 + 0.02 * nrm(ks[7], (DEPTH, D_MODEL)),
        'w_in': nrm(ks[8], (DEPTH, D_MODEL, IN_COLS)) * D_MODEL ** -0.5,
        'gate_b': gate_b,
        'attn_sink': 0.5 * nrm(ks[11], (DEPTH, A_HEADS)),
        'mnorm_g': 1.0 + 0.02 * nrm(ks[12], (DEPTH, M_W)),
        'w_out': nrm(ks[13], (DEPTH, MIX, D_MODEL)) * MIX ** -0.5,
        'norm2_g': 1.0 + 0.02 * nrm(ks[14], (DEPTH, D_MODEL)),
        'w_up': nrm(ks[15], (DEPTH, D_MODEL, D_FF)) * D_MODEL ** -0.5,
        'w_down': nrm(ks[16], (DEPTH, D_FF, D_MODEL)) * D_FF ** -0.5,
        'final_g': 1.0 + 0.02 * nrm(ks[17], (D_MODEL,)),
    }


def reference(x_prompt, x_sample, cache_k, cache_v, state_C, state_n, state_m,
              norm1_g, w_in, gate_b, attn_sink, mnorm_g, w_out, norm2_g, w_up, w_down, final_g):
    Bp, T, _ = x_prompt.shape
    Bs, S, _ = x_sample.shape
    pos_p = jnp.arange(T, dtype=jnp.float32)
    pos_s = jnp.arange(S, dtype=jnp.float32) + float(PAST_LEN)
    zC = jnp.zeros((Bp, M_HEADS, M_HEAD_DIM, M_HEAD_DIM), jnp.float32)
    zn = jnp.zeros((Bp, M_HEADS, M_HEAD_DIM), jnp.float32)
    zm = jnp.zeros((Bp, M_HEADS), jnp.float32)
    yp, ys = x_prompt, x_sample
    pk, pv, pC, pn, pm = [], [], [], [], []
    sk, sv, sC, sn, sm = [], [], [], [], []
    for l in range(DEPTH):
        w = (norm1_g[l], w_in[l], gate_b[l], attn_sink[l], mnorm_g[l], w_out[l], norm2_g[l], w_up[l], w_down[l])
        yp, k_r, v_r, C, n, m = layer(yp, pos_p, zC, zn, zm, None, None, CHUNK, *w)
        pk.append(k_r); pv.append(v_r); pC.append(C); pn.append(n); pm.append(m)
        ys, k_r, v_r, C, n, m = layer(ys, pos_s, state_C[l], state_n[l], state_m[l], cache_k[l], cache_v[l], S, *w)
        sk.append(k_r); sv.append(v_r); sC.append(C); sn.append(n); sm.append(m)
    y_prompt = rmsnorm(yp, final_g)
    y_sample = rmsnorm(ys, final_g)
    return (y_prompt, y_sample,
            jnp.stack(pk), jnp.stack(pv), jnp.stack(pC), jnp.stack(pn), jnp.stack(pm),
            jnp.stack(sk), jnp.stack(sv), jnp.stack(sC), jnp.stack(sn), jnp.stack(sm))
```

```python
import functools

import jax
import jax.numpy as jnp
from jax import lax
from jax.experimental import pallas as pl
from jax.experimental.pallas import tpu as pltpu

D_MODEL = 1024
CHUNK = 64
A_HEADS = 8
A_KV_HEADS = 2
A_HEAD_DIM = 64
WINDOW = 128
ROPE_THETA = 500000.0
ROPE_DIM = A_HEAD_DIM // 4
ROPE_HALF = ROPE_DIM // 2
M_HEADS = 4
M_HEAD_DIM = 128
D_FF = 4 * D_MODEL
NORM_EPS = 1e-6
NEG = -1e30
PAST_LEN = 4096
A_Q = A_HEADS * A_HEAD_DIM
A_KV = A_KV_HEADS * A_HEAD_DIM
M_W = M_HEADS * M_HEAD_DIM

LANES = 128
GATE_COLS = LANES
COL_KV = A_Q
COL_M = A_Q + 2 * A_KV
COL_G = COL_M + 4 * M_W
IN_COLS_PAD = COL_G + GATE_COLS

TOKEN_TILE = 512
PROMPT_MLSTM_CHUNK = 256
FF_CHUNK = 1024
VMEM_LIMIT = 56 * 1024 * 1024

BF16 = jnp.bfloat16
F32 = jnp.float32


def _dot(a, b):
    return jnp.dot(a, b, preferred_element_type=F32)


def _dot_nt(a, b):
    return lax.dot_general(a, b, (((1,), (1,)), ((), ())), preferred_element_type=F32)


def _rms(x, g):
    r = lax.rsqrt(jnp.mean(x * x, axis=-1, keepdims=True) + NORM_EPS)
    return x * r * g


def _inproj_kernel(x_ref, g_ref, w_ref, cos_ref, sa_ref, sb_ref, gb_ref,
                   q_ref, k_ref, v_ref, m4_ref, gt_ref):
    h = _rms(x_ref[...], g_ref[...]).astype(BF16)
    cos, sa, sb = cos_ref[...], sa_ref[...], sb_ref[...]

    def rotary(z):
        left = pltpu.roll(z, LANES - ROPE_HALF, axis=1)
        right = pltpu.roll(z, ROPE_HALF, axis=1)
        return z * cos + left * sa + right * sb

    for j in range(A_Q // LANES):
        zq = _dot(h, w_ref[:, j * LANES:(j + 1) * LANES])
        q_ref[:, j * LANES:(j + 1) * LANES] = rotary(zq).astype(BF16)
    zkv = _dot(h, w_ref[:, COL_KV:COL_KV + 2 * A_KV])
    k_ref[...] = rotary(zkv[:, :A_KV])
    v_ref[...] = zkv[:, A_KV:]
    for j in range(4):
        zm = _dot(h, w_ref[:, COL_M + j * M_W:COL_M + (j + 1) * M_W])
        if j == 1:
            zm = zm * (M_HEAD_DIM ** -0.5)
        m4_ref[:, j * M_W:(j + 1) * M_W] = zm.astype(BF16)
    zg = _dot(h, w_ref[:, COL_G:COL_G + GATE_COLS]) + gb_ref[...]
    logsig = jnp.minimum(zg, 0.0) - jnp.log1p(jnp.exp(-jnp.abs(zg)))
    lane = lax.broadcasted_iota(jnp.int32, zg.shape, 1)
    gt_ref[...] = jnp.where(lane < M_HEADS, zg, logsig)


def _inproj(x, g1, w, tabs, gb, tm):
    n = x.shape[0]
    cos, sa, sb = tabs
    nt = cos.shape[0] // tm
    row = lambda i: (i, 0)
    fixed = lambda i: (0, 0)
    tab = lambda i: (i % nt, 0)
    return pl.pallas_call(
        _inproj_kernel,
        grid=(n // tm,),
        in_specs=[
            pl.BlockSpec((tm, D_MODEL), row),
            pl.BlockSpec((1, D_MODEL), fixed),
            pl.BlockSpec((D_MODEL, IN_COLS_PAD), fixed),
            pl.BlockSpec((tm, LANES), tab),
            pl.BlockSpec((tm, LANES), tab),
            pl.BlockSpec((tm, LANES), tab),
            pl.BlockSpec((1, GATE_COLS), fixed),
        ],
        out_specs=[
            pl.BlockSpec((tm, A_Q), row),
            pl.BlockSpec((tm, A_KV), row),
            pl.BlockSpec((tm, A_KV), row),
            pl.BlockSpec((tm, 4 * M_W), row),
            pl.BlockSpec((tm, GATE_COLS), row),
        ],
        out_shape=[
            jax.ShapeDtypeStruct((n, A_Q), BF16),
            jax.ShapeDtypeStruct((n, A_KV), F32),
            jax.ShapeDtypeStruct((n, A_KV), F32),
            jax.ShapeDtypeStruct((n, 4 * M_W), BF16),
            jax.ShapeDtypeStruct((n, GATE_COLS), F32),
        ],
        compiler_params=pltpu.CompilerParams(
            dimension_semantics=("arbitrary",), vmem_limit_bytes=VMEM_LIMIT),
        name="inproj",
    )(x, g1, w, cos, sa, sb, gb)


def _split3(x):
    hi = x.astype(BF16)
    r1 = x - hi.astype(F32)
    mid = r1.astype(BF16)
    lo = (r1 - mid.astype(F32)).astype(BF16)
    return hi, mid, lo


def _mixer_kernel(sink_ref, q_ref, k_ref, v_ref, m4_ref, gt_ref, hk_ref, hv_ref, c0_ref, nm0_ref,
                  mg_ref, mix_ref, c_ref, nm_ref,
                  ka, kb, kc, kd, vt, vs, *, L, halo_valid):
    t = pl.program_id(1)
    n_chunks = L // CHUNK
    lane_k = lax.broadcasted_iota(jnp.int32, (1, LANES), 1)
    low = lane_k < A_HEAD_DIM

    def put_kv(rows, k, v):
        ksw = pltpu.roll(k, A_HEAD_DIM, axis=1)
        zero = jnp.zeros_like(k)
        ka[rows, :] = jnp.where(low, k, zero).astype(BF16)
        kd[rows, :] = jnp.where(low, zero, k).astype(BF16)
        kc[rows, :] = jnp.where(low, ksw, zero).astype(BF16)
        kb[rows, :] = jnp.where(low, zero, ksw).astype(BF16)
        vt[rows, :] = v.astype(BF16)
        vs[rows, :] = pltpu.roll(v, A_HEAD_DIM, axis=1).astype(BF16)

    @pl.when(t == 0)
    def _():
        put_kv(pl.ds(0, WINDOW), hk_ref[0], hv_ref[0])
        c_ref[...] = c0_ref[...]
        nm_ref[...] = nm0_ref[...]

    put_kv(pl.ds(WINDOW, L), k_ref[...], v_ref[...])

    n_keys = WINDOW + CHUNK
    row2 = lax.broadcasted_iota(jnp.int32, (2 * CHUNK, 1), 0)
    first = row2 < CHUNK
    lane_o = lax.broadcasted_iota(jnp.int32, (2 * CHUNK, LANES), 1)
    key_col = lax.broadcasted_iota(jnp.int32, (1, n_keys), 1)
    for ci in range(n_chunks):
        r0 = ci * CHUNK
        if not halo_valid:
            chunk_idx = t * n_chunks + ci
            valid = (chunk_idx - WINDOW // CHUNK) * CHUNK + key_col >= 0
        for g in range(A_KV_HEADS):
            pa, pb = 2 * g, 2 * g + 1
            qs = jnp.concatenate([q_ref[r0:r0 + CHUNK, pa * LANES:(pa + 1) * LANES],
                                  q_ref[r0:r0 + CHUNK, pb * LANES:(pb + 1) * LANES]], axis=0)
            k_even, k_odd = (ka, kb) if g == 0 else (kc, kd)
            v_even, v_odd = (vt, vs) if g == 0 else (vs, vt)
            outs = []
            for par, kbuf, vbuf in ((0, k_even, v_even), (1, k_odd, v_odd)):
                s = _dot_nt(qs, kbuf[r0:r0 + n_keys, :]) * (A_HEAD_DIM ** -0.5)
                if not halo_valid:
                    s = jnp.where(valid, s, NEG)
                sk = jnp.where(first, sink_ref[4 * g + par], sink_ref[4 * g + 2 + par])
                mx = jnp.maximum(jnp.max(s, axis=-1, keepdims=True), sk)
                p = jnp.exp(s - mx)
                den = jnp.sum(p, axis=-1, keepdims=True) + jnp.exp(sk - mx)
                outs.append(_dot(p.astype(BF16), vbuf[r0:r0 + n_keys, :]) / den)
            o = jnp.where(lane_o < A_HEAD_DIM, outs[0], outs[1]).astype(BF16)
            mix_ref[r0:r0 + CHUNK, pa * LANES:(pa + 1) * LANES] = o[:CHUNK]
            mix_ref[r0:r0 + CHUNK, pb * LANES:(pb + 1) * LANES] = o[CHUNK:]

    for buf in (ka, kb, kc, kd, vt, vs):
        buf[0:WINDOW, :] = buf[L:L + WINDOW, :]

    gates = gt_ref[...]
    ri = lax.broadcasted_iota(jnp.int32, (L, L), 0)
    ci_ = lax.broadcasted_iota(jnp.int32, (L, L), 1)
    causal = ci_ <= ri
    tril = causal.astype(BF16)
    hi, mid, lo = _split3(gates)
    bc = _dot(tril, hi) + _dot(tril, mid) + _dot(tril, lo)
    gates_t = gates.T
    bc_t = bc.T
    for h in range(M_HEADS):
        b_col = bc[:, M_HEADS + h:M_HEADS + h + 1]
        i_col = gates[:, h:h + 1]
        b_row = bc_t[M_HEADS + h:M_HEADS + h + 1, :]
        i_row = gates_t[h:h + 1, :]
        m_prev = nm_ref[0, M_HEADS + h:M_HEADS + h + 1, 0:1]
        n_row = nm_ref[0, h:h + 1, :]
        c_old = c_ref[0, h]
        qh = m4_ref[:, h * M_HEAD_DIM:(h + 1) * M_HEAD_DIM]
        kh = m4_ref[:, M_W + h * M_HEAD_DIM:M_W + (h + 1) * M_HEAD_DIM]
        vh = m4_ref[:, 2 * M_W + h * M_HEAD_DIM:2 * M_W + (h + 1) * M_HEAD_DIM]
        oh = m4_ref[:, 3 * M_W + h * M_HEAD_DIM:3 * M_W + (h + 1) * M_HEAD_DIM]

        d = jnp.where(causal, b_col - b_row + i_row, NEG)
        inter = b_col + m_prev
        mt = jnp.maximum(inter, jnp.max(d, axis=-1, keepdims=True))
        s = _dot_nt(qh, kh) * jnp.exp(d - mt)
        w_inter = jnp.exp(inter - mt)
        num = _dot(s.astype(BF16), vh) + w_inter * _dot_nt(qh, c_old.astype(BF16))
        den = (jnp.sum(s, axis=-1, keepdims=True)
               + w_inter * jnp.sum(qh.astype(F32) * n_row, axis=-1, keepdims=True))
        hh = num / jnp.maximum(jnp.abs(den), jnp.exp(-mt))
        hn = _rms(hh, mg_ref[:, h * M_HEAD_DIM:(h + 1) * M_HEAD_DIM])
        out = hn * jax.nn.sigmoid(oh.astype(F32))
        mix_ref[:, A_Q + h * M_HEAD_DIM:A_Q + (h + 1) * M_HEAD_DIM] = out.astype(BF16)

        m_last = mt[L - 1:L, :]
        b_last = b_col[L - 1:L, :]
        ws = jnp.exp(b_last - b_col + i_col - m_last)
        decay = jnp.exp(b_last + m_prev - m_last)
        vw_t = (vh.astype(F32) * ws).T.astype(BF16)
        c_ref[0, h] = decay * c_old + _dot(vw_t, kh)
        nm_ref[0, h:h + 1, :] = decay * n_row + jnp.sum(kh.astype(F32) * ws, axis=0, keepdims=True)
        nm_ref[0, M_HEADS + h:M_HEADS + h + 1, :] = jnp.broadcast_to(m_last, (1, LANES))


def _mixer(q, k, v, m4, gt, halo_k, halo_v, c0, nm0, sink, mg, *, batch, L, halo_valid):
    n = q.shape[0]
    steps = n // (batch * L)
    row = lambda b, t: (b * steps + t, 0)
    per_b3 = lambda b, t: (b, 0, 0)
    per_b4 = lambda b, t: (b, 0, 0, 0)
    kv_scratch = pltpu.VMEM((WINDOW + L, LANES), BF16)
    return pl.pallas_call(
        functools.partial(_mixer_kernel, L=L, halo_valid=halo_valid),
        grid=(batch, steps),
        in_specs=[
            pl.BlockSpec(memory_space=pltpu.SMEM),
            pl.BlockSpec((L, A_Q), row),
            pl.BlockSpec((L, A_KV), row),
            pl.BlockSpec((L, A_KV), row),
            pl.BlockSpec((L, 4 * M_W), row),
            pl.BlockSpec((L, GATE_COLS), row),
            pl.BlockSpec((1, WINDOW, A_KV), per_b3),
            pl.BlockSpec((1, WINDOW, A_KV), per_b3),
            pl.BlockSpec((1, M_HEADS, M_HEAD_DIM, M_HEAD_DIM), per_b4),
            pl.BlockSpec((1, 2 * M_HEADS, LANES), per_b3),
            pl.BlockSpec((1, M_W), lambda b, t: (0, 0)),
        ],
        out_specs=[
            pl.BlockSpec((L, D_MODEL), row),
            pl.BlockSpec((1, M_HEADS, M_HEAD_DIM, M_HEAD_DIM), per_b4),
            pl.BlockSpec((1, 2 * M_HEADS, LANES), per_b3),
        ],
        out_shape=[
            jax.ShapeDtypeStruct((n, D_MODEL), BF16),
            jax.ShapeDtypeStruct((batch, M_HEADS, M_HEAD_DIM, M_HEAD_DIM), F32),
            jax.ShapeDtypeStruct((batch, 2 * M_HEADS, LANES), F32),
        ],
        scratch_shapes=[kv_scratch] * 6,
        compiler_params=pltpu.CompilerParams(
            dimension_semantics=("arbitrary", "arbitrary"), vmem_limit_bytes=VMEM_LIMIT),
        name="mixer",
    )(sink, q, k, v, m4, gt, halo_k, halo_v, c0, nm0, mg)


def _ffn_kernel(*refs, final):
    if final:
        x_ref, mix_ref, wo_ref, g2_ref, wu_ref, wd_ref, fg_ref, out_ref = refs
    else:
        x_ref, mix_ref, wo_ref, g2_ref, wu_ref, wd_ref, out_ref = refs
    x1 = x_ref[...] + _dot(mix_ref[...], wo_ref[...])
    xn = _rms(x1, g2_ref[...]).astype(BF16)
    acc = x1
    for c in range(D_FF // FF_CHUNK):
        u = _dot(xn, wu_ref[:, c * FF_CHUNK:(c + 1) * FF_CHUNK])
        a = jnp.square(jnp.maximum(u, 0.0)).astype(BF16)
        acc = acc + _dot(a, wd_ref[c * FF_CHUNK:(c + 1) * FF_CHUNK, :])
    if final:
        acc = _rms(acc, fg_ref[...])
    out_ref[...] = acc


def _ffn(x, mix, wo, g2, wu, wd, fg, tm):
    n = x.shape[0]
    final = fg is not None
    row = lambda i: (i, 0)
    fixed = lambda i: (0, 0)
    single = pl.Buffered(1)
    in_specs = [
        pl.BlockSpec((tm, D_MODEL), row),
        pl.BlockSpec((tm, D_MODEL), row),
        pl.BlockSpec((D_MODEL, D_MODEL), fixed, pipeline_mode=single),
        pl.BlockSpec((1, D_MODEL), fixed),
        pl.BlockSpec((D_MODEL, D_FF), fixed, pipeline_mode=single),
        pl.BlockSpec((D_FF, D_MODEL), fixed, pipeline_mode=single),
    ]
    args = [x, mix, wo, g2, wu, wd]
    if final:
        in_specs.append(pl.BlockSpec((1, D_MODEL), fixed))
        args.append(fg)
    return pl.pallas_call(
        functools.partial(_ffn_kernel, final=final),
        grid=(n // tm,),
        in_specs=in_specs,
        out_specs=pl.BlockSpec((tm, D_MODEL), row),
        out_shape=jax.ShapeDtypeStruct((n, D_MODEL), F32),
        compiler_params=pltpu.CompilerParams(
            dimension_semantics=("arbitrary",), vmem_limit_bytes=VMEM_LIMIT),
        name="ffn_final" if final else "ffn",
    )(*args)


def _rotary_tables(pos, rows):
    inv = ROPE_THETA ** (-(jnp.arange(ROPE_HALF, dtype=F32) * 2.0 / ROPE_DIM))
    ang = pos[:, None] * inv[None, :]
    cos, sin = jnp.cos(ang), jnp.sin(ang)
    n = pos.shape[0]
    pad = jnp.zeros((n, A_HEAD_DIM - ROPE_DIM), F32)
    zeros = jnp.zeros_like(sin)
    cos_h = jnp.concatenate([cos, cos, pad + 1.0], axis=1)
    sa_h = jnp.concatenate([-sin, zeros, pad], axis=1)
    sb_h = jnp.concatenate([zeros, sin, pad], axis=1)
    rep = LANES // A_HEAD_DIM
    tile = lambda a: jnp.tile(a, (rows // n, rep))
    return tile(cos_h), tile(sa_h), tile(sb_h)


def _pad_w_in(w):
    return jnp.pad(w, ((0, 0), (0, IN_COLS_PAD - w.shape[1]))).astype(BF16)


def kernel(x_prompt, x_sample, cache_k, cache_v, state_C, state_n, state_m, norm1_g, w_in, gate_b,
           attn_sink, mnorm_g, w_out, norm2_g, w_up, w_down, final_g):
    bp, T, _ = x_prompt.shape
    bs, S, _ = x_sample.shape
    depth = w_in.shape[0]
    assert T % PROMPT_MLSTM_CHUNK == 0 and S == CHUNK
    assert (bp * T) % TOKEN_TILE == 0 and (bs * S) % TOKEN_TILE == 0

    tabs_p = _rotary_tables(jnp.arange(T, dtype=F32), max(T, TOKEN_TILE))
    tabs_s = _rotary_tables(jnp.arange(S, dtype=F32) + float(PAST_LEN), max(S, TOKEN_TILE))

    zeros_halo = jnp.zeros((bp, WINDOW, A_KV), F32)
    zeros_c = jnp.zeros((bp, M_HEADS, M_HEAD_DIM, M_HEAD_DIM), F32)
    zeros_nm = jnp.zeros((bp, 2 * M_HEADS, LANES), F32)

    yp = x_prompt.reshape(bp * T, D_MODEL)
    ys = x_sample.reshape(bs * S, D_MODEL)
    outs = {name: [] for name in ("pk", "pv", "pC", "pn", "pm", "sk", "sv", "sC", "sn", "sm")}
    fg = final_g.reshape(1, D_MODEL)
    for l in range(depth):
        g1 = norm1_g[l].reshape(1, D_MODEL)
        g2 = norm2_g[l].reshape(1, D_MODEL)
        w1 = _pad_w_in(w_in[l])
        gb = jnp.pad(gate_b[l], (0, GATE_COLS - 2 * M_HEADS)).reshape(1, GATE_COLS)
        mg = mnorm_g[l].reshape(1, M_W)
        wo = w_out[l].astype(BF16)
        wu = w_up[l].astype(BF16)
        wd = w_down[l].astype(BF16)
        last = fg if l == depth - 1 else None

        q, k, v, m4, gt = _inproj(yp, g1, w1, tabs_p, gb, TOKEN_TILE)
        mix, C, nm = _mixer(q, k, v, m4, gt, zeros_halo, zeros_halo, zeros_c, zeros_nm,
                            attn_sink[l], mg, batch=bp, L=PROMPT_MLSTM_CHUNK, halo_valid=False)
        yp = _ffn(yp, mix, wo, g2, wu, wd, last, TOKEN_TILE)
        outs["pk"].append(k.reshape(bp, T, A_KV_HEADS, A_HEAD_DIM)[:, T - WINDOW:])
        outs["pv"].append(v.reshape(bp, T, A_KV_HEADS, A_HEAD_DIM)[:, T - WINDOW:])
        outs["pC"].append(C)
        outs["pn"].append(nm[:, :M_HEADS, :])
        outs["pm"].append(nm[:, M_HEADS:, 0])

        q, k, v, m4, gt = _inproj(ys, g1, w1, tabs_s, gb, TOKEN_TILE)
        nm0 = jnp.concatenate(
            [state_n[l], jnp.broadcast_to(state_m[l][:, :, None], (bs, M_HEADS, LANES))], axis=1)
        mix, C, nm = _mixer(q, k, v, m4, gt,
                            cache_k[l].reshape(bs, WINDOW, A_KV), cache_v[l].reshape(bs, WINDOW, A_KV),
                            state_C[l], nm0, attn_sink[l], mg, batch=bs, L=S, halo_valid=True)
        ys = _ffn(ys, mix, wo, g2, wu, wd, last, TOKEN_TILE)
        outs["sk"].append(k.reshape(bs, S, A_KV_HEADS, A_HEAD_DIM))
        outs["sv"].append(v.reshape(bs, S, A_KV_HEADS, A_HEAD_DIM))
        outs["sC"].append(C)
        outs["sn"].append(nm[:, :M_HEADS, :])
        outs["sm"].append(nm[:, M_HEADS:, 0])

    st = lambda name: jnp.stack(outs[name])
    return (yp.reshape(bp, T, D_MODEL), ys.reshape(bs, S, D_MODEL),
            st("pk"), st("pv"), st("pC"), st("pn"), st("pm"),
            st("sk"), st("sv"), st("sC"), st("sn"), st("sm"))
```

```python
import functools

import jax
import jax.numpy as jnp
from jax import lax
from jax.experimental import pallas as pl
from jax.experimental.pallas import tpu as pltpu

D_MODEL = 1024
CHUNK = 64
A_HEADS = 8
A_KV_HEADS = 2
A_HEAD_DIM = 64
WINDOW = 128
ROPE_THETA = 500000.0
ROPE_DIM = A_HEAD_DIM // 4
ROPE_HALF = ROPE_DIM // 2
M_HEADS = 4
M_HEAD_DIM = 128
D_FF = 4 * D_MODEL
NORM_EPS = 1e-6
NEG = -1e30
PAST_LEN = 4096
A_Q = A_HEADS * A_HEAD_DIM
A_KV = A_KV_HEADS * A_HEAD_DIM
M_W = M_HEADS * M_HEAD_DIM

LANES = 128
GATE_COLS = LANES
COL_KV = A_Q
COL_M = A_Q + 2 * A_KV
COL_G = COL_M + 4 * M_W
IN_COLS_PAD = COL_G + GATE_COLS

TOKEN_TILE = 512
PROMPT_MLSTM_CHUNK = 256
FF_CHUNK = 1024
VMEM_LIMIT = 56 * 1024 * 1024

BF16 = jnp.bfloat16
F32 = jnp.float32


def _dot(a, b):
    return jnp.dot(a, b, preferred_element_type=F32)


def _dot_nt(a, b):
    return lax.dot_general(a, b, (((1,), (1,)), ((), ())), preferred_element_type=F32)


def _rms(x, g):
    r = lax.rsqrt(jnp.mean(x * x, axis=-1, keepdims=True) + NORM_EPS)
    return x * r * g


def _inproj_kernel(x_ref, g_ref, w_ref, cos_ref, sa_ref, sb_ref, gb_ref,
                   q_ref, k_ref, v_ref, m4_ref, gt_ref):
    h = _rms(x_ref[...], g_ref[...]).astype(BF16)
    cos, sa, sb = cos_ref[...], sa_ref[...], sb_ref[...]

    def rotary(z):
        left = pltpu.roll(z, LANES - ROPE_HALF, axis=1)
        right = pltpu.roll(z, ROPE_HALF, axis=1)
        return z * cos + left * sa + right * sb

    for j in range(A_Q // LANES):
        zq = _dot(h, w_ref[:, j * LANES:(j + 1) * LANES])
        q_ref[:, j * LANES:(j + 1) * LANES] = rotary(zq).astype(BF16)
    zkv = _dot(h, w_ref[:, COL_KV:COL_KV + 2 * A_KV])
    k_ref[...] = rotary(zkv[:, :A_KV])
    v_ref[...] = zkv[:, A_KV:]
    for j in range(4):
        zm = _dot(h, w_ref[:, COL_M + j * M_W:COL_M + (j + 1) * M_W])
        if j == 1:
            zm = zm * (M_HEAD_DIM ** -0.5)
        m4_ref[:, j * M_W:(j + 1) * M_W] = zm.astype(BF16)
    zg = _dot(h, w_ref[:, COL_G:COL_G + GATE_COLS]) + gb_ref[...]
    logsig = jnp.minimum(zg, 0.0) - jnp.log1p(jnp.exp(-jnp.abs(zg)))
    lane = lax.broadcasted_iota(jnp.int32, zg.shape, 1)
    gt_ref[...] = jnp.where(lane < M_HEADS, zg, logsig)


def _inproj(x, g1, w, tabs, gb, tm):
    n = x.shape[0]
    cos, sa, sb = tabs
    nt = cos.shape[0] // tm
    row = lambda i: (i, 0)
    fixed = lambda i: (0, 0)
    tab = lambda i: (i % nt, 0)
    return pl.pallas_call(
        _inproj_kernel,
        grid=(n // tm,),
        in_specs=[
            pl.BlockSpec((tm, D_MODEL), row),
            pl.BlockSpec((1, D_MODEL), fixed),
            pl.BlockSpec((D_MODEL, IN_COLS_PAD), fixed),
            pl.BlockSpec((tm, LANES), tab),
            pl.BlockSpec((tm, LANES), tab),
            pl.BlockSpec((tm, LANES), tab),
            pl.BlockSpec((1, GATE_COLS), fixed),
        ],
        out_specs=[
            pl.BlockSpec((tm, A_Q), row),
            pl.BlockSpec((tm, A_KV), row),
            pl.BlockSpec((tm, A_KV), row),
            pl.BlockSpec((tm, 4 * M_W), row),
            pl.BlockSpec((tm, GATE_COLS), row),
        ],
        out_shape=[
            jax.ShapeDtypeStruct((n, A_Q), BF16),
            jax.ShapeDtypeStruct((n, A_KV), F32),
            jax.ShapeDtypeStruct((n, A_KV), F32),
            jax.ShapeDtypeStruct((n, 4 * M_W), BF16),
            jax.ShapeDtypeStruct((n, GATE_COLS), F32),
        ],
        compiler_params=pltpu.CompilerParams(
            dimension_semantics=("arbitrary",), vmem_limit_bytes=VMEM_LIMIT),
        name="inproj",
    )(x, g1, w, cos, sa, sb, gb)


def _split3(x):
    hi = x.astype(BF16)
    r1 = x - hi.astype(F32)
    mid = r1.astype(BF16)
    lo = (r1 - mid.astype(F32)).astype(BF16)
    return hi, mid, lo


def _mixer_kernel(sink_ref, q_ref, k_ref, v_ref, m4_ref, gt_ref, hk_ref, hv_ref, c0_ref, nm0_ref,
                  mg_ref, mix_ref, c_ref, nm_ref,
                  ka, kb, kc, kd, vt, vs, *, L, halo_valid):
    t = pl.program_id(1)
    n_chunks = L // CHUNK
    lane_k = lax.broadcasted_iota(jnp.int32, (1, LANES), 1)
    low = lane_k < A_HEAD_DIM

    def put_kv(rows, k, v):
        ksw = pltpu.roll(k, A_HEAD_DIM, axis=1)
        zero = jnp.zeros_like(k)
        ka[rows, :] = jnp.where(low, k, zero).astype(BF16)
        kd[rows, :] = jnp.where(low, zero, k).astype(BF16)
        kc[rows, :] = jnp.where(low, ksw, zero).astype(BF16)
        kb[rows, :] = jnp.where(low, zero, ksw).astype(BF16)
        vt[rows, :] = v.astype(BF16)
        vs[rows, :] = pltpu.roll(v, A_HEAD_DIM, axis=1).astype(BF16)

    @pl.when(t == 0)
    def _():
        put_kv(pl.ds(0, WINDOW), hk_ref[0], hv_ref[0])
        c_ref[...] = c0_ref[...]
        nm_ref[...] = nm0_ref[...]

    put_kv(pl.ds(WINDOW, L), k_ref[...], v_ref[...])

    n_keys = WINDOW + CHUNK
    row2 = lax.broadcasted_iota(jnp.int32, (2 * CHUNK, 1), 0)
    first = row2 < CHUNK
    lane_o = lax.broadcasted_iota(jnp.int32, (2 * CHUNK, LANES), 1)
    key_col = lax.broadcasted_iota(jnp.int32, (1, n_keys), 1)
    blocks = [(ci, g, par) for ci in range(n_chunks) for g in range(A_KV_HEADS) for par in range(2)]

    def kv_bufs(g, par):
        kbuf = ((ka, kb), (kc, kd))[g][par]
        vbuf = ((vt, vs), (vs, vt))[g][par]
        return kbuf, vbuf

    scores = {}
    for ci, g, par in blocks:
        r0 = ci * CHUNK
        pa, pb = 2 * g, 2 * g + 1
        qs = jnp.concatenate([q_ref[r0:r0 + CHUNK, pa * LANES:(pa + 1) * LANES],
                              q_ref[r0:r0 + CHUNK, pb * LANES:(pb + 1) * LANES]], axis=0)
        scores[ci, g, par] = _dot_nt(qs, kv_bufs(g, par)[0][r0:r0 + n_keys, :])

    gates = gt_ref[...]
    ri = lax.broadcasted_iota(jnp.int32, (L, L), 0)
    ci_ = lax.broadcasted_iota(jnp.int32, (L, L), 1)
    causal = ci_ <= ri
    tril = causal.astype(BF16)
    hi, mid, lo = _split3(gates)
    bc = _dot(tril, hi) + _dot(tril, mid) + _dot(tril, lo)
    heads = range(M_HEADS)
    m_slice = lambda j, h: m4_ref[:, j * M_W + h * M_HEAD_DIM:j * M_W + (h + 1) * M_HEAD_DIM]
    c_old = [c_ref[0, h] for h in heads]
    qk = [_dot_nt(m_slice(0, h), m_slice(1, h)) for h in heads]
    qc = [_dot_nt(m_slice(0, h), c_old[h].astype(BF16)) for h in heads]

    probs, dens = {}, {}
    for ci, g, par in blocks:
        s = scores[ci, g, par] * (A_HEAD_DIM ** -0.5)
        if not halo_valid:
            chunk_idx = t * n_chunks + ci
            valid = (chunk_idx - WINDOW // CHUNK) * CHUNK + key_col >= 0
            s = jnp.where(valid, s, NEG)
        sk = jnp.where(first, sink_ref[4 * g + par], sink_ref[4 * g + 2 + par])
        mx = jnp.maximum(jnp.max(s, axis=-1, keepdims=True), sk)
        p = jnp.exp(s - mx)
        dens[ci, g, par] = jnp.sum(p, axis=-1, keepdims=True) + jnp.exp(sk - mx)
        probs[ci, g, par] = p.astype(BF16)

    gates_t = gates.T
    bc_t = bc.T
    s_bf, s_sum, w_int, mts, n_rows, vw_ts, decays, kws = [], [], [], [], [], [], [], []
    for h in heads:
        b_col = bc[:, M_HEADS + h:M_HEADS + h + 1]
        i_col = gates[:, h:h + 1]
        b_row = bc_t[M_HEADS + h:M_HEADS + h + 1, :]
        i_row = gates_t[h:h + 1, :]
        m_prev = nm_ref[0, M_HEADS + h:M_HEADS + h + 1, 0:1]
        d = jnp.where(causal, b_col - b_row + i_row, NEG)
        inter = b_col + m_prev
        mt = jnp.maximum(inter, jnp.max(d, axis=-1, keepdims=True))
        s = qk[h] * jnp.exp(d - mt)
        s_bf.append(s.astype(BF16))
        s_sum.append(jnp.sum(s, axis=-1, keepdims=True))
        w_int.append(jnp.exp(inter - mt))
        mts.append(mt)
        n_rows.append(nm_ref[0, h:h + 1, :])
        m_last = mt[L - 1:L, :]
        b_last = b_col[L - 1:L, :]
        ws = jnp.exp(b_last - b_col + i_col - m_last)
        decays.append(jnp.exp(b_last + m_prev - m_last))
        vw_ts.append((m_slice(2, h).astype(F32) * ws).T.astype(BF16))
        kws.append(jnp.sum(m_slice(1, h).astype(F32) * ws, axis=0, keepdims=True))

    pv = {b: _dot(probs[b], kv_bufs(b[1], b[2])[1][b[0] * CHUNK:b[0] * CHUNK + n_keys, :])
          for b in blocks}
    sv = [_dot(s_bf[h], m_slice(2, h)) for h in heads]
    c_upd = [_dot(vw_ts[h], m_slice(1, h)) for h in heads]

    for ci in range(n_chunks):
        r0 = ci * CHUNK
        for g in range(A_KV_HEADS):
            pa, pb = 2 * g, 2 * g + 1
            o = jnp.where(lane_o < A_HEAD_DIM, pv[ci, g, 0] / dens[ci, g, 0],
                          pv[ci, g, 1] / dens[ci, g, 1]).astype(BF16)
            mix_ref[r0:r0 + CHUNK, pa * LANES:(pa + 1) * LANES] = o[:CHUNK]
            mix_ref[r0:r0 + CHUNK, pb * LANES:(pb + 1) * LANES] = o[CHUNK:]
    for h in heads:
        num = sv[h] + w_int[h] * qc[h]
        den = s_sum[h] + w_int[h] * jnp.sum(m_slice(0, h).astype(F32) * n_rows[h],
                                            axis=-1, keepdims=True)
        hh = num / jnp.maximum(jnp.abs(den), jnp.exp(-mts[h]))
        hn = _rms(hh, mg_ref[:, h * M_HEAD_DIM:(h + 1) * M_HEAD_DIM])
        out = hn * jax.nn.sigmoid(m_slice(3, h).astype(F32))
        mix_ref[:, A_Q + h * M_HEAD_DIM:A_Q + (h + 1) * M_HEAD_DIM] = out.astype(BF16)
        c_ref[0, h] = decays[h] * c_old[h] + c_upd[h]
        nm_ref[0, h:h + 1, :] = decays[h] * n_rows[h] + kws[h]
        nm_ref[0, M_HEADS + h:M_HEADS + h + 1, :] = jnp.broadcast_to(mts[h][L - 1:L, :], (1, LANES))

    for buf in (ka, kb, kc, kd, vt, vs):
        buf[0:WINDOW, :] = buf[L:L + WINDOW, :]


def _mixer(q, k, v, m4, gt, halo_k, halo_v, c0, nm0, sink, mg, *, batch, L, halo_valid):
    n = q.shape[0]
    steps = n // (batch * L)
    row = lambda b, t: (b * steps + t, 0)
    per_b3 = lambda b, t: (b, 0, 0)
    per_b4 = lambda b, t: (b, 0, 0, 0)
    kv_scratch = pltpu.VMEM((WINDOW + L, LANES), BF16)
    return pl.pallas_call(
        functools.partial(_mixer_kernel, L=L, halo_valid=halo_valid),
        grid=(batch, steps),
        in_specs=[
            pl.BlockSpec(memory_space=pltpu.SMEM),
            pl.BlockSpec((L, A_Q), row),
            pl.BlockSpec((L, A_KV), row),
            pl.BlockSpec((L, A_KV), row),
            pl.BlockSpec((L, 4 * M_W), row),
            pl.BlockSpec((L, GATE_COLS), row),
            pl.BlockSpec((1, WINDOW, A_KV), per_b3),
            pl.BlockSpec((1, WINDOW, A_KV), per_b3),
            pl.BlockSpec((1, M_HEADS, M_HEAD_DIM, M_HEAD_DIM), per_b4),
            pl.BlockSpec((1, 2 * M_HEADS, LANES), per_b3),
            pl.BlockSpec((1, M_W), lambda b, t: (0, 0)),
        ],
        out_specs=[
            pl.BlockSpec((L, D_MODEL), row),
            pl.BlockSpec((1, M_HEADS, M_HEAD_DIM, M_HEAD_DIM), per_b4),
            pl.BlockSpec((1, 2 * M_HEADS, LANES), per_b3),
        ],
        out_shape=[
            jax.ShapeDtypeStruct((n, D_MODEL), BF16),
            jax.ShapeDtypeStruct((batch, M_HEADS, M_HEAD_DIM, M_HEAD_DIM), F32),
            jax.ShapeDtypeStruct((batch, 2 * M_HEADS, LANES), F32),
        ],
        scratch_shapes=[kv_scratch] * 6,
        compiler_params=pltpu.CompilerParams(
            dimension_semantics=("arbitrary", "arbitrary"), vmem_limit_bytes=VMEM_LIMIT),
        name="mixer",
    )(sink, q, k, v, m4, gt, halo_k, halo_v, c0, nm0, mg)


def _ffn_kernel(*refs, final):
    if final:
        x_ref, mix_ref, wo_ref, g2_ref, wu_ref, wd_ref, fg_ref, out_ref = refs
    else:
        x_ref, mix_ref, wo_ref, g2_ref, wu_ref, wd_ref, out_ref = refs
    x1 = x_ref[...] + _dot(mix_ref[...], wo_ref[...])
    xn = _rms(x1, g2_ref[...]).astype(BF16)
    acc = x1
    for c in range(D_FF // FF_CHUNK):
        u = _dot(xn, wu_ref[:, c * FF_CHUNK:(c + 1) * FF_CHUNK])
        a = jnp.square(jnp.maximum(u, 0.0)).astype(BF16)
        acc = acc + _dot(a, wd_ref[c * FF_CHUNK:(c + 1) * FF_CHUNK, :])
    if final:
        acc = _rms(acc, fg_ref[...])
    out_ref[...] = acc


def _ffn(x, mix, wo, g2, wu, wd, fg, tm):
    n = x.shape[0]
    final = fg is not None
    row = lambda i: (i, 0)
    fixed = lambda i: (0, 0)
    single = pl.Buffered(1)
    in_specs = [
        pl.BlockSpec((tm, D_MODEL), row),
        pl.BlockSpec((tm, D_MODEL), row),
        pl.BlockSpec((D_MODEL, D_MODEL), fixed, pipeline_mode=single),
        pl.BlockSpec((1, D_MODEL), fixed),
        pl.BlockSpec((D_MODEL, D_FF), fixed, pipeline_mode=single),
        pl.BlockSpec((D_FF, D_MODEL), fixed, pipeline_mode=single),
    ]
    args = [x, mix, wo, g2, wu, wd]
    if final:
        in_specs.append(pl.BlockSpec((1, D_MODEL), fixed))
        args.append(fg)
    return pl.pallas_call(
        functools.partial(_ffn_kernel, final=final),
        grid=(n // tm,),
        in_specs=in_specs,
        out_specs=pl.BlockSpec((tm, D_MODEL), row),
        out_shape=jax.ShapeDtypeStruct((n, D_MODEL), F32),
        compiler_params=pltpu.CompilerParams(
            dimension_semantics=("arbitrary",), vmem_limit_bytes=VMEM_LIMIT),
        name="ffn_final" if final else "ffn",
    )(*args)


def _rotary_tables(pos, rows):
    inv = ROPE_THETA ** (-(jnp.arange(ROPE_HALF, dtype=F32) * 2.0 / ROPE_DIM))
    ang = pos[:, None] * inv[None, :]
    cos, sin = jnp.cos(ang), jnp.sin(ang)
    n = pos.shape[0]
    pad = jnp.zeros((n, A_HEAD_DIM - ROPE_DIM), F32)
    zeros = jnp.zeros_like(sin)
    cos_h = jnp.concatenate([cos, cos, pad + 1.0], axis=1)
    sa_h = jnp.concatenate([-sin, zeros, pad], axis=1)
    sb_h = jnp.concatenate([zeros, sin, pad], axis=1)
    rep = LANES // A_HEAD_DIM
    tile = lambda a: jnp.tile(a, (rows // n, rep))
    return tile(cos_h), tile(sa_h), tile(sb_h)


def _pad_w_in(w):
    return jnp.pad(w, ((0, 0), (0, IN_COLS_PAD - w.shape[1]))).astype(BF16)


def kernel(x_prompt, x_sample, cache_k, cache_v, state_C, state_n, state_m, norm1_g, w_in, gate_b,
           attn_sink, mnorm_g, w_out, norm2_g, w_up, w_down, final_g):
    bp, T, _ = x_prompt.shape
    bs, S, _ = x_sample.shape
    depth = w_in.shape[0]
    assert T % PROMPT_MLSTM_CHUNK == 0 and S == CHUNK
    assert (bp * T) % TOKEN_TILE == 0 and (bs * S) % TOKEN_TILE == 0

    tabs_p = _rotary_tables(jnp.arange(T, dtype=F32), max(T, TOKEN_TILE))
    tabs_s = _rotary_tables(jnp.arange(S, dtype=F32) + float(PAST_LEN), max(S, TOKEN_TILE))

    zeros_halo = jnp.zeros((bp, WINDOW, A_KV), F32)
    zeros_c = jnp.zeros((bp, M_HEADS, M_HEAD_DIM, M_HEAD_DIM), F32)
    zeros_nm = jnp.zeros((bp, 2 * M_HEADS, LANES), F32)

    yp = x_prompt.reshape(bp * T, D_MODEL)
    ys = x_sample.reshape(bs * S, D_MODEL)
    outs = {name: [] for name in ("pk", "pv", "pC", "pn", "pm", "sk", "sv", "sC", "sn", "sm")}
    fg = final_g.reshape(1, D_MODEL)
    for l in range(depth):
        g1 = norm1_g[l].reshape(1, D_MODEL)
        g2 = norm2_g[l].reshape(1, D_MODEL)
        w1 = _pad_w_in(w_in[l])
        gb = jnp.pad(gate_b[l], (0, GATE_COLS - 2 * M_HEADS)).reshape(1, GATE_COLS)
        mg = mnorm_g[l].reshape(1, M_W)
        wo = w_out[l].astype(BF16)
        wu = w_up[l].astype(BF16)
        wd = w_down[l].astype(BF16)
        last = fg if l == depth - 1 else None

        q, k, v, m4, gt = _inproj(yp, g1, w1, tabs_p, gb, TOKEN_TILE)
        mix, C, nm = _mixer(q, k, v, m4, gt, zeros_halo, zeros_halo, zeros_c, zeros_nm,
                            attn_sink[l], mg, batch=bp, L=PROMPT_MLSTM_CHUNK, halo_valid=False)
        yp = _ffn(yp, mix, wo, g2, wu, wd, last, TOKEN_TILE)
        outs["pk"].append(k.reshape(bp, T, A_KV_HEADS, A_HEAD_DIM)[:, T - WINDOW:])
        outs["pv"].append(v.reshape(bp, T, A_KV_HEADS, A_HEAD_DIM)[:, T - WINDOW:])
        outs["pC"].append(C)
        outs["pn"].append(nm[:, :M_HEADS, :])
        outs["pm"].append(nm[:, M_HEADS:, 0])

        q, k, v, m4, gt = _inproj(ys, g1, w1, tabs_s, gb, TOKEN_TILE)
        nm0 = jnp.concatenate(
            [state_n[l], jnp.broadcast_to(state_m[l][:, :, None], (bs, M_HEADS, LANES))], axis=1)
        mix, C, nm = _mixer(q, k, v, m4, gt,
                            cache_k[l].reshape(bs, WINDOW, A_KV), cache_v[l].reshape(bs, WINDOW, A_KV),
                            state_C[l], nm0, attn_sink[l], mg, batch=bs, L=S, halo_valid=True)
        ys = _ffn(ys, mix, wo, g2, wu, wd, last, TOKEN_TILE)
        outs["sk"].append(k.reshape(bs, S, A_KV_HEADS, A_HEAD_DIM))
        outs["sv"].append(v.reshape(bs, S, A_KV_HEADS, A_HEAD_DIM))
        outs["sC"].append(C)
        outs["sn"].append(nm[:, :M_HEADS, :])
        outs["sm"].append(nm[:, M_HEADS:, 0])

    st = lambda name: jnp.stack(outs[name])
    return (yp.reshape(bp, T, D_MODEL), ys.reshape(bs, S, D_MODEL),
            st("pk"), st("pv"), st("pC"), st("pn"), st("pm"),
            st("sk"), st("sv"), st("sC"), st("sn"), st("sm"))
```

```python
import functools

import jax
import jax.numpy as jnp
from jax import lax
from jax.experimental import pallas as pl
from jax.experimental.pallas import tpu as pltpu

D_MODEL = 1024
CHUNK = 64
A_HEADS = 8
A_KV_HEADS = 2
A_HEAD_DIM = 64
WINDOW = 128
ROPE_THETA = 500000.0
ROPE_DIM = A_HEAD_DIM // 4
ROPE_HALF = ROPE_DIM // 2
M_HEADS = 4
M_HEAD_DIM = 128
D_FF = 4 * D_MODEL
NORM_EPS = 1e-6
NEG = -1e30
PAST_LEN = 4096
A_Q = A_HEADS * A_HEAD_DIM
A_KV = A_KV_HEADS * A_HEAD_DIM
M_W = M_HEADS * M_HEAD_DIM

LANES = 128
SUBLANES = 8
Q_PAD = A_HEADS * LANES
GATE_COLS = 2 * LANES
COL_KV = A_Q
COL_M = A_Q + 2 * A_KV
COL_G = COL_M + 4 * M_W
IN_COLS_PAD = COL_G + GATE_COLS
Q_HEAD_ORDER = (0, 2, 5, 7, 1, 3, 4, 6)

TOKEN_TILE = 512
PROMPT_MLSTM_CHUNK = 256
FF_CHUNK = 1024
VMEM_LIMIT = 56 * 1024 * 1024

BF16 = jnp.bfloat16
F32 = jnp.float32


def _dot(a, b):
    return jnp.dot(a, b, preferred_element_type=F32)


def _dot_nt(a, b):
    return lax.dot_general(a, b, (((1,), (1,)), ((), ())), preferred_element_type=F32)


def _rms(x, g):
    r = lax.rsqrt(jnp.mean(x * x, axis=-1, keepdims=True) + NORM_EPS)
    return x * r * g


def _inproj_kernel(x_ref, g_ref, w_ref, cos_ref, sa_ref, sb_ref, gb_ref,
                   q_ref, k_ref, v_ref, m4_ref, gt_ref):
    h = _rms(x_ref[...], g_ref[...]).astype(BF16)
    cos, sa, sb = cos_ref[...], sa_ref[...], sb_ref[...]
    low = lax.broadcasted_iota(jnp.int32, (1, LANES), 1) < A_HEAD_DIM

    def rotary(z):
        left = pltpu.roll(z, LANES - ROPE_HALF, axis=1)
        right = pltpu.roll(z, ROPE_HALF, axis=1)
        return z * cos + left * sa + right * sb

    for pair in range(A_Q // LANES):
        zq = rotary(_dot(h, w_ref[:, pair * LANES:(pair + 1) * LANES])) * (A_HEAD_DIM ** -0.5)
        zsw = pltpu.roll(zq, A_HEAD_DIM, axis=1)
        zero = jnp.zeros_like(zq)
        if pair < A_HEADS // (2 * A_KV_HEADS):
            even, odd = jnp.where(low, zq, zero), jnp.where(low, zsw, zero)
        else:
            even, odd = jnp.where(low, zero, zsw), jnp.where(low, zero, zq)
        for head, val in ((2 * pair, even), (2 * pair + 1, odd)):
            j = Q_HEAD_ORDER.index(head)
            q_ref[:, j * LANES:(j + 1) * LANES] = val.astype(BF16)
    zkv = _dot(h, w_ref[:, COL_KV:COL_KV + 2 * A_KV])
    k_ref[...] = rotary(zkv[:, :A_KV])
    v_ref[...] = zkv[:, A_KV:]
    for j in range(4):
        zm = _dot(h, w_ref[:, COL_M + j * M_W:COL_M + (j + 1) * M_W])
        if j == 1:
            zm = zm * (M_HEAD_DIM ** -0.5)
        m4_ref[:, j * M_W:(j + 1) * M_W] = zm.astype(BF16)
    zg = _dot(h, w_ref[:, COL_G:COL_G + GATE_COLS]) + gb_ref[...]
    zf = zg[:, LANES:]
    gt_ref[:, :LANES] = zg[:, :LANES]
    gt_ref[:, LANES:] = jnp.minimum(zf, 0.0) - jnp.log1p(jnp.exp(-jnp.abs(zf)))


def _inproj(x, g1, w, tabs, gb, tm):
    n = x.shape[0]
    cos, sa, sb = tabs
    nt = cos.shape[0] // tm
    row = lambda i: (i, 0)
    fixed = lambda i: (0, 0)
    tab = lambda i: (i % nt, 0)
    return pl.pallas_call(
        _inproj_kernel,
        grid=(n // tm,),
        in_specs=[
            pl.BlockSpec((tm, D_MODEL), row),
            pl.BlockSpec((1, D_MODEL), fixed),
            pl.BlockSpec((D_MODEL, IN_COLS_PAD), fixed),
            pl.BlockSpec((tm, LANES), tab),
            pl.BlockSpec((tm, LANES), tab),
            pl.BlockSpec((tm, LANES), tab),
            pl.BlockSpec((1, GATE_COLS), fixed),
        ],
        out_specs=[
            pl.BlockSpec((tm, Q_PAD), row),
            pl.BlockSpec((tm, A_KV), row),
            pl.BlockSpec((tm, A_KV), row),
            pl.BlockSpec((tm, 4 * M_W), row),
            pl.BlockSpec((tm, GATE_COLS), row),
        ],
        out_shape=[
            jax.ShapeDtypeStruct((n, Q_PAD), BF16),
            jax.ShapeDtypeStruct((n, A_KV), F32),
            jax.ShapeDtypeStruct((n, A_KV), F32),
            jax.ShapeDtypeStruct((n, 4 * M_W), BF16),
            jax.ShapeDtypeStruct((n, GATE_COLS), F32),
        ],
        compiler_params=pltpu.CompilerParams(
            dimension_semantics=("arbitrary",), vmem_limit_bytes=VMEM_LIMIT),
        name="inproj",
    )(x, g1, w, cos, sa, sb, gb)


def _split3(x):
    hi = x.astype(BF16)
    r1 = x - hi.astype(F32)
    mid = r1.astype(BF16)
    lo = (r1 - mid.astype(F32)).astype(BF16)
    return hi, mid, lo


def _mixer_kernel(sink_ref, q_ref, k_ref, v_ref, m4_ref, gt_ref, hk_ref, hv_ref, c0_ref, m0_ref,
                  mg_ref, mix_ref, c_ref, m_ref, kbuf, vt, vs, *, L, steps, halo_valid):
    t = pl.program_id(1)
    n_chunks = L // CHUNK
    n_keys = WINDOW + CHUNK
    half = A_HEAD_DIM

    def put_vt(cols, v):
        v_t = v.T
        vt[:, cols] = v_t.astype(BF16)
        vs[:, cols] = jnp.concatenate([v_t[half:], v_t[:half]], axis=0).astype(BF16)

    @pl.when(t == 0)
    def _():
        vt[...] = jnp.zeros_like(vt)
        vs[...] = jnp.zeros_like(vs)
        kbuf[0:WINDOW, :] = hk_ref[0].astype(BF16)
        put_vt(pl.ds(0, WINDOW), hv_ref[0])
        c_ref[...] = c0_ref[...]
        m_ref[...] = m0_ref[...]

    kbuf[WINDOW:WINDOW + L, :] = k_ref[...].astype(BF16)
    put_vt(pl.ds(WINDOW, L), v_ref[...])

    heads = range(M_HEADS)
    m_slice = lambda j, h: m4_ref[:, j * M_W + h * M_HEAD_DIM:j * M_W + (h + 1) * M_HEAD_DIM]

    scores = []
    for ci in range(n_chunks):
        r0 = ci * CHUNK
        q8 = jnp.concatenate([q_ref[r0:r0 + CHUNK, j * LANES:(j + 1) * LANES]
                              for j in range(A_HEADS)], axis=0)
        scores.append(_dot_nt(kbuf[r0:r0 + n_keys, :], q8))

    gates_i = gt_ref[:, :LANES]
    gates_f = gt_ref[:, LANES:]
    ri = lax.broadcasted_iota(jnp.int32, (L, L), 0)
    ci_ = lax.broadcasted_iota(jnp.int32, (L, L), 1)
    causal = ci_ <= ri
    b3 = _dot(causal.astype(BF16), jnp.concatenate(_split3(gates_f), axis=1))
    b = b3[:, :LANES] + b3[:, LANES:2 * LANES] + b3[:, 2 * LANES:]
    c_old = [c_ref[0, h] for h in heads]
    qk = [_dot_nt(m_slice(0, h), m_slice(1, h)) for h in heads]
    qc = [_dot_nt(m_slice(0, h), c_old[h].astype(BF16)) for h in heads]

    lane_q = lax.broadcasted_iota(jnp.int32, (1, A_HEADS * CHUNK), 1) // CHUNK
    sink_row = jnp.zeros((1, A_HEADS * CHUNK), F32)
    for j, head in enumerate(Q_HEAD_ORDER):
        sink_row = jnp.where(lane_q == j, sink_ref[head], sink_row)
    key_row = lax.broadcasted_iota(jnp.int32, (n_keys, A_HEADS * CHUNK), 0)
    probs, rdens = [], []
    for ci in range(n_chunks):
        s = scores[ci]
        if not halo_valid and ci < WINDOW // CHUNK:
            first_key = (t * n_chunks + ci - WINDOW // CHUNK) * CHUNK
            s = jnp.where(key_row + first_key >= 0, s, NEG)
        mx = jnp.maximum(jnp.max(s, axis=0, keepdims=True), sink_row)
        p = jnp.exp(s - mx)
        rdens.append(1.0 / (jnp.sum(p, axis=0, keepdims=True) + jnp.exp(sink_row - mx)))
        pad = jnp.zeros((CHUNK, A_HEADS * CHUNK), BF16)
        pb = p.astype(BF16)
        probs.append(jnp.concatenate([pb, pad] if ci % 2 == 0 else [pad, pb], axis=0))

    a = gates_i - b
    row_l = lax.broadcasted_iota(jnp.int32, (L, LANES), 0)
    cm = a
    shift = 1
    while shift < L:
        cm = jnp.maximum(cm, jnp.where(row_l >= shift, pltpu.roll(cm, shift, axis=0), NEG))
        shift *= 2
    m_prev = m_ref[0, 0:1, :]
    c = jnp.maximum(cm, m_prev)
    mt = c + b
    w_inter = jnp.exp(m_prev - c)
    e_neg_mt = jnp.exp(-mt)
    m_last = mt[L - 1:L, :]
    b_last = b[L - 1:L, :]
    delta = b_last - m_last
    decay = jnp.exp(b_last + m_prev - m_last)
    a_t = a.T
    ones = jnp.ones((L, M_HEAD_DIM), BF16)
    s_bf, upd_lhs = [], []
    for h in heads:
        a_row = a_t[h:h + 1, :]
        e = jnp.exp(jnp.where(causal, a_row - c[:, h:h + 1], NEG))
        s_bf.append((qk[h] * e).astype(BF16))
        ws_row = jnp.exp(a_row + delta[:, h:h + 1])
        v_t = m_slice(2, h).astype(F32).T
        upd_lhs.append(jnp.concatenate(
            [v_t * ws_row, jnp.broadcast_to(ws_row, (M_HEAD_DIM, L))], axis=0).astype(BF16))

    pv = []
    for ci in range(n_chunks):
        c0 = ci * CHUNK if ci % 2 == 0 else (ci - 1) * CHUNK
        win = slice(c0, c0 + 2 * LANES)
        pv.append((_dot(vt[:, win], probs[ci][:, :2 * LANES]),
                   _dot(vs[:, win], probs[ci][:, 2 * LANES:])))
    sv = [_dot(s_bf[h], jnp.concatenate([m_slice(2, h), ones], axis=1)) for h in heads]
    c_upd = [_dot(upd_lhs[h], m_slice(1, h)) for h in heads]

    for ci in range(n_chunks):
        r0 = ci * CHUNK
        ra = pv[ci][0] * rdens[ci][:, :2 * LANES]
        rb = pv[ci][1] * rdens[ci][:, 2 * LANES:]
        o0 = jnp.concatenate([ra[:half, :LANES], rb[half:, :LANES]], axis=0).T.astype(BF16)
        o1 = jnp.concatenate([rb[:half, LANES:], ra[half:, LANES:]], axis=0).T.astype(BF16)
        mix_ref[r0:r0 + CHUNK, 0 * LANES:1 * LANES] = o0[:CHUNK]
        mix_ref[r0:r0 + CHUNK, 1 * LANES:2 * LANES] = o0[CHUNK:]
        mix_ref[r0:r0 + CHUNK, 2 * LANES:3 * LANES] = o1[:CHUNK]
        mix_ref[r0:r0 + CHUNK, 3 * LANES:4 * LANES] = o1[CHUNK:]
    for h in heads:
        tot = sv[h] + w_inter[:, h:h + 1] * qc[h]
        hh = tot[:, :M_HEAD_DIM] / jnp.maximum(jnp.abs(tot[:, M_HEAD_DIM:]), e_neg_mt[:, h:h + 1])
        hn = _rms(hh, mg_ref[:, h * M_HEAD_DIM:(h + 1) * M_HEAD_DIM])
        out = hn * jax.nn.sigmoid(m_slice(3, h).astype(F32))
        mix_ref[:, A_Q + h * M_HEAD_DIM:A_Q + (h + 1) * M_HEAD_DIM] = out.astype(BF16)
        c_ref[0, h] = decay[:, h:h + 1] * c_old[h] + c_upd[h]
    m_ref[0] = jnp.broadcast_to(m_last, (SUBLANES, LANES))

    if steps > 1:
        kbuf[0:WINDOW, :] = kbuf[L:L + WINDOW, :]
        vt[:, 0:WINDOW] = vt[:, L:L + WINDOW]
        vs[:, 0:WINDOW] = vs[:, L:L + WINDOW]


def _mixer(q, k, v, m4, gt, halo_k, halo_v, c0, m0, sink, mg, *, batch, L, halo_valid):
    n = q.shape[0]
    steps = n // (batch * L)
    assert steps == 1 or L % LANES == 0
    row = lambda b, t: (b * steps + t, 0)
    per_b3 = lambda b, t: (b, 0, 0)
    per_b4 = lambda b, t: (b, 0, 0, 0)
    kv_cols = max(WINDOW + L, 2 * LANES)
    return pl.pallas_call(
        functools.partial(_mixer_kernel, L=L, steps=steps, halo_valid=halo_valid),
        grid=(batch, steps),
        in_specs=[
            pl.BlockSpec(memory_space=pltpu.SMEM),
            pl.BlockSpec((L, Q_PAD), row),
            pl.BlockSpec((L, A_KV), row),
            pl.BlockSpec((L, A_KV), row),
            pl.BlockSpec((L, 4 * M_W), row),
            pl.BlockSpec((L, GATE_COLS), row),
            pl.BlockSpec((1, WINDOW, A_KV), per_b3),
            pl.BlockSpec((1, WINDOW, A_KV), per_b3),
            pl.BlockSpec((1, M_HEADS, 2 * M_HEAD_DIM, M_HEAD_DIM), per_b4),
            pl.BlockSpec((1, SUBLANES, LANES), per_b3),
            pl.BlockSpec((1, M_W), lambda b, t: (0, 0)),
        ],
        out_specs=[
            pl.BlockSpec((L, D_MODEL), row),
            pl.BlockSpec((1, M_HEADS, 2 * M_HEAD_DIM, M_HEAD_DIM), per_b4),
            pl.BlockSpec((1, SUBLANES, LANES), per_b3),
        ],
        out_shape=[
            jax.ShapeDtypeStruct((n, D_MODEL), BF16),
            jax.ShapeDtypeStruct((batch, M_HEADS, 2 * M_HEAD_DIM, M_HEAD_DIM), F32),
            jax.ShapeDtypeStruct((batch, SUBLANES, LANES), F32),
        ],
        scratch_shapes=[
            pltpu.VMEM((WINDOW + L, LANES), BF16),
            pltpu.VMEM((LANES, kv_cols), BF16),
            pltpu.VMEM((LANES, kv_cols), BF16),
        ],
        compiler_params=pltpu.CompilerParams(
            dimension_semantics=("arbitrary", "arbitrary"), vmem_limit_bytes=VMEM_LIMIT),
        name="mixer",
    )(sink, q, k, v, m4, gt, halo_k, halo_v, c0, m0, mg)


def _ffn_kernel(*refs, final):
    if final:
        x_ref, mix_ref, wo_ref, g2_ref, wu_ref, wd_ref, fg_ref, out_ref = refs
    else:
        x_ref, mix_ref, wo_ref, g2_ref, wu_ref, wd_ref, out_ref = refs
    x1 = x_ref[...] + _dot(mix_ref[...], wo_ref[...])
    xn = _rms(x1, g2_ref[...]).astype(BF16)
    acc = x1
    for c in range(D_FF // FF_CHUNK):
        u = _dot(xn, wu_ref[:, c * FF_CHUNK:(c + 1) * FF_CHUNK])
        a = jnp.square(jnp.maximum(u, 0.0)).astype(BF16)
        acc = acc + _dot(a, wd_ref[c * FF_CHUNK:(c + 1) * FF_CHUNK, :])
    if final:
        acc = _rms(acc, fg_ref[...])
    out_ref[...] = acc


def _ffn(x, mix, wo, g2, wu, wd, fg, tm):
    n = x.shape[0]
    final = fg is not None
    row = lambda i: (i, 0)
    fixed = lambda i: (0, 0)
    single = pl.Buffered(1)
    in_specs = [
        pl.BlockSpec((tm, D_MODEL), row),
        pl.BlockSpec((tm, D_MODEL), row),
        pl.BlockSpec((D_MODEL, D_MODEL), fixed, pipeline_mode=single),
        pl.BlockSpec((1, D_MODEL), fixed),
        pl.BlockSpec((D_MODEL, D_FF), fixed, pipeline_mode=single),
        pl.BlockSpec((D_FF, D_MODEL), fixed, pipeline_mode=single),
    ]
    args = [x, mix, wo, g2, wu, wd]
    if final:
        in_specs.append(pl.BlockSpec((1, D_MODEL), fixed))
        args.append(fg)
    return pl.pallas_call(
        functools.partial(_ffn_kernel, final=final),
        grid=(n // tm,),
        in_specs=in_specs,
        out_specs=pl.BlockSpec((tm, D_MODEL), row),
        out_shape=jax.ShapeDtypeStruct((n, D_MODEL), F32),
        compiler_params=pltpu.CompilerParams(
            dimension_semantics=("arbitrary",), vmem_limit_bytes=VMEM_LIMIT),
        name="ffn_final" if final else "ffn",
    )(*args)


def _rotary_tables(pos, rows):
    inv = ROPE_THETA ** (-(jnp.arange(ROPE_HALF, dtype=F32) * 2.0 / ROPE_DIM))
    ang = pos[:, None] * inv[None, :]
    cos, sin = jnp.cos(ang), jnp.sin(ang)
    n = pos.shape[0]
    pad = jnp.zeros((n, A_HEAD_DIM - ROPE_DIM), F32)
    zeros = jnp.zeros_like(sin)
    cos_h = jnp.concatenate([cos, cos, pad + 1.0], axis=1)
    sa_h = jnp.concatenate([-sin, zeros, pad], axis=1)
    sb_h = jnp.concatenate([zeros, sin, pad], axis=1)
    rep = LANES // A_HEAD_DIM
    tile = lambda a: jnp.tile(a, (rows // n, rep))
    return tile(cos_h), tile(sa_h), tile(sb_h)


def _pad_w_in(w):
    gate_pad = jnp.zeros((w.shape[0], LANES - M_HEADS), w.dtype)
    return jnp.concatenate(
        [w[:, :COL_G], w[:, COL_G:COL_G + M_HEADS], gate_pad, w[:, COL_G + M_HEADS:], gate_pad],
        axis=1).astype(BF16)


def _pad_gate_bias(gb):
    pad = jnp.zeros((LANES - M_HEADS,), gb.dtype)
    return jnp.concatenate([gb[:M_HEADS], pad, gb[M_HEADS:], pad]).reshape(1, GATE_COLS)


def _pack_state(C, n, m):
    bsz = C.shape[0]
    c_aug = jnp.concatenate(
        [C, jnp.broadcast_to(n[:, :, None, :], (bsz, M_HEADS, M_HEAD_DIM, M_HEAD_DIM))], axis=2)
    m_row = jnp.pad(m, ((0, 0), (0, LANES - M_HEADS)))
    return c_aug, jnp.broadcast_to(m_row[:, None, :], (bsz, SUBLANES, LANES))


def _unpack_state(c_aug, m_rows):
    return c_aug[:, :, :M_HEAD_DIM, :], c_aug[:, :, M_HEAD_DIM, :], m_rows[:, 0, :M_HEADS]


def kernel(x_prompt, x_sample, cache_k, cache_v, state_C, state_n, state_m, norm1_g, w_in, gate_b,
           attn_sink, mnorm_g, w_out, norm2_g, w_up, w_down, final_g):
    bp, T, _ = x_prompt.shape
    bs, S, _ = x_sample.shape
    depth = w_in.shape[0]
    assert T % PROMPT_MLSTM_CHUNK == 0 and S == CHUNK
    assert T % TOKEN_TILE == 0 and TOKEN_TILE % S == 0 and (bs * S) % TOKEN_TILE == 0

    tabs_p = _rotary_tables(jnp.arange(T, dtype=F32), max(T, TOKEN_TILE))
    tabs_s = _rotary_tables(jnp.arange(S, dtype=F32) + float(PAST_LEN), max(S, TOKEN_TILE))

    zeros_halo = jnp.zeros((bp, WINDOW, A_KV), F32)
    zeros_c = jnp.zeros((bp, M_HEADS, 2 * M_HEAD_DIM, M_HEAD_DIM), F32)
    zeros_m = jnp.zeros((bp, SUBLANES, LANES), F32)

    yp = x_prompt.reshape(bp * T, D_MODEL)
    ys = x_sample.reshape(bs * S, D_MODEL)
    outs = {name: [] for name in ("pk", "pv", "pC", "pn", "pm", "sk", "sv", "sC", "sn", "sm")}
    fg = final_g.reshape(1, D_MODEL)
    for l in range(depth):
        g1 = norm1_g[l].reshape(1, D_MODEL)
        g2 = norm2_g[l].reshape(1, D_MODEL)
        w1 = _pad_w_in(w_in[l])
        gb = _pad_gate_bias(gate_b[l])
        mg = mnorm_g[l].reshape(1, M_W)
        wo = w_out[l].astype(BF16)
        wu = w_up[l].astype(BF16)
        wd = w_down[l].astype(BF16)
        last = fg if l == depth - 1 else None

        q, k, v, m4, gt = _inproj(yp, g1, w1, tabs_p, gb, TOKEN_TILE)
        mix, c_aug, m_rows = _mixer(q, k, v, m4, gt, zeros_halo, zeros_halo, zeros_c, zeros_m,
                                    attn_sink[l], mg, batch=bp, L=PROMPT_MLSTM_CHUNK,
                                    halo_valid=False)
        yp = _ffn(yp, mix, wo, g2, wu, wd, last, TOKEN_TILE)
        C, n, m = _unpack_state(c_aug, m_rows)
        outs["pk"].append(k.reshape(bp, T, A_KV_HEADS, A_HEAD_DIM)[:, T - WINDOW:])
        outs["pv"].append(v.reshape(bp, T, A_KV_HEADS, A_HEAD_DIM)[:, T - WINDOW:])
        outs["pC"].append(C)
        outs["pn"].append(n)
        outs["pm"].append(m)

        q, k, v, m4, gt = _inproj(ys, g1, w1, tabs_s, gb, TOKEN_TILE)
        c0, m0 = _pack_state(state_C[l], state_n[l], state_m[l])
        mix, c_aug, m_rows = _mixer(q, k, v, m4, gt,
                                    cache_k[l].reshape(bs, WINDOW, A_KV),
                                    cache_v[l].reshape(bs, WINDOW, A_KV),
                                    c0, m0, attn_sink[l], mg, batch=bs, L=S, halo_valid=True)
        ys = _ffn(ys, mix, wo, g2, wu, wd, last, TOKEN_TILE)
        C, n, m = _unpack_state(c_aug, m_rows)
        outs["sk"].append(k.reshape(bs, S, A_KV_HEADS, A_HEAD_DIM))
        outs["sv"].append(v.reshape(bs, S, A_KV_HEADS, A_HEAD_DIM))
        outs["sC"].append(C)
        outs["sn"].append(n)
        outs["sm"].append(m)

    st = lambda name: jnp.stack(outs[name])
    return (yp.reshape(bp, T, D_MODEL), ys.reshape(bs, S, D_MODEL),
            st("pk"), st("pv"), st("pC"), st("pn"), st("pm"),
            st("sk"), st("sv"), st("sC"), st("sn"), st("sm"))
```

```python
import functools

import jax
import jax.numpy as jnp
from jax import lax
from jax.experimental import pallas as pl
from jax.experimental.pallas import tpu as pltpu

D_MODEL = 1024
CHUNK = 64
A_HEADS = 8
A_KV_HEADS = 2
A_HEAD_DIM = 64
WINDOW = 128
ROPE_THETA = 500000.0
ROPE_DIM = A_HEAD_DIM // 4
ROPE_HALF = ROPE_DIM // 2
M_HEADS = 4
M_HEAD_DIM = 128
D_FF = 4 * D_MODEL
NORM_EPS = 1e-6
NEG = -1e30
PAST_LEN = 4096
A_Q = A_HEADS * A_HEAD_DIM
A_KV = A_KV_HEADS * A_HEAD_DIM
M_W = M_HEADS * M_HEAD_DIM

LANES = 128
SUBLANES = 8
Q_PAD = A_HEADS * LANES
GATE_COLS = 2 * LANES
COL_KV = A_Q
COL_M = A_Q + 2 * A_KV
COL_G = COL_M + 4 * M_W
IN_COLS_PAD = COL_G + GATE_COLS
Q_HEAD_ORDER = (0, 2, 5, 7, 1, 3, 4, 6)

TOKEN_TILE = 512
PROMPT_MLSTM_CHUNK = 256
FF_CHUNK = 1024
VMEM_LIMIT = 56 * 1024 * 1024

BF16 = jnp.bfloat16
F32 = jnp.float32


def _dot(a, b):
    return jnp.dot(a, b, preferred_element_type=F32)


def _dot_nt(a, b):
    return lax.dot_general(a, b, (((1,), (1,)), ((), ())), preferred_element_type=F32)


def _rms(x, g):
    r = lax.rsqrt(jnp.mean(x * x, axis=-1, keepdims=True) + NORM_EPS)
    return x * r * g


def _inproj_kernel(x_ref, g_ref, w_ref, cos_ref, sa_ref, sb_ref, gb_ref,
                   q_ref, k_ref, v_ref, m4_ref, gt_ref):
    h = _rms(x_ref[...], g_ref[...]).astype(BF16)
    cos, sa, sb = cos_ref[...], sa_ref[...], sb_ref[...]
    low = lax.broadcasted_iota(jnp.int32, (1, LANES), 1) < A_HEAD_DIM

    def rotary(z):
        left = pltpu.roll(z, LANES - ROPE_HALF, axis=1)
        right = pltpu.roll(z, ROPE_HALF, axis=1)
        return z * cos + left * sa + right * sb

    zq_all = _dot(h, w_ref[:, :A_Q])
    for pair in range(A_Q // LANES):
        zq = rotary(zq_all[:, pair * LANES:(pair + 1) * LANES]) * (A_HEAD_DIM ** -0.5)
        zsw = pltpu.roll(zq, A_HEAD_DIM, axis=1)
        zero = jnp.zeros_like(zq)
        if pair < A_HEADS // (2 * A_KV_HEADS):
            even, odd = jnp.where(low, zq, zero), jnp.where(low, zsw, zero)
        else:
            even, odd = jnp.where(low, zero, zsw), jnp.where(low, zero, zq)
        for head, val in ((2 * pair, even), (2 * pair + 1, odd)):
            j = Q_HEAD_ORDER.index(head)
            q_ref[:, j * LANES:(j + 1) * LANES] = val.astype(BF16)
    zkv = _dot(h, w_ref[:, COL_KV:COL_KV + 2 * A_KV])
    k_ref[...] = rotary(zkv[:, :A_KV])
    v_ref[...] = zkv[:, A_KV:]
    for j in range(4):
        zm = _dot(h, w_ref[:, COL_M + j * M_W:COL_M + (j + 1) * M_W])
        if j == 1:
            zm = zm * (M_HEAD_DIM ** -0.5)
        m4_ref[:, j * M_W:(j + 1) * M_W] = zm.astype(BF16)
    zg = _dot(h, w_ref[:, COL_G:COL_G + GATE_COLS]) + gb_ref[...]
    zf = zg[:, LANES:]
    gt_ref[:, :LANES] = zg[:, :LANES]
    gt_ref[:, LANES:] = jnp.minimum(zf, 0.0) - jnp.log1p(jnp.exp(-jnp.abs(zf)))


def _inproj(x, g1, w, tabs, gb, tm):
    n = x.shape[0]
    cos, sa, sb = tabs
    nt = cos.shape[0] // tm
    row = lambda i: (i, 0)
    fixed = lambda i: (0, 0)
    tab = lambda i: (i % nt, 0)
    return pl.pallas_call(
        _inproj_kernel,
        grid=(n // tm,),
        in_specs=[
            pl.BlockSpec((tm, D_MODEL), row),
            pl.BlockSpec((1, D_MODEL), fixed),
            pl.BlockSpec((D_MODEL, IN_COLS_PAD), fixed),
            pl.BlockSpec((tm, LANES), tab),
            pl.BlockSpec((tm, LANES), tab),
            pl.BlockSpec((tm, LANES), tab),
            pl.BlockSpec((1, GATE_COLS), fixed),
        ],
        out_specs=[
            pl.BlockSpec((tm, Q_PAD), row),
            pl.BlockSpec((tm, A_KV), row),
            pl.BlockSpec((tm, A_KV), row),
            pl.BlockSpec((tm, 4 * M_W), row),
            pl.BlockSpec((tm, GATE_COLS), row),
        ],
        out_shape=[
            jax.ShapeDtypeStruct((n, Q_PAD), BF16),
            jax.ShapeDtypeStruct((n, A_KV), F32),
            jax.ShapeDtypeStruct((n, A_KV), F32),
            jax.ShapeDtypeStruct((n, 4 * M_W), BF16),
            jax.ShapeDtypeStruct((n, GATE_COLS), F32),
        ],
        compiler_params=pltpu.CompilerParams(
            dimension_semantics=("arbitrary",), vmem_limit_bytes=VMEM_LIMIT),
        name="inproj",
    )(x, g1, w, cos, sa, sb, gb)


def _split3(x):
    hi = x.astype(BF16)
    r1 = x - hi.astype(F32)
    mid = r1.astype(BF16)
    lo = (r1 - mid.astype(F32)).astype(BF16)
    return hi, mid, lo


def _mixer_kernel(sink_ref, q_ref, k_ref, v_ref, m4_ref, gt_ref, hk_ref, hv_ref, c0_ref, n0_ref,
                  m0_ref, mg_ref, mix_ref, c_ref, n_ref, m_ref, kbuf, vt, vs, cst,
                  *, L, steps, halo_valid):
    t = pl.program_id(1)
    n_chunks = L // CHUNK
    n_keys = WINDOW + CHUNK
    half = A_HEAD_DIM

    def put_vt(cols, v):
        v_t = v.T
        vt[:, cols] = v_t.astype(BF16)
        vs[:, cols] = jnp.concatenate([v_t[half:], v_t[:half]], axis=0).astype(BF16)

    @pl.when(t == 0)
    def _():
        vt[...] = jnp.zeros_like(vt)
        vs[...] = jnp.zeros_like(vs)
        kbuf[0:WINDOW, :] = hk_ref[0].astype(BF16)
        put_vt(pl.ds(0, WINDOW), hv_ref[0])
        for h in range(M_HEADS):
            cst[h, :M_HEAD_DIM, :] = c0_ref[0, h]
            cst[h, M_HEAD_DIM:, :] = jnp.broadcast_to(n0_ref[0, h:h + 1, :], (M_HEAD_DIM, M_HEAD_DIM))
        m_ref[...] = m0_ref[...]

    kbuf[WINDOW:WINDOW + L, :] = k_ref[...].astype(BF16)
    put_vt(pl.ds(WINDOW, L), v_ref[...])

    heads = range(M_HEADS)
    m_slice = lambda j, h: m4_ref[:, j * M_W + h * M_HEAD_DIM:j * M_W + (h + 1) * M_HEAD_DIM]

    scores = []
    for ci in range(n_chunks):
        r0 = ci * CHUNK
        q8 = jnp.concatenate([q_ref[r0:r0 + CHUNK, j * LANES:(j + 1) * LANES]
                              for j in range(A_HEADS)], axis=0)
        scores.append(_dot_nt(kbuf[r0:r0 + n_keys, :], q8))

    gates_i = gt_ref[:, :LANES]
    gates_f = gt_ref[:, LANES:]
    ri = lax.broadcasted_iota(jnp.int32, (L, L), 0)
    ci_ = lax.broadcasted_iota(jnp.int32, (L, L), 1)
    causal = ci_ <= ri
    b3 = _dot(causal.astype(BF16), jnp.concatenate(_split3(gates_f), axis=1))
    b = b3[:, :LANES] + b3[:, LANES:2 * LANES] + b3[:, 2 * LANES:]
    c_old = [cst[h] for h in heads]
    qk = [_dot_nt(m_slice(0, h), m_slice(1, h)) for h in heads]
    qc = [_dot_nt(m_slice(0, h), c_old[h].astype(BF16)) for h in heads]

    lane_q = lax.broadcasted_iota(jnp.int32, (1, A_HEADS * CHUNK), 1) // CHUNK
    sink_row = jnp.zeros((1, A_HEADS * CHUNK), F32)
    for j, head in enumerate(Q_HEAD_ORDER):
        sink_row = jnp.where(lane_q == j, sink_ref[head], sink_row)
    key_row = lax.broadcasted_iota(jnp.int32, (n_keys, A_HEADS * CHUNK), 0)
    probs, rdens = [], []
    for ci in range(n_chunks):
        s = scores[ci]
        if not halo_valid and ci < WINDOW // CHUNK:
            first_key = (t * n_chunks + ci - WINDOW // CHUNK) * CHUNK
            s = jnp.where(key_row + first_key >= 0, s, NEG)
        mx = jnp.maximum(jnp.max(s, axis=0, keepdims=True), sink_row)
        p = jnp.exp(s - mx)
        rdens.append(1.0 / (jnp.sum(p, axis=0, keepdims=True) + jnp.exp(sink_row - mx)))
        pad = jnp.zeros((CHUNK, A_HEADS * CHUNK), BF16)
        pb = p.astype(BF16)
        probs.append(jnp.concatenate([pb, pad] if ci % 2 == 0 else [pad, pb], axis=0))

    a = gates_i - b
    row_l = lax.broadcasted_iota(jnp.int32, (L, LANES), 0)
    cm = a
    shift = 1
    while shift < L:
        cm = jnp.maximum(cm, jnp.where(row_l >= shift, pltpu.roll(cm, shift, axis=0), NEG))
        shift *= 2
    m_prev = m_ref[0, 0:1, :]
    c = jnp.maximum(cm, m_prev)
    mt = c + b
    w_inter = jnp.exp(m_prev - c)
    e_neg_mt = jnp.exp(-mt)
    m_last = mt[L - 1:L, :]
    b_last = b[L - 1:L, :]
    delta = b_last - m_last
    decay = jnp.exp(b_last + m_prev - m_last)
    a_t = a.T
    ones = jnp.ones((L, M_HEAD_DIM), BF16)
    s_bf, upd_lhs = [], []
    for h in heads:
        a_row = a_t[h:h + 1, :]
        e = jnp.exp(jnp.where(causal, a_row - c[:, h:h + 1], NEG))
        s_bf.append((qk[h] * e).astype(BF16))
        ws_row = jnp.exp(a_row + delta[:, h:h + 1])
        v_t = m_slice(2, h).astype(F32).T
        upd_lhs.append(jnp.concatenate(
            [v_t * ws_row, jnp.broadcast_to(ws_row, (M_HEAD_DIM, L))], axis=0).astype(BF16))

    pv = []
    for ci in range(n_chunks):
        c0 = ci * CHUNK if ci % 2 == 0 else (ci - 1) * CHUNK
        win = slice(c0, c0 + 2 * LANES)
        pv.append((_dot(vt[:, win], probs[ci][:, :2 * LANES]),
                   _dot(vs[:, win], probs[ci][:, 2 * LANES:])))
    sv = [_dot(s_bf[h], jnp.concatenate([m_slice(2, h), ones], axis=1)) for h in heads]
    c_upd = [_dot(upd_lhs[h], m_slice(1, h)) for h in heads]

    for ci in range(n_chunks):
        r0 = ci * CHUNK
        ra = pv[ci][0] * rdens[ci][:, :2 * LANES]
        rb = pv[ci][1] * rdens[ci][:, 2 * LANES:]
        o0 = jnp.concatenate([ra[:half, :LANES], rb[half:, :LANES]], axis=0).T.astype(BF16)
        o1 = jnp.concatenate([rb[:half, LANES:], ra[half:, LANES:]], axis=0).T.astype(BF16)
        mix_ref[r0:r0 + CHUNK, 0 * LANES:1 * LANES] = o0[:CHUNK]
        mix_ref[r0:r0 + CHUNK, 1 * LANES:2 * LANES] = o0[CHUNK:]
        mix_ref[r0:r0 + CHUNK, 2 * LANES:3 * LANES] = o1[:CHUNK]
        mix_ref[r0:r0 + CHUNK, 3 * LANES:4 * LANES] = o1[CHUNK:]
    for h in heads:
        tot = sv[h] + w_inter[:, h:h + 1] * qc[h]
        hh = tot[:, :M_HEAD_DIM] / jnp.maximum(jnp.abs(tot[:, M_HEAD_DIM:]), e_neg_mt[:, h:h + 1])
        hn = _rms(hh, mg_ref[:, h * M_HEAD_DIM:(h + 1) * M_HEAD_DIM])
        out = hn * jax.nn.sigmoid(m_slice(3, h).astype(F32))
        mix_ref[:, A_Q + h * M_HEAD_DIM:A_Q + (h + 1) * M_HEAD_DIM] = out.astype(BF16)
        cst[h] = decay[:, h:h + 1] * c_old[h] + c_upd[h]
    m_ref[0] = jnp.broadcast_to(m_last, (SUBLANES, LANES))

    def write_state():
        for h in heads:
            c_ref[0, h] = cst[h, :M_HEAD_DIM, :]
            n_ref[0, h:h + 1, :] = cst[h, M_HEAD_DIM:M_HEAD_DIM + 1, :]

    if steps == 1:
        write_state()
    else:
        pl.when(t == steps - 1)(write_state)

    if steps > 1:
        kbuf[0:WINDOW, :] = kbuf[L:L + WINDOW, :]
        vt[:, 0:WINDOW] = vt[:, L:L + WINDOW]
        vs[:, 0:WINDOW] = vs[:, L:L + WINDOW]


def _mixer(q, k, v, m4, gt, halo_k, halo_v, c0, n0, m0, sink, mg, *, batch, L, halo_valid):
    n = q.shape[0]
    steps = n // (batch * L)
    assert steps == 1 or L % LANES == 0
    row = lambda b, t: (b * steps + t, 0)
    per_b3 = lambda b, t: (b, 0, 0)
    per_b4 = lambda b, t: (b, 0, 0, 0)
    kv_cols = max(WINDOW + L, 2 * LANES)
    return pl.pallas_call(
        functools.partial(_mixer_kernel, L=L, steps=steps, halo_valid=halo_valid),
        grid=(batch, steps),
        in_specs=[
            pl.BlockSpec(memory_space=pltpu.SMEM),
            pl.BlockSpec((L, Q_PAD), row),
            pl.BlockSpec((L, A_KV), row),
            pl.BlockSpec((L, A_KV), row),
            pl.BlockSpec((L, 4 * M_W), row),
            pl.BlockSpec((L, GATE_COLS), row),
            pl.BlockSpec((1, WINDOW, A_KV), per_b3),
            pl.BlockSpec((1, WINDOW, A_KV), per_b3),
            pl.BlockSpec((1, M_HEADS, M_HEAD_DIM, M_HEAD_DIM), per_b4),
            pl.BlockSpec((1, M_HEADS, M_HEAD_DIM), per_b3),
            pl.BlockSpec((1, SUBLANES, LANES), per_b3),
            pl.BlockSpec((1, M_W), lambda b, t: (0, 0)),
        ],
        out_specs=[
            pl.BlockSpec((L, D_MODEL), row),
            pl.BlockSpec((1, M_HEADS, M_HEAD_DIM, M_HEAD_DIM), per_b4),
            pl.BlockSpec((1, M_HEADS, M_HEAD_DIM), per_b3),
            pl.BlockSpec((1, SUBLANES, LANES), per_b3),
        ],
        out_shape=[
            jax.ShapeDtypeStruct((n, D_MODEL), BF16),
            jax.ShapeDtypeStruct((batch, M_HEADS, M_HEAD_DIM, M_HEAD_DIM), F32),
            jax.ShapeDtypeStruct((batch, M_HEADS, M_HEAD_DIM), F32),
            jax.ShapeDtypeStruct((batch, SUBLANES, LANES), F32),
        ],
        scratch_shapes=[
            pltpu.VMEM((WINDOW + L, LANES), BF16),
            pltpu.VMEM((LANES, kv_cols), BF16),
            pltpu.VMEM((LANES, kv_cols), BF16),
            pltpu.VMEM((M_HEADS, 2 * M_HEAD_DIM, M_HEAD_DIM), F32),
        ],
        compiler_params=pltpu.CompilerParams(
            dimension_semantics=("arbitrary", "arbitrary"), vmem_limit_bytes=VMEM_LIMIT),
        name="mixer",
    )(sink, q, k, v, m4, gt, halo_k, halo_v, c0, n0, m0, mg)


def _ffn_kernel(*refs, final):
    if final:
        x_ref, mix_ref, wo_ref, g2_ref, wu_ref, wd_ref, fg_ref, out_ref = refs
    else:
        x_ref, mix_ref, wo_ref, g2_ref, wu_ref, wd_ref, out_ref = refs
    x1 = x_ref[...] + _dot(mix_ref[...], wo_ref[...])
    xn = _rms(x1, g2_ref[...]).astype(BF16)
    acc = x1
    for c in range(D_FF // FF_CHUNK):
        u = _dot(xn, wu_ref[:, c * FF_CHUNK:(c + 1) * FF_CHUNK])
        a = jnp.square(jnp.maximum(u, 0.0)).astype(BF16)
        acc = acc + _dot(a, wd_ref[c * FF_CHUNK:(c + 1) * FF_CHUNK, :])
    if final:
        acc = _rms(acc, fg_ref[...])
    out_ref[...] = acc


def _ffn(x, mix, wo, g2, wu, wd, fg, tm):
    n = x.shape[0]
    final = fg is not None
    row = lambda i: (i, 0)
    fixed = lambda i: (0, 0)
    single = pl.Buffered(1)
    in_specs = [
        pl.BlockSpec((tm, D_MODEL), row),
        pl.BlockSpec((tm, D_MODEL), row),
        pl.BlockSpec((D_MODEL, D_MODEL), fixed, pipeline_mode=single),
        pl.BlockSpec((1, D_MODEL), fixed),
        pl.BlockSpec((D_MODEL, D_FF), fixed, pipeline_mode=single),
        pl.BlockSpec((D_FF, D_MODEL), fixed, pipeline_mode=single),
    ]
    args = [x, mix, wo, g2, wu, wd]
    if final:
        in_specs.append(pl.BlockSpec((1, D_MODEL), fixed))
        args.append(fg)
    return pl.pallas_call(
        functools.partial(_ffn_kernel, final=final),
        grid=(n // tm,),
        in_specs=in_specs,
        out_specs=pl.BlockSpec((tm, D_MODEL), row),
        out_shape=jax.ShapeDtypeStruct((n, D_MODEL), F32),
        compiler_params=pltpu.CompilerParams(
            dimension_semantics=("arbitrary",), vmem_limit_bytes=VMEM_LIMIT),
        name="ffn_final" if final else "ffn",
    )(*args)


def _rotary_tables(pos, rows):
    inv = ROPE_THETA ** (-(jnp.arange(ROPE_HALF, dtype=F32) * 2.0 / ROPE_DIM))
    ang = pos[:, None] * inv[None, :]
    cos, sin = jnp.cos(ang), jnp.sin(ang)
    n = pos.shape[0]
    pad = jnp.zeros((n, A_HEAD_DIM - ROPE_DIM), F32)
    zeros = jnp.zeros_like(sin)
    cos_h = jnp.concatenate([cos, cos, pad + 1.0], axis=1)
    sa_h = jnp.concatenate([-sin, zeros, pad], axis=1)
    sb_h = jnp.concatenate([zeros, sin, pad], axis=1)
    rep = LANES // A_HEAD_DIM
    tile = lambda a: jnp.tile(a, (rows // n, rep))
    return tile(cos_h), tile(sa_h), tile(sb_h)


def _pad_w_in(w):
    gate_pad = jnp.zeros((w.shape[0], LANES - M_HEADS), w.dtype)
    return jnp.concatenate(
        [w[:, :COL_G], w[:, COL_G:COL_G + M_HEADS], gate_pad, w[:, COL_G + M_HEADS:], gate_pad],
        axis=1).astype(BF16)


def _pad_gate_bias(gb):
    pad = jnp.zeros((LANES - M_HEADS,), gb.dtype)
    return jnp.concatenate([gb[:M_HEADS], pad, gb[M_HEADS:], pad]).reshape(1, GATE_COLS)


def _m_rows(m):
    m_row = jnp.pad(m, ((0, 0), (0, LANES - M_HEADS)))
    return jnp.broadcast_to(m_row[:, None, :], (m.shape[0], SUBLANES, LANES))


def kernel(x_prompt, x_sample, cache_k, cache_v, state_C, state_n, state_m, norm1_g, w_in, gate_b,
           attn_sink, mnorm_g, w_out, norm2_g, w_up, w_down, final_g):
    bp, T, _ = x_prompt.shape
    bs, S, _ = x_sample.shape
    depth = w_in.shape[0]
    assert T % PROMPT_MLSTM_CHUNK == 0 and S == CHUNK
    assert T % TOKEN_TILE == 0 and TOKEN_TILE % S == 0 and (bs * S) % TOKEN_TILE == 0

    tabs_p = _rotary_tables(jnp.arange(T, dtype=F32), max(T, TOKEN_TILE))
    tabs_s = _rotary_tables(jnp.arange(S, dtype=F32) + float(PAST_LEN), max(S, TOKEN_TILE))

    zeros_halo = jnp.zeros((bp, WINDOW, A_KV), F32)
    zeros_c = jnp.zeros((bp, M_HEADS, M_HEAD_DIM, M_HEAD_DIM), F32)
    zeros_n = jnp.zeros((bp, M_HEADS, M_HEAD_DIM), F32)
    zeros_m = jnp.zeros((bp, SUBLANES, LANES), F32)

    yp = x_prompt.reshape(bp * T, D_MODEL)
    ys = x_sample.reshape(bs * S, D_MODEL)
    outs = {name: [] for name in ("pk", "pv", "pC", "pn", "pm", "sk", "sv", "sC", "sn", "sm")}
    fg = final_g.reshape(1, D_MODEL)
    for l in range(depth):
        g1 = norm1_g[l].reshape(1, D_MODEL)
        g2 = norm2_g[l].reshape(1, D_MODEL)
        w1 = _pad_w_in(w_in[l])
        gb = _pad_gate_bias(gate_b[l])
        mg = mnorm_g[l].reshape(1, M_W)
        wo = w_out[l].astype(BF16)
        wu = w_up[l].astype(BF16)
        wd = w_down[l].astype(BF16)
        last = fg if l == depth - 1 else None

        q, k, v, m4, gt = _inproj(yp, g1, w1, tabs_p, gb, TOKEN_TILE)
        mix, C, n, m_rows = _mixer(q, k, v, m4, gt, zeros_halo, zeros_halo, zeros_c, zeros_n,
                                   zeros_m, attn_sink[l], mg, batch=bp, L=PROMPT_MLSTM_CHUNK,
                                   halo_valid=False)
        yp = _ffn(yp, mix, wo, g2, wu, wd, last, TOKEN_TILE)
        m = m_rows[:, 0, :M_HEADS]
        outs["pk"].append(k.reshape(bp, T, A_KV_HEADS, A_HEAD_DIM)[:, T - WINDOW:])
        outs["pv"].append(v.reshape(bp, T, A_KV_HEADS, A_HEAD_DIM)[:, T - WINDOW:])
        outs["pC"].append(C)
        outs["pn"].append(n)
        outs["pm"].append(m)

        q, k, v, m4, gt = _inproj(ys, g1, w1, tabs_s, gb, TOKEN_TILE)
        mix, C, n, m_rows = _mixer(q, k, v, m4, gt,
                                   cache_k[l].reshape(bs, WINDOW, A_KV),
                                   cache_v[l].reshape(bs, WINDOW, A_KV),
                                   state_C[l], state_n[l], _m_rows(state_m[l]),
                                   attn_sink[l], mg, batch=bs, L=S, halo_valid=True)
        ys = _ffn(ys, mix, wo, g2, wu, wd, last, TOKEN_TILE)
        m = m_rows[:, 0, :M_HEADS]
        outs["sk"].append(k.reshape(bs, S, A_KV_HEADS, A_HEAD_DIM))
        outs["sv"].append(v.reshape(bs, S, A_KV_HEADS, A_HEAD_DIM))
        outs["sC"].append(C)
        outs["sn"].append(n)
        outs["sm"].append(m)

    st = lambda name: jnp.stack(outs[name])
    return (yp.reshape(bp, T, D_MODEL), ys.reshape(bs, S, D_MODEL),
            st("pk"), st("pv"), st("pC"), st("pn"), st("pm"),
            st("sk"), st("sv"), st("sC"), st("sn"), st("sm"))
```

```python
import functools

import jax
import jax.numpy as jnp
from jax import lax
from jax.experimental import pallas as pl
from jax.experimental.pallas import tpu as pltpu

D_MODEL = 1024
CHUNK = 64
A_HEADS = 8
A_KV_HEADS = 2
A_HEAD_DIM = 64
WINDOW = 128
ROPE_THETA = 500000.0
ROPE_DIM = A_HEAD_DIM // 4
ROPE_HALF = ROPE_DIM // 2
M_HEADS = 4
M_HEAD_DIM = 128
D_FF = 4 * D_MODEL
NORM_EPS = 1e-6
NEG = -1e30
PAST_LEN = 4096
A_Q = A_HEADS * A_HEAD_DIM
A_KV = A_KV_HEADS * A_HEAD_DIM
M_W = M_HEADS * M_HEAD_DIM

LANES = 128
SUBLANES = 8
Q_PAD = A_HEADS * LANES
GATE_COLS = 2 * LANES
COL_KV = A_Q
COL_M = A_Q + 2 * A_KV
COL_G = COL_M + 4 * M_W
IN_COLS_PAD = COL_G + GATE_COLS
Q_HEAD_ORDER = (0, 2, 5, 7, 1, 3, 4, 6)

TOKEN_TILE = 512
PROMPT_MLSTM_CHUNK = 256
FF_CHUNK = 1024
VMEM_LIMIT = 56 * 1024 * 1024

BF16 = jnp.bfloat16
F32 = jnp.float32


def _dot(a, b):
    return jnp.dot(a, b, preferred_element_type=F32)


def _dot_nt(a, b):
    return lax.dot_general(a, b, (((1,), (1,)), ((), ())), preferred_element_type=F32)


def _rms(x, g):
    r = lax.rsqrt(jnp.mean(x * x, axis=-1, keepdims=True) + NORM_EPS)
    return x * r * g


def _inproj_kernel(x_ref, g_ref, w_ref, cos_ref, sa_ref, sb_ref, gb_ref,
                   q_ref, k_ref, v_ref, m4_ref, gt_ref):
    h = _rms(x_ref[...], g_ref[...]).astype(BF16)
    cos, sa, sb = cos_ref[...], sa_ref[...], sb_ref[...]
    low = lax.broadcasted_iota(jnp.int32, (1, LANES), 1) < A_HEAD_DIM

    def rotary(z):
        left = pltpu.roll(z, LANES - ROPE_HALF, axis=1)
        right = pltpu.roll(z, ROPE_HALF, axis=1)
        return z * cos + left * sa + right * sb

    zq_all = _dot(h, w_ref[:, :A_Q])
    for pair in range(A_Q // LANES):
        zq = rotary(zq_all[:, pair * LANES:(pair + 1) * LANES]) * (A_HEAD_DIM ** -0.5)
        zsw = pltpu.roll(zq, A_HEAD_DIM, axis=1)
        zero = jnp.zeros_like(zq)
        if pair < A_HEADS // (2 * A_KV_HEADS):
            even, odd = jnp.where(low, zq, zero), jnp.where(low, zsw, zero)
        else:
            even, odd = jnp.where(low, zero, zsw), jnp.where(low, zero, zq)
        for head, val in ((2 * pair, even), (2 * pair + 1, odd)):
            j = Q_HEAD_ORDER.index(head)
            q_ref[:, j * LANES:(j + 1) * LANES] = val.astype(BF16)
    zkv = _dot(h, w_ref[:, COL_KV:COL_KV + 2 * A_KV])
    k_ref[...] = rotary(zkv[:, :A_KV])
    v_ref[...] = zkv[:, A_KV:]
    for j in range(4):
        zm = _dot(h, w_ref[:, COL_M + j * M_W:COL_M + (j + 1) * M_W])
        if j == 1:
            zm = zm * (M_HEAD_DIM ** -0.5)
        m4_ref[:, j * M_W:(j + 1) * M_W] = zm.astype(BF16)
    zg = _dot(h, w_ref[:, COL_G:COL_G + GATE_COLS]) + gb_ref[...]
    zf = zg[:, LANES:]
    gt_ref[:, :LANES] = zg[:, :LANES]
    gt_ref[:, LANES:] = jnp.minimum(zf, 0.0) - jnp.log1p(jnp.exp(-jnp.abs(zf)))


def _inproj(x, g1, w, tabs, gb, tm, layer):
    n = x.shape[0]
    cos, sa, sb = tabs
    nt = cos.shape[0] // tm
    row = lambda i: (i, 0)
    of_layer = lambda i: (layer, 0, 0)
    tab = lambda i: (i % nt, 0)
    return pl.pallas_call(
        _inproj_kernel,
        grid=(n // tm,),
        in_specs=[
            pl.BlockSpec((tm, D_MODEL), row),
            pl.BlockSpec((None, 1, D_MODEL), of_layer),
            pl.BlockSpec((None, D_MODEL, IN_COLS_PAD), of_layer),
            pl.BlockSpec((tm, LANES), tab),
            pl.BlockSpec((tm, LANES), tab),
            pl.BlockSpec((tm, LANES), tab),
            pl.BlockSpec((None, 1, GATE_COLS), of_layer),
        ],
        out_specs=[
            pl.BlockSpec((tm, Q_PAD), row),
            pl.BlockSpec((tm, A_KV), row),
            pl.BlockSpec((tm, A_KV), row),
            pl.BlockSpec((tm, 4 * M_W), row),
            pl.BlockSpec((tm, GATE_COLS), row),
        ],
        out_shape=[
            jax.ShapeDtypeStruct((n, Q_PAD), BF16),
            jax.ShapeDtypeStruct((n, A_KV), F32),
            jax.ShapeDtypeStruct((n, A_KV), F32),
            jax.ShapeDtypeStruct((n, 4 * M_W), BF16),
            jax.ShapeDtypeStruct((n, GATE_COLS), F32),
        ],
        compiler_params=pltpu.CompilerParams(
            dimension_semantics=("arbitrary",), vmem_limit_bytes=VMEM_LIMIT),
        name="inproj",
    )(x, g1, w, cos, sa, sb, gb)


def _split3(x):
    hi = x.astype(BF16)
    r1 = x - hi.astype(F32)
    mid = r1.astype(BF16)
    lo = (r1 - mid.astype(F32)).astype(BF16)
    return hi, mid, lo


def _mixer_kernel(sink_ref, q_ref, k_ref, v_ref, m4_ref, gt_ref, hk_ref, hv_ref, c0_ref, n0_ref,
                  m0_ref, mg_ref, mix_ref, c_ref, n_ref, m_ref, kbuf, vt, vs, cst,
                  *, L, steps, halo_valid, layer):
    t = pl.program_id(1)
    n_chunks = L // CHUNK
    n_keys = WINDOW + CHUNK
    half = A_HEAD_DIM

    def put_vt(cols, v):
        v_t = v.T
        vt[:, cols] = v_t.astype(BF16)
        vs[:, cols] = jnp.concatenate([v_t[half:], v_t[:half]], axis=0).astype(BF16)

    @pl.when(t == 0)
    def _():
        vt[...] = jnp.zeros_like(vt)
        vs[...] = jnp.zeros_like(vs)
        kbuf[0:WINDOW, :] = hk_ref[0].astype(BF16)
        put_vt(pl.ds(0, WINDOW), hv_ref[0])
        for h in range(M_HEADS):
            cst[h, :M_HEAD_DIM, :] = c0_ref[0, h]
            cst[h, M_HEAD_DIM:, :] = jnp.broadcast_to(n0_ref[0, h:h + 1, :], (M_HEAD_DIM, M_HEAD_DIM))
        m_ref[...] = m0_ref[...]

    kbuf[WINDOW:WINDOW + L, :] = k_ref[...].astype(BF16)
    put_vt(pl.ds(WINDOW, L), v_ref[...])

    heads = range(M_HEADS)
    m_slice = lambda j, h: m4_ref[:, j * M_W + h * M_HEAD_DIM:j * M_W + (h + 1) * M_HEAD_DIM]

    scores = []
    for ci in range(n_chunks):
        r0 = ci * CHUNK
        q8 = jnp.concatenate([q_ref[r0:r0 + CHUNK, j * LANES:(j + 1) * LANES]
                              for j in range(A_HEADS)], axis=0)
        scores.append(_dot_nt(kbuf[r0:r0 + n_keys, :], q8))

    gates_i = gt_ref[:, :LANES]
    gates_f = gt_ref[:, LANES:]
    ri = lax.broadcasted_iota(jnp.int32, (L, L), 0)
    ci_ = lax.broadcasted_iota(jnp.int32, (L, L), 1)
    causal = ci_ <= ri
    b3 = _dot(causal.astype(BF16), jnp.concatenate(_split3(gates_f), axis=1))
    b = b3[:, :LANES] + b3[:, LANES:2 * LANES] + b3[:, 2 * LANES:]
    c_old = [cst[h] for h in heads]
    qk = [_dot_nt(m_slice(0, h), m_slice(1, h)) for h in heads]
    qc = [_dot_nt(m_slice(0, h), c_old[h].astype(BF16)) for h in heads]

    lane_q = lax.broadcasted_iota(jnp.int32, (1, A_HEADS * CHUNK), 1) // CHUNK
    sink_row = jnp.zeros((1, A_HEADS * CHUNK), F32)
    for j, head in enumerate(Q_HEAD_ORDER):
        sink_row = jnp.where(lane_q == j, sink_ref[layer, head], sink_row)
    key_row = lax.broadcasted_iota(jnp.int32, (n_keys, A_HEADS * CHUNK), 0)
    probs, rdens = [], []
    for ci in range(n_chunks):
        s = scores[ci]
        if not halo_valid and ci < WINDOW // CHUNK:
            first_key = (t * n_chunks + ci - WINDOW // CHUNK) * CHUNK
            s = jnp.where(key_row + first_key >= 0, s, NEG)
        mx = jnp.maximum(jnp.max(s, axis=0, keepdims=True), sink_row)
        p = jnp.exp(s - mx)
        rdens.append(1.0 / (jnp.sum(p, axis=0, keepdims=True) + jnp.exp(sink_row - mx)))
        pad = jnp.zeros((CHUNK, A_HEADS * CHUNK), BF16)
        pb = p.astype(BF16)
        probs.append(jnp.concatenate([pb, pad] if ci % 2 == 0 else [pad, pb], axis=0))

    a = gates_i - b
    row_l = lax.broadcasted_iota(jnp.int32, (L, LANES), 0)
    cm = a
    shift = 1
    while shift < L:
        cm = jnp.maximum(cm, jnp.where(row_l >= shift, pltpu.roll(cm, shift, axis=0), NEG))
        shift *= 2
    m_prev = m_ref[0, 0:1, :]
    c = jnp.maximum(cm, m_prev)
    mt = c + b
    w_inter = jnp.exp(m_prev - c)
    e_neg_mt = jnp.exp(-mt)
    m_last = mt[L - 1:L, :]
    b_last = b[L - 1:L, :]
    delta = b_last - m_last
    decay = jnp.exp(b_last + m_prev - m_last)
    a_t = a.T
    ones = jnp.ones((L, M_HEAD_DIM), BF16)
    s_bf, upd_lhs = [], []
    for h in heads:
        a_row = a_t[h:h + 1, :]
        e = jnp.exp(jnp.where(causal, a_row - c[:, h:h + 1], NEG))
        s_bf.append((qk[h] * e).astype(BF16))
        ws_row = jnp.exp(a_row + delta[:, h:h + 1])
        v_t = m_slice(2, h).astype(F32).T
        upd_lhs.append(jnp.concatenate(
            [v_t * ws_row, jnp.broadcast_to(ws_row, (M_HEAD_DIM, L))], axis=0).astype(BF16))

    pv = []
    for ci in range(n_chunks):
        c0 = ci * CHUNK if ci % 2 == 0 else (ci - 1) * CHUNK
        win = slice(c0, c0 + 2 * LANES)
        pv.append((_dot(vt[:, win], probs[ci][:, :2 * LANES]),
                   _dot(vs[:, win], probs[ci][:, 2 * LANES:])))
    sv = [_dot(s_bf[h], jnp.concatenate([m_slice(2, h), ones], axis=1)) for h in heads]
    c_upd = [_dot(upd_lhs[h], m_slice(1, h)) for h in heads]

    for ci in range(n_chunks):
        r0 = ci * CHUNK
        ra = pv[ci][0] * rdens[ci][:, :2 * LANES]
        rb = pv[ci][1] * rdens[ci][:, 2 * LANES:]
        o0 = jnp.concatenate([ra[:half, :LANES], rb[half:, :LANES]], axis=0).T.astype(BF16)
        o1 = jnp.concatenate([rb[:half, LANES:], ra[half:, LANES:]], axis=0).T.astype(BF16)
        mix_ref[r0:r0 + CHUNK, 0 * LANES:1 * LANES] = o0[:CHUNK]
        mix_ref[r0:r0 + CHUNK, 1 * LANES:2 * LANES] = o0[CHUNK:]
        mix_ref[r0:r0 + CHUNK, 2 * LANES:3 * LANES] = o1[:CHUNK]
        mix_ref[r0:r0 + CHUNK, 3 * LANES:4 * LANES] = o1[CHUNK:]
    for h in heads:
        tot = sv[h] + w_inter[:, h:h + 1] * qc[h]
        hh = tot[:, :M_HEAD_DIM] / jnp.maximum(jnp.abs(tot[:, M_HEAD_DIM:]), e_neg_mt[:, h:h + 1])
        hn = _rms(hh, mg_ref[:, h * M_HEAD_DIM:(h + 1) * M_HEAD_DIM])
        out = hn * jax.nn.sigmoid(m_slice(3, h).astype(F32))
        mix_ref[:, A_Q + h * M_HEAD_DIM:A_Q + (h + 1) * M_HEAD_DIM] = out.astype(BF16)
        cst[h] = decay[:, h:h + 1] * c_old[h] + c_upd[h]
    m_ref[0] = jnp.broadcast_to(m_last, (SUBLANES, LANES))

    def write_state():
        for h in heads:
            c_ref[0, h] = cst[h, :M_HEAD_DIM, :]
            n_ref[0, h:h + 1, :] = cst[h, M_HEAD_DIM:M_HEAD_DIM + 1, :]

    if steps == 1:
        write_state()
    else:
        pl.when(t == steps - 1)(write_state)

    if steps > 1:
        kbuf[0:WINDOW, :] = kbuf[L:L + WINDOW, :]
        vt[:, 0:WINDOW] = vt[:, L:L + WINDOW]
        vs[:, 0:WINDOW] = vs[:, L:L + WINDOW]


def _mixer(q, k, v, m4, gt, halo_k, halo_v, c0, n0, m0, sink, mg,
           *, batch, L, halo_valid, layer, state_layer):
    n = q.shape[0]
    steps = n // (batch * L)
    assert steps == 1 or L % LANES == 0
    row = lambda b, t: (b * steps + t, 0)
    per_b3 = lambda b, t: (b, 0, 0)
    per_b4 = lambda b, t: (b, 0, 0, 0)
    init3 = lambda b, t: (state_layer, b, 0, 0)
    init4 = lambda b, t: (state_layer, b, 0, 0, 0)
    kv_cols = max(WINDOW + L, 2 * LANES)
    return pl.pallas_call(
        functools.partial(_mixer_kernel, L=L, steps=steps, halo_valid=halo_valid, layer=layer),
        grid=(batch, steps),
        in_specs=[
            pl.BlockSpec(memory_space=pltpu.SMEM),
            pl.BlockSpec((L, Q_PAD), row),
            pl.BlockSpec((L, A_KV), row),
            pl.BlockSpec((L, A_KV), row),
            pl.BlockSpec((L, 4 * M_W), row),
            pl.BlockSpec((L, GATE_COLS), row),
            pl.BlockSpec((None, 1, WINDOW, A_KV), init3),
            pl.BlockSpec((None, 1, WINDOW, A_KV), init3),
            pl.BlockSpec((None, 1, M_HEADS, M_HEAD_DIM, M_HEAD_DIM), init4),
            pl.BlockSpec((None, 1, M_HEADS, M_HEAD_DIM), init3),
            pl.BlockSpec((None, 1, SUBLANES, LANES), init3),
            pl.BlockSpec((None, 1, M_W), lambda b, t: (layer, 0, 0)),
        ],
        out_specs=[
            pl.BlockSpec((L, D_MODEL), row),
            pl.BlockSpec((1, M_HEADS, M_HEAD_DIM, M_HEAD_DIM), per_b4),
            pl.BlockSpec((1, M_HEADS, M_HEAD_DIM), per_b3),
            pl.BlockSpec((1, SUBLANES, LANES), per_b3),
        ],
        out_shape=[
            jax.ShapeDtypeStruct((n, D_MODEL), BF16),
            jax.ShapeDtypeStruct((batch, M_HEADS, M_HEAD_DIM, M_HEAD_DIM), F32),
            jax.ShapeDtypeStruct((batch, M_HEADS, M_HEAD_DIM), F32),
            jax.ShapeDtypeStruct((batch, SUBLANES, LANES), F32),
        ],
        scratch_shapes=[
            pltpu.VMEM((WINDOW + L, LANES), BF16),
            pltpu.VMEM((LANES, kv_cols), BF16),
            pltpu.VMEM((LANES, kv_cols), BF16),
            pltpu.VMEM((M_HEADS, 2 * M_HEAD_DIM, M_HEAD_DIM), F32),
        ],
        compiler_params=pltpu.CompilerParams(
            dimension_semantics=("arbitrary", "arbitrary"), vmem_limit_bytes=VMEM_LIMIT),
        name="mixer",
    )(sink, q, k, v, m4, gt, halo_k, halo_v, c0, n0, m0, mg)


def _ffn_kernel(*refs, final):
    if final:
        x_ref, mix_ref, wo_ref, g2_ref, wu_ref, wd_ref, fg_ref, out_ref = refs
    else:
        x_ref, mix_ref, wo_ref, g2_ref, wu_ref, wd_ref, out_ref = refs
    x1 = x_ref[...] + _dot(mix_ref[...], wo_ref[...])
    xn = _rms(x1, g2_ref[...]).astype(BF16)
    acc = x1
    for c in range(D_FF // FF_CHUNK):
        u = _dot(xn, wu_ref[:, c * FF_CHUNK:(c + 1) * FF_CHUNK])
        a = jnp.square(jnp.maximum(u, 0.0)).astype(BF16)
        acc = acc + _dot(a, wd_ref[c * FF_CHUNK:(c + 1) * FF_CHUNK, :])
    if final:
        acc = _rms(acc, fg_ref[...])
    out_ref[...] = acc


def _ffn(x, mix, wo, g2, wu, wd, fg, tm, layer):
    n = x.shape[0]
    final = fg is not None
    row = lambda i: (i, 0)
    fixed = lambda i: (0, 0)
    of_layer = lambda i: (layer, 0, 0)
    single = pl.Buffered(1)
    in_specs = [
        pl.BlockSpec((tm, D_MODEL), row),
        pl.BlockSpec((tm, D_MODEL), row),
        pl.BlockSpec((None, D_MODEL, D_MODEL), of_layer, pipeline_mode=single),
        pl.BlockSpec((None, 1, D_MODEL), of_layer),
        pl.BlockSpec((None, D_MODEL, D_FF), of_layer, pipeline_mode=single),
        pl.BlockSpec((None, D_FF, D_MODEL), of_layer, pipeline_mode=single),
    ]
    args = [x, mix, wo, g2, wu, wd]
    if final:
        in_specs.append(pl.BlockSpec((1, D_MODEL), fixed))
        args.append(fg)
    return pl.pallas_call(
        functools.partial(_ffn_kernel, final=final),
        grid=(n // tm,),
        in_specs=in_specs,
        out_specs=pl.BlockSpec((tm, D_MODEL), row),
        out_shape=jax.ShapeDtypeStruct((n, D_MODEL), F32),
        compiler_params=pltpu.CompilerParams(
            dimension_semantics=("arbitrary",), vmem_limit_bytes=VMEM_LIMIT),
        name="ffn_final" if final else "ffn",
    )(*args)


def _rotary_tables(pos, rows):
    inv = ROPE_THETA ** (-(jnp.arange(ROPE_HALF, dtype=F32) * 2.0 / ROPE_DIM))
    ang = pos[:, None] * inv[None, :]
    cos, sin = jnp.cos(ang), jnp.sin(ang)
    n = pos.shape[0]
    pad = jnp.zeros((n, A_HEAD_DIM - ROPE_DIM), F32)
    zeros = jnp.zeros_like(sin)
    cos_h = jnp.concatenate([cos, cos, pad + 1.0], axis=1)
    sa_h = jnp.concatenate([-sin, zeros, pad], axis=1)
    sb_h = jnp.concatenate([zeros, sin, pad], axis=1)
    rep = LANES // A_HEAD_DIM
    tile = lambda a: jnp.tile(a, (rows // n, rep))
    return tile(cos_h), tile(sa_h), tile(sb_h)


def _pad_w_in(w):
    gate_pad = jnp.zeros(w.shape[:2] + (LANES - M_HEADS,), w.dtype)
    return jnp.concatenate(
        [w[..., :COL_G], w[..., COL_G:COL_G + M_HEADS], gate_pad, w[..., COL_G + M_HEADS:], gate_pad],
        axis=-1).astype(BF16)


def _pad_gate_bias(gb):
    pad = jnp.zeros((gb.shape[0], LANES - M_HEADS), gb.dtype)
    return jnp.concatenate([gb[:, :M_HEADS], pad, gb[:, M_HEADS:], pad], axis=1)[:, None, :]


def _m_rows(m):
    m_row = jnp.pad(m, [(0, 0)] * (m.ndim - 1) + [(0, LANES - M_HEADS)])
    return jnp.broadcast_to(m_row[..., None, :], m.shape[:-1] + (SUBLANES, LANES))


def kernel(x_prompt, x_sample, cache_k, cache_v, state_C, state_n, state_m, norm1_g, w_in, gate_b,
           attn_sink, mnorm_g, w_out, norm2_g, w_up, w_down, final_g):
    bp, T, _ = x_prompt.shape
    bs, S, _ = x_sample.shape
    depth = w_in.shape[0]
    assert T % PROMPT_MLSTM_CHUNK == 0 and S == CHUNK
    assert T % TOKEN_TILE == 0 and TOKEN_TILE % S == 0 and (bs * S) % TOKEN_TILE == 0

    tabs_p = _rotary_tables(jnp.arange(T, dtype=F32), max(T, TOKEN_TILE))
    tabs_s = _rotary_tables(jnp.arange(S, dtype=F32) + float(PAST_LEN), max(S, TOKEN_TILE))

    zeros_halo = jnp.zeros((1, bp, WINDOW, A_KV), F32)
    zeros_c = jnp.zeros((1, bp, M_HEADS, M_HEAD_DIM, M_HEAD_DIM), F32)
    zeros_n = jnp.zeros((1, bp, M_HEADS, M_HEAD_DIM), F32)
    zeros_m = jnp.zeros((1, bp, SUBLANES, LANES), F32)

    g1 = norm1_g[:, None, :]
    g2 = norm2_g[:, None, :]
    mg = mnorm_g[:, None, :]
    w1 = _pad_w_in(w_in)
    gb = _pad_gate_bias(gate_b)
    wo = w_out.astype(BF16)
    wu = w_up.astype(BF16)
    wd = w_down.astype(BF16)
    halo_k = cache_k.reshape(depth, bs, WINDOW, A_KV)
    halo_v = cache_v.reshape(depth, bs, WINDOW, A_KV)
    m0 = _m_rows(state_m)

    yp = x_prompt.reshape(bp * T, D_MODEL)
    ys = x_sample.reshape(bs * S, D_MODEL)
    outs = {name: [] for name in ("pk", "pv", "pC", "pn", "pm", "sk", "sv", "sC", "sn", "sm")}
    fg = final_g.reshape(1, D_MODEL)
    for l in range(depth):
        last = fg if l == depth - 1 else None

        q, k, v, m4, gt = _inproj(yp, g1, w1, tabs_p, gb, TOKEN_TILE, l)
        mix, C, n, m_rows = _mixer(q, k, v, m4, gt, zeros_halo, zeros_halo, zeros_c, zeros_n,
                                   zeros_m, attn_sink, mg, batch=bp, L=PROMPT_MLSTM_CHUNK,
                                   halo_valid=False, layer=l, state_layer=0)
        yp = _ffn(yp, mix, wo, g2, wu, wd, last, TOKEN_TILE, l)
        m = m_rows[:, 0, :M_HEADS]
        outs["pk"].append(k.reshape(bp, T, A_KV)[:, T - WINDOW:])
        outs["pv"].append(v.reshape(bp, T, A_KV)[:, T - WINDOW:])
        outs["pC"].append(C)
        outs["pn"].append(n)
        outs["pm"].append(m)

        q, k, v, m4, gt = _inproj(ys, g1, w1, tabs_s, gb, TOKEN_TILE, l)
        mix, C, n, m_rows = _mixer(q, k, v, m4, gt, halo_k, halo_v, state_C, state_n, m0,
                                   attn_sink, mg, batch=bs, L=S, halo_valid=True,
                                   layer=l, state_layer=l)
        ys = _ffn(ys, mix, wo, g2, wu, wd, last, TOKEN_TILE, l)
        m = m_rows[:, 0, :M_HEADS]
        outs["sk"].append(k.reshape(bs, S, A_KV))
        outs["sv"].append(v.reshape(bs, S, A_KV))
        outs["sC"].append(C)
        outs["sn"].append(n)
        outs["sm"].append(m)

    st = lambda name: jnp.stack(outs[name])
    kv = lambda name: (lambda a: a.reshape(a.shape[:-1] + (A_KV_HEADS, A_HEAD_DIM)))(st(name))
    return (yp.reshape(bp, T, D_MODEL), ys.reshape(bs, S, D_MODEL),
            kv("pk"), kv("pv"), st("pC"), st("pn"), st("pm"),
            kv("sk"), kv("sv"), st("sC"), st("sn"), st("sm"))
```

```python
import functools

import jax
import jax.numpy as jnp
from jax import lax
from jax.experimental import pallas as pl
from jax.experimental.pallas import tpu as pltpu

D_MODEL = 1024
CHUNK = 64
A_HEADS = 8
A_KV_HEADS = 2
A_HEAD_DIM = 64
WINDOW = 128
ROPE_THETA = 500000.0
ROPE_DIM = A_HEAD_DIM // 4
ROPE_HALF = ROPE_DIM // 2
M_HEADS = 4
M_HEAD_DIM = 128
D_FF = 4 * D_MODEL
NORM_EPS = 1e-6
NEG = -1e30
PAST_LEN = 4096
A_Q = A_HEADS * A_HEAD_DIM
A_KV = A_KV_HEADS * A_HEAD_DIM
M_W = M_HEADS * M_HEAD_DIM

LANES = 128
SUBLANES = 8
Q_PAD = A_HEADS * LANES
GATE_COLS = 2 * LANES
COL_KV = A_Q
COL_M = A_Q + 2 * A_KV
COL_G = COL_M + 4 * M_W
IN_COLS_PAD = COL_G + GATE_COLS
Q_HEAD_ORDER = (0, 2, 5, 7, 1, 3, 4, 6)

TOKEN_TILE = 512
PROMPT_MLSTM_CHUNK = 256
FF_CHUNK = 1024
VMEM_LIMIT = 56 * 1024 * 1024

BF16 = jnp.bfloat16
F32 = jnp.float32


def _dot(a, b):
    return jnp.dot(a, b, preferred_element_type=F32)


def _dot_nt(a, b):
    return lax.dot_general(a, b, (((1,), (1,)), ((), ())), preferred_element_type=F32)


def _rms(x, g):
    r = lax.rsqrt(jnp.mean(x * x, axis=-1, keepdims=True) + NORM_EPS)
    return x * r * g


def _projection_parts(x_ref, g_ref, w_ref, cos_ref, sa_ref, sb_ref, gb_ref,
                      q_ref, k_ref, v_ref, m4_ref, gt_ref):
    h = _rms(x_ref[...], g_ref[...]).astype(BF16)
    low = lax.broadcasted_iota(jnp.int32, (1, LANES), 1) < A_HEAD_DIM

    def rotary(z):
        left = pltpu.roll(z, LANES - ROPE_HALF, axis=1)
        right = pltpu.roll(z, ROPE_HALF, axis=1)
        return z * cos_ref[...] + left * sa_ref[...] + right * sb_ref[...]

    def attention_part():
        zq_all = _dot(h, w_ref[:, :A_Q])
        zkv = _dot(h, w_ref[:, COL_KV:COL_KV + 2 * A_KV])
        for pair in range(A_Q // LANES):
            zq = rotary(zq_all[:, pair * LANES:(pair + 1) * LANES]) * (A_HEAD_DIM ** -0.5)
            zsw = pltpu.roll(zq, A_HEAD_DIM, axis=1)
            zero = jnp.zeros_like(zq)
            if pair < A_HEADS // (2 * A_KV_HEADS):
                even, odd = jnp.where(low, zq, zero), jnp.where(low, zsw, zero)
            else:
                even, odd = jnp.where(low, zero, zsw), jnp.where(low, zero, zq)
            for head, val in ((2 * pair, even), (2 * pair + 1, odd)):
                j = Q_HEAD_ORDER.index(head)
                q_ref[:, j * LANES:(j + 1) * LANES] = val.astype(BF16)
        k_ref[...] = rotary(zkv[:, :A_KV])
        v_ref[...] = zkv[:, A_KV:]

    def mlstm_qk_part():
        for j in range(2):
            zm = _dot(h, w_ref[:, COL_M + j * M_W:COL_M + (j + 1) * M_W])
            if j == 1:
                zm = zm * (M_HEAD_DIM ** -0.5)
            m4_ref[:, j * M_W:(j + 1) * M_W] = zm.astype(BF16)

    def mlstm_vo_part():
        for j in range(2, 4):
            zm = _dot(h, w_ref[:, COL_M + j * M_W:COL_M + (j + 1) * M_W])
            m4_ref[:, j * M_W:(j + 1) * M_W] = zm.astype(BF16)
        zg = _dot(h, w_ref[:, COL_G:COL_G + GATE_COLS]) + gb_ref[...]
        zf = zg[:, LANES:]
        gt_ref[:, :LANES] = zg[:, :LANES]
        gt_ref[:, LANES:] = jnp.minimum(zf, 0.0) - jnp.log1p(jnp.exp(-jnp.abs(zf)))

    return attention_part, mlstm_qk_part, mlstm_vo_part


def _inproj_kernel(*refs):
    for part in _projection_parts(*refs):
        part()


def _inproj(x, g1, w, tabs, gb, tm, layer):
    n = x.shape[0]
    cos, sa, sb = tabs
    nt = cos.shape[0] // tm
    row = lambda i: (i, 0)
    of_layer = lambda i: (layer, 0, 0)
    tab = lambda i: (i % nt, 0)
    return pl.pallas_call(
        _inproj_kernel,
        grid=(n // tm,),
        in_specs=[
            pl.BlockSpec((tm, D_MODEL), row),
            pl.BlockSpec((None, 1, D_MODEL), of_layer),
            pl.BlockSpec((None, D_MODEL, IN_COLS_PAD), of_layer),
            pl.BlockSpec((tm, LANES), tab),
            pl.BlockSpec((tm, LANES), tab),
            pl.BlockSpec((tm, LANES), tab),
            pl.BlockSpec((None, 1, GATE_COLS), of_layer),
        ],
        out_specs=[
            pl.BlockSpec((tm, Q_PAD), row),
            pl.BlockSpec((tm, A_KV), row),
            pl.BlockSpec((tm, A_KV), row),
            pl.BlockSpec((tm, 4 * M_W), row),
            pl.BlockSpec((tm, GATE_COLS), row),
        ],
        out_shape=[
            jax.ShapeDtypeStruct((n, Q_PAD), BF16),
            jax.ShapeDtypeStruct((n, A_KV), F32),
            jax.ShapeDtypeStruct((n, A_KV), F32),
            jax.ShapeDtypeStruct((n, 4 * M_W), BF16),
            jax.ShapeDtypeStruct((n, GATE_COLS), F32),
        ],
        compiler_params=pltpu.CompilerParams(
            dimension_semantics=("arbitrary",), vmem_limit_bytes=VMEM_LIMIT),
        name="inproj",
    )(x, g1, w, cos, sa, sb, gb)


def _split3(x):
    hi = x.astype(BF16)
    r1 = x - hi.astype(F32)
    mid = r1.astype(BF16)
    lo = (r1 - mid.astype(F32)).astype(BF16)
    return hi, mid, lo


def _mixer_body(t, sink_ref, q_ref, k_ref, v_ref, m4_ref, gt_ref, init_state, mg_ref,
                mix_ref, c_ref, n_ref, m_ref, kbuf, vt, vs, cst,
                *, L, steps, halo_valid, layer, after_first=None, after_second=None):
    n_chunks = L // CHUNK
    n_keys = WINDOW + CHUNK
    half = A_HEAD_DIM

    def put_vt(cols, v):
        v_t = v.T
        vt[:, cols] = v_t.astype(BF16)
        vs[:, cols] = jnp.concatenate([v_t[half:], v_t[:half]], axis=0).astype(BF16)

    @pl.when(t == 0)
    def _():
        vt[...] = jnp.zeros_like(vt)
        vs[...] = jnp.zeros_like(vs)
        init_state(put_vt)

    kbuf[WINDOW:WINDOW + L, :] = k_ref[...].astype(BF16)
    put_vt(pl.ds(WINDOW, L), v_ref[...])

    heads = range(M_HEADS)
    m_slice = lambda j, h: m4_ref[:, j * M_W + h * M_HEAD_DIM:j * M_W + (h + 1) * M_HEAD_DIM]

    scores = []
    for ci in range(n_chunks):
        r0 = ci * CHUNK
        q8 = jnp.concatenate([q_ref[r0:r0 + CHUNK, j * LANES:(j + 1) * LANES]
                              for j in range(A_HEADS)], axis=0)
        scores.append(_dot_nt(kbuf[r0:r0 + n_keys, :], q8))

    gates_i = gt_ref[:, :LANES]
    gates_f = gt_ref[:, LANES:]
    ri = lax.broadcasted_iota(jnp.int32, (L, L), 0)
    ci_ = lax.broadcasted_iota(jnp.int32, (L, L), 1)
    causal = ci_ <= ri
    b3 = _dot(causal.astype(BF16), jnp.concatenate(_split3(gates_f), axis=1))
    b = b3[:, :LANES] + b3[:, LANES:2 * LANES] + b3[:, 2 * LANES:]
    c_old = [cst[h] for h in heads]
    qk = [_dot_nt(m_slice(0, h), m_slice(1, h)) for h in heads]
    qc = [_dot_nt(m_slice(0, h), c_old[h].astype(BF16)) for h in heads]
    if after_first is not None:
        after_first()

    lane_q = lax.broadcasted_iota(jnp.int32, (1, A_HEADS * CHUNK), 1) // CHUNK
    sink_row = jnp.zeros((1, A_HEADS * CHUNK), F32)
    for j, head in enumerate(Q_HEAD_ORDER):
        sink_row = jnp.where(lane_q == j, sink_ref[layer, head], sink_row)
    key_row = lax.broadcasted_iota(jnp.int32, (n_keys, A_HEADS * CHUNK), 0)
    probs, rdens = [], []
    for ci in range(n_chunks):
        s = scores[ci]
        if not halo_valid and ci < WINDOW // CHUNK:
            first_key = (t * n_chunks + ci - WINDOW // CHUNK) * CHUNK
            s = jnp.where(key_row + first_key >= 0, s, NEG)
        mx = jnp.maximum(jnp.max(s, axis=0, keepdims=True), sink_row)
        p = jnp.exp(s - mx)
        rdens.append(1.0 / (jnp.sum(p, axis=0, keepdims=True) + jnp.exp(sink_row - mx)))
        pad = jnp.zeros((CHUNK, A_HEADS * CHUNK), BF16)
        pb = p.astype(BF16)
        probs.append(jnp.concatenate([pb, pad] if ci % 2 == 0 else [pad, pb], axis=0))

    a = gates_i - b
    row_l = lax.broadcasted_iota(jnp.int32, (L, LANES), 0)
    cm = a
    shift = 1
    while shift < L:
        cm = jnp.maximum(cm, jnp.where(row_l >= shift, pltpu.roll(cm, shift, axis=0), NEG))
        shift *= 2
    m_prev = m_ref[0, 0:1, :]
    c = jnp.maximum(cm, m_prev)
    mt = c + b
    w_inter = jnp.exp(m_prev - c)
    e_neg_mt = jnp.exp(-mt)
    m_last = mt[L - 1:L, :]
    b_last = b[L - 1:L, :]
    delta = b_last - m_last
    decay = jnp.exp(b_last + m_prev - m_last)
    a_t = a.T
    ones = jnp.ones((L, M_HEAD_DIM), BF16)
    out_gate = [jax.nn.sigmoid(m_slice(3, h).astype(F32)) for h in heads]
    s_bf, upd_lhs = [], []
    for h in heads:
        a_row = a_t[h:h + 1, :]
        e = jnp.exp(jnp.where(causal, a_row - c[:, h:h + 1], NEG))
        s_bf.append((qk[h] * e).astype(BF16))
        ws_row = jnp.exp(a_row + delta[:, h:h + 1])
        v_t = m_slice(2, h).astype(F32).T
        upd_lhs.append(jnp.concatenate(
            [v_t * ws_row, jnp.broadcast_to(ws_row, (M_HEAD_DIM, L))], axis=0).astype(BF16))

    pv = []
    for ci in range(n_chunks):
        c0 = ci * CHUNK if ci % 2 == 0 else (ci - 1) * CHUNK
        win = slice(c0, c0 + 2 * LANES)
        pv.append((_dot(vt[:, win], probs[ci][:, :2 * LANES]),
                   _dot(vs[:, win], probs[ci][:, 2 * LANES:])))
    sv = [_dot(s_bf[h], jnp.concatenate([m_slice(2, h), ones], axis=1)) for h in heads]
    c_upd = [_dot(upd_lhs[h], m_slice(1, h)) for h in heads]
    if after_second is not None:
        after_second()

    for ci in range(n_chunks):
        r0 = ci * CHUNK
        ra = pv[ci][0] * rdens[ci][:, :2 * LANES]
        rb = pv[ci][1] * rdens[ci][:, 2 * LANES:]
        o0 = jnp.concatenate([ra[:half, :LANES], rb[half:, :LANES]], axis=0).T.astype(BF16)
        o1 = jnp.concatenate([rb[:half, LANES:], ra[half:, LANES:]], axis=0).T.astype(BF16)
        mix_ref[r0:r0 + CHUNK, 0 * LANES:1 * LANES] = o0[:CHUNK]
        mix_ref[r0:r0 + CHUNK, 1 * LANES:2 * LANES] = o0[CHUNK:]
        mix_ref[r0:r0 + CHUNK, 2 * LANES:3 * LANES] = o1[:CHUNK]
        mix_ref[r0:r0 + CHUNK, 3 * LANES:4 * LANES] = o1[CHUNK:]
    for h in heads:
        tot = sv[h] + w_inter[:, h:h + 1] * qc[h]
        hh = tot[:, :M_HEAD_DIM] / jnp.maximum(jnp.abs(tot[:, M_HEAD_DIM:]), e_neg_mt[:, h:h + 1])
        hn = _rms(hh, mg_ref[:, h * M_HEAD_DIM:(h + 1) * M_HEAD_DIM])
        out = hn * out_gate[h]
        mix_ref[:, A_Q + h * M_HEAD_DIM:A_Q + (h + 1) * M_HEAD_DIM] = out.astype(BF16)
        cst[h] = decay[:, h:h + 1] * c_old[h] + c_upd[h]
    m_ref[0] = jnp.broadcast_to(m_last, (SUBLANES, LANES))

    def write_state():
        for h in heads:
            c_ref[0, h] = cst[h, :M_HEAD_DIM, :]
            n_ref[0, h:h + 1, :] = cst[h, M_HEAD_DIM:M_HEAD_DIM + 1, :]

    if steps == 1:
        write_state()
    else:
        pl.when(t == steps - 1)(write_state)

    if steps > 1:
        kbuf[0:WINDOW, :] = kbuf[L:L + WINDOW, :]
        vt[:, 0:WINDOW] = vt[:, L:L + WINDOW]
        vs[:, 0:WINDOW] = vs[:, L:L + WINDOW]


def _mixer_kernel(sink_ref, q_ref, k_ref, v_ref, m4_ref, gt_ref, hk_ref, hv_ref, c0_ref, n0_ref,
                  m0_ref, mg_ref, mix_ref, c_ref, n_ref, m_ref, kbuf, vt, vs, cst, **static):
    def init_state(put_vt):
        kbuf[0:WINDOW, :] = hk_ref[0].astype(BF16)
        put_vt(pl.ds(0, WINDOW), hv_ref[0])
        for h in range(M_HEADS):
            cst[h, :M_HEAD_DIM, :] = c0_ref[0, h]
            cst[h, M_HEAD_DIM:, :] = jnp.broadcast_to(n0_ref[0, h:h + 1, :], (M_HEAD_DIM, M_HEAD_DIM))
        m_ref[...] = m0_ref[...]

    _mixer_body(pl.program_id(1), sink_ref, q_ref, k_ref, v_ref, m4_ref, gt_ref, init_state, mg_ref,
                mix_ref, c_ref, n_ref, m_ref, kbuf, vt, vs, cst, **static)


def _front_kernel(sink_ref, x_ref, g_ref, w_ref, cos_ref, sa_ref, sb_ref, gb_ref, mg_ref,
                  mix_ref, klast_ref, vlast_ref, c_ref, n_ref, m_ref,
                  q_z, k_z, v_z, m4_z, gt_z, kbuf, vt, vs, cst, *, L, steps, n_tiles, layer):
    s = pl.program_id(0)
    wslot = s % 2
    rslot = 1 - wslot
    t = jnp.maximum(s - 1, 0) % steps

    @pl.when(s == 0)
    def _():
        q_z[1] = jnp.zeros(q_z.shape[1:], q_z.dtype)
        k_z[1] = jnp.zeros(k_z.shape[1:], k_z.dtype)
        v_z[1] = jnp.zeros(v_z.shape[1:], v_z.dtype)
        m4_z[1] = jnp.zeros(m4_z.shape[1:], m4_z.dtype)
        gt_z[1] = jnp.zeros(gt_z.shape[1:], gt_z.dtype)

    attention_part, mlstm_qk_part, mlstm_vo_part = _projection_parts(
        x_ref, g_ref, w_ref, cos_ref, sa_ref, sb_ref, gb_ref,
        q_z.at[wslot], k_z.at[wslot], v_z.at[wslot], m4_z.at[wslot], gt_z.at[wslot])

    def after_first():
        attention_part()
        mlstm_qk_part()

    def init_state(put_vt):
        kbuf[0:WINDOW, :] = jnp.zeros((WINDOW, LANES), BF16)
        cst[...] = jnp.zeros_like(cst)
        m_ref[...] = jnp.zeros_like(m_ref)

    _mixer_body(t, sink_ref, q_z.at[rslot], k_z.at[rslot], v_z.at[rslot], m4_z.at[rslot],
                gt_z.at[rslot], init_state, mg_ref, mix_ref, c_ref, n_ref, m_ref, kbuf, vt, vs, cst,
                L=L, steps=steps, halo_valid=False, layer=layer,
                after_first=after_first, after_second=mlstm_vo_part)

    @pl.when(jnp.minimum(s, n_tiles - 1) % steps == steps - 1)
    def _():
        klast_ref[0] = k_z[wslot, L - WINDOW:, :]
        vlast_ref[0] = v_z[wslot, L - WINDOW:, :]


def _front(x, g1, w, tabs, gb, sink, mg, *, batch, L, layer):
    n = x.shape[0]
    n_tiles = n // L
    steps = n_tiles // batch
    assert steps > 1 and L % LANES == 0 and L >= WINDOW
    cos, sa, sb = tabs
    assert cos.shape[0] == steps * L
    proj = lambda s: jnp.minimum(s, n_tiles - 1)
    mixed = lambda s: jnp.maximum(s - 1, 0)
    of_layer = lambda s: (layer, 0, 0)
    tab = lambda s: (proj(s) % steps, 0)
    seq3 = lambda s: (mixed(s) // steps, 0, 0)
    last_rows = lambda s: (proj(s) // steps, 0, 0)
    kv_cols = WINDOW + L
    return pl.pallas_call(
        functools.partial(_front_kernel, L=L, steps=steps, n_tiles=n_tiles, layer=layer),
        grid=(n_tiles + 1,),
        in_specs=[
            pl.BlockSpec(memory_space=pltpu.SMEM),
            pl.BlockSpec((L, D_MODEL), lambda s: (proj(s), 0)),
            pl.BlockSpec((None, 1, D_MODEL), of_layer),
            pl.BlockSpec((None, D_MODEL, IN_COLS_PAD), of_layer, pipeline_mode=pl.Buffered(1)),
            pl.BlockSpec((L, LANES), tab),
            pl.BlockSpec((L, LANES), tab),
            pl.BlockSpec((L, LANES), tab),
            pl.BlockSpec((None, 1, GATE_COLS), of_layer),
            pl.BlockSpec((None, 1, M_W), of_layer),
        ],
        out_specs=[
            pl.BlockSpec((L, D_MODEL), lambda s: (mixed(s), 0)),
            pl.BlockSpec((1, WINDOW, A_KV), last_rows),
            pl.BlockSpec((1, WINDOW, A_KV), last_rows),
            pl.BlockSpec((1, M_HEADS, M_HEAD_DIM, M_HEAD_DIM), lambda s: (mixed(s) // steps, 0, 0, 0)),
            pl.BlockSpec((1, M_HEADS, M_HEAD_DIM), seq3),
            pl.BlockSpec((1, SUBLANES, LANES), seq3),
        ],
        out_shape=[
            jax.ShapeDtypeStruct((n, D_MODEL), BF16),
            jax.ShapeDtypeStruct((batch, WINDOW, A_KV), F32),
            jax.ShapeDtypeStruct((batch, WINDOW, A_KV), F32),
            jax.ShapeDtypeStruct((batch, M_HEADS, M_HEAD_DIM, M_HEAD_DIM), F32),
            jax.ShapeDtypeStruct((batch, M_HEADS, M_HEAD_DIM), F32),
            jax.ShapeDtypeStruct((batch, SUBLANES, LANES), F32),
        ],
        scratch_shapes=[
            pltpu.VMEM((2, L, Q_PAD), BF16),
            pltpu.VMEM((2, L, A_KV), F32),
            pltpu.VMEM((2, L, A_KV), F32),
            pltpu.VMEM((2, L, 4 * M_W), BF16),
            pltpu.VMEM((2, L, GATE_COLS), F32),
            pltpu.VMEM((WINDOW + L, LANES), BF16),
            pltpu.VMEM((LANES, kv_cols), BF16),
            pltpu.VMEM((LANES, kv_cols), BF16),
            pltpu.VMEM((M_HEADS, 2 * M_HEAD_DIM, M_HEAD_DIM), F32),
        ],
        compiler_params=pltpu.CompilerParams(
            dimension_semantics=("arbitrary",), vmem_limit_bytes=VMEM_LIMIT),
        name="front",
    )(sink, x, g1, w, cos, sa, sb, gb, mg)


def _mixer(q, k, v, m4, gt, halo_k, halo_v, c0, n0, m0, sink, mg,
           *, batch, L, halo_valid, layer, state_layer):
    n = q.shape[0]
    steps = n // (batch * L)
    assert steps == 1 or L % LANES == 0
    row = lambda b, t: (b * steps + t, 0)
    per_b3 = lambda b, t: (b, 0, 0)
    per_b4 = lambda b, t: (b, 0, 0, 0)
    init3 = lambda b, t: (state_layer, b, 0, 0)
    init4 = lambda b, t: (state_layer, b, 0, 0, 0)
    kv_cols = max(WINDOW + L, 2 * LANES)
    return pl.pallas_call(
        functools.partial(_mixer_kernel, L=L, steps=steps, halo_valid=halo_valid, layer=layer),
        grid=(batch, steps),
        in_specs=[
            pl.BlockSpec(memory_space=pltpu.SMEM),
            pl.BlockSpec((L, Q_PAD), row),
            pl.BlockSpec((L, A_KV), row),
            pl.BlockSpec((L, A_KV), row),
            pl.BlockSpec((L, 4 * M_W), row),
            pl.BlockSpec((L, GATE_COLS), row),
            pl.BlockSpec((None, 1, WINDOW, A_KV), init3),
            pl.BlockSpec((None, 1, WINDOW, A_KV), init3),
            pl.BlockSpec((None, 1, M_HEADS, M_HEAD_DIM, M_HEAD_DIM), init4),
            pl.BlockSpec((None, 1, M_HEADS, M_HEAD_DIM), init3),
            pl.BlockSpec((None, 1, SUBLANES, LANES), init3),
            pl.BlockSpec((None, 1, M_W), lambda b, t: (layer, 0, 0)),
        ],
        out_specs=[
            pl.BlockSpec((L, D_MODEL), row),
            pl.BlockSpec((1, M_HEADS, M_HEAD_DIM, M_HEAD_DIM), per_b4),
            pl.BlockSpec((1, M_HEADS, M_HEAD_DIM), per_b3),
            pl.BlockSpec((1, SUBLANES, LANES), per_b3),
        ],
        out_shape=[
            jax.ShapeDtypeStruct((n, D_MODEL), BF16),
            jax.ShapeDtypeStruct((batch, M_HEADS, M_HEAD_DIM, M_HEAD_DIM), F32),
            jax.ShapeDtypeStruct((batch, M_HEADS, M_HEAD_DIM), F32),
            jax.ShapeDtypeStruct((batch, SUBLANES, LANES), F32),
        ],
        scratch_shapes=[
            pltpu.VMEM((WINDOW + L, LANES), BF16),
            pltpu.VMEM((LANES, kv_cols), BF16),
            pltpu.VMEM((LANES, kv_cols), BF16),
            pltpu.VMEM((M_HEADS, 2 * M_HEAD_DIM, M_HEAD_DIM), F32),
        ],
        compiler_params=pltpu.CompilerParams(
            dimension_semantics=("arbitrary", "arbitrary"), vmem_limit_bytes=VMEM_LIMIT),
        name="mixer",
    )(sink, q, k, v, m4, gt, halo_k, halo_v, c0, n0, m0, mg)


def _ffn_kernel(*refs, final):
    if final:
        x_ref, mix_ref, wo_ref, g2_ref, wu_ref, wd_ref, fg_ref, out_ref = refs
    else:
        x_ref, mix_ref, wo_ref, g2_ref, wu_ref, wd_ref, out_ref = refs
    x1 = x_ref[...] + _dot(mix_ref[...], wo_ref[...])
    xn = _rms(x1, g2_ref[...]).astype(BF16)
    acc = x1
    for c in range(D_FF // FF_CHUNK):
        u = _dot(xn, wu_ref[:, c * FF_CHUNK:(c + 1) * FF_CHUNK])
        a = jnp.square(jnp.maximum(u, 0.0)).astype(BF16)
        acc = acc + _dot(a, wd_ref[c * FF_CHUNK:(c + 1) * FF_CHUNK, :])
    if final:
        acc = _rms(acc, fg_ref[...])
    out_ref[...] = acc


def _ffn(x, mix, wo, g2, wu, wd, fg, tm, layer):
    n = x.shape[0]
    final = fg is not None
    row = lambda i: (i, 0)
    fixed = lambda i: (0, 0)
    of_layer = lambda i: (layer, 0, 0)
    single = pl.Buffered(1)
    in_specs = [
        pl.BlockSpec((tm, D_MODEL), row),
        pl.BlockSpec((tm, D_MODEL), row),
        pl.BlockSpec((None, D_MODEL, D_MODEL), of_layer, pipeline_mode=single),
        pl.BlockSpec((None, 1, D_MODEL), of_layer),
        pl.BlockSpec((None, D_MODEL, D_FF), of_layer, pipeline_mode=single),
        pl.BlockSpec((None, D_FF, D_MODEL), of_layer, pipeline_mode=single),
    ]
    args = [x, mix, wo, g2, wu, wd]
    if final:
        in_specs.append(pl.BlockSpec((1, D_MODEL), fixed))
        args.append(fg)
    return pl.pallas_call(
        functools.partial(_ffn_kernel, final=final),
        grid=(n // tm,),
        in_specs=in_specs,
        out_specs=pl.BlockSpec((tm, D_MODEL), row),
        out_shape=jax.ShapeDtypeStruct((n, D_MODEL), F32),
        compiler_params=pltpu.CompilerParams(
            dimension_semantics=("arbitrary",), vmem_limit_bytes=VMEM_LIMIT),
        name="ffn_final" if final else "ffn",
    )(*args)


def _rotary_tables(pos, rows):
    inv = ROPE_THETA ** (-(jnp.arange(ROPE_HALF, dtype=F32) * 2.0 / ROPE_DIM))
    ang = pos[:, None] * inv[None, :]
    cos, sin = jnp.cos(ang), jnp.sin(ang)
    n = pos.shape[0]
    pad = jnp.zeros((n, A_HEAD_DIM - ROPE_DIM), F32)
    zeros = jnp.zeros_like(sin)
    cos_h = jnp.concatenate([cos, cos, pad + 1.0], axis=1)
    sa_h = jnp.concatenate([-sin, zeros, pad], axis=1)
    sb_h = jnp.concatenate([zeros, sin, pad], axis=1)
    rep = LANES // A_HEAD_DIM
    tile = lambda a: jnp.tile(a, (rows // n, rep))
    return tile(cos_h), tile(sa_h), tile(sb_h)


def _pad_w_in(w):
    gate_pad = jnp.zeros(w.shape[:2] + (LANES - M_HEADS,), w.dtype)
    return jnp.concatenate(
        [w[..., :COL_G], w[..., COL_G:COL_G + M_HEADS], gate_pad, w[..., COL_G + M_HEADS:], gate_pad],
        axis=-1).astype(BF16)


def _pad_gate_bias(gb):
    pad = jnp.zeros((gb.shape[0], LANES - M_HEADS), gb.dtype)
    return jnp.concatenate([gb[:, :M_HEADS], pad, gb[:, M_HEADS:], pad], axis=1)[:, None, :]


def _m_rows(m):
    m_row = jnp.pad(m, [(0, 0)] * (m.ndim - 1) + [(0, LANES - M_HEADS)])
    return jnp.broadcast_to(m_row[..., None, :], m.shape[:-1] + (SUBLANES, LANES))


def kernel(x_prompt, x_sample, cache_k, cache_v, state_C, state_n, state_m, norm1_g, w_in, gate_b,
           attn_sink, mnorm_g, w_out, norm2_g, w_up, w_down, final_g):
    bp, T, _ = x_prompt.shape
    bs, S, _ = x_sample.shape
    depth = w_in.shape[0]
    assert T % PROMPT_MLSTM_CHUNK == 0 and S == CHUNK
    assert T % TOKEN_TILE == 0 and TOKEN_TILE % S == 0 and (bs * S) % TOKEN_TILE == 0

    tabs_p = _rotary_tables(jnp.arange(T, dtype=F32), max(T, TOKEN_TILE))
    tabs_s = _rotary_tables(jnp.arange(S, dtype=F32) + float(PAST_LEN), max(S, TOKEN_TILE))

    g1 = norm1_g[:, None, :]
    g2 = norm2_g[:, None, :]
    mg = mnorm_g[:, None, :]
    w1 = _pad_w_in(w_in)
    gb = _pad_gate_bias(gate_b)
    wo = w_out.astype(BF16)
    wu = w_up.astype(BF16)
    wd = w_down.astype(BF16)
    halo_k = cache_k.reshape(depth, bs, WINDOW, A_KV)
    halo_v = cache_v.reshape(depth, bs, WINDOW, A_KV)
    m0 = _m_rows(state_m)

    yp = x_prompt.reshape(bp * T, D_MODEL)
    ys = x_sample.reshape(bs * S, D_MODEL)
    outs = {name: [] for name in ("pk", "pv", "pC", "pn", "pm", "sk", "sv", "sC", "sn", "sm")}
    fg = final_g.reshape(1, D_MODEL)
    for l in range(depth):
        last = fg if l == depth - 1 else None

        mix, k_last, v_last, C, n, m_rows = _front(yp, g1, w1, tabs_p, gb, attn_sink, mg,
                                                   batch=bp, L=PROMPT_MLSTM_CHUNK, layer=l)
        yp = _ffn(yp, mix, wo, g2, wu, wd, last, TOKEN_TILE, l)
        m = m_rows[:, 0, :M_HEADS]
        outs["pk"].append(k_last)
        outs["pv"].append(v_last)
        outs["pC"].append(C)
        outs["pn"].append(n)
        outs["pm"].append(m)

        q, k, v, m4, gt = _inproj(ys, g1, w1, tabs_s, gb, TOKEN_TILE, l)
        mix, C, n, m_rows = _mixer(q, k, v, m4, gt, halo_k, halo_v, state_C, state_n, m0,
                                   attn_sink, mg, batch=bs, L=S, halo_valid=True,
                                   layer=l, state_layer=l)
        ys = _ffn(ys, mix, wo, g2, wu, wd, last, TOKEN_TILE, l)
        m = m_rows[:, 0, :M_HEADS]
        outs["sk"].append(k.reshape(bs, S, A_KV))
        outs["sv"].append(v.reshape(bs, S, A_KV))
        outs["sC"].append(C)
        outs["sn"].append(n)
        outs["sm"].append(m)

    st = lambda name: jnp.stack(outs[name])
    kv = lambda name: (lambda a: a.reshape(a.shape[:-1] + (A_KV_HEADS, A_HEAD_DIM)))(st(name))
    return (yp.reshape(bp, T, D_MODEL), ys.reshape(bs, S, D_MODEL),
            kv("pk"), kv("pv"), st("pC"), st("pn"), st("pm"),
            kv("sk"), kv("sv"), st("sC"), st("sn"), st("sm"))
```

```python
import functools

import jax
import jax.numpy as jnp
from jax import lax
from jax.experimental import pallas as pl
from jax.experimental.pallas import tpu as pltpu

D_MODEL = 1024
CHUNK = 64
A_HEADS = 8
A_KV_HEADS = 2
A_HEAD_DIM = 64
WINDOW = 128
ROPE_THETA = 500000.0
ROPE_DIM = A_HEAD_DIM // 4
ROPE_HALF = ROPE_DIM // 2
M_HEADS = 4
M_HEAD_DIM = 128
D_FF = 4 * D_MODEL
NORM_EPS = 1e-6
NEG = -1e30
LOG2E = 1.4426950408889634
PAST_LEN = 4096
A_Q = A_HEADS * A_HEAD_DIM
A_KV = A_KV_HEADS * A_HEAD_DIM
M_W = M_HEADS * M_HEAD_DIM

LANES = 128
SUBLANES = 8
Q_PAD = A_HEADS * LANES
GATE_COLS = 2 * LANES
COL_KV = A_Q
COL_M = A_Q + 2 * A_KV
COL_G = COL_M + 4 * M_W
IN_COLS_PAD = COL_G + GATE_COLS
Q_HEAD_ORDER = (0, 2, 5, 7, 1, 3, 4, 6)

TOKEN_TILE = 512
PROMPT_MLSTM_CHUNK = 256
SEQ_GROUP = 2
FF_CHUNK = 1024
VMEM_LIMIT = 56 * 1024 * 1024

BF16 = jnp.bfloat16
F32 = jnp.float32


def _dot(a, b):
    return jnp.dot(a, b, preferred_element_type=F32)


def _dot_nt(a, b):
    return lax.dot_general(a, b, (((1,), (1,)), ((), ())), preferred_element_type=F32)


def _rms(x, g):
    r = lax.rsqrt(jnp.mean(x * x, axis=-1, keepdims=True) + NORM_EPS)
    return x * r * g


def _projection_parts(x_ref, g_ref, w_ref, cos_ref, sa_ref, sb_ref, gb_ref,
                      q_ref, k_ref, v_ref, m4_ref, gt_ref):
    h = _rms(x_ref[...], g_ref[...]).astype(BF16)
    low = lax.broadcasted_iota(jnp.int32, (1, LANES), 1) < A_HEAD_DIM

    def rotary(z):
        left = pltpu.roll(z, LANES - ROPE_HALF, axis=1)
        right = pltpu.roll(z, ROPE_HALF, axis=1)
        return z * cos_ref[...] + left * sa_ref[...] + right * sb_ref[...]

    def attention_part():
        zq_all = _dot(h, w_ref[:, :A_Q])
        zkv = _dot(h, w_ref[:, COL_KV:COL_KV + 2 * A_KV])
        for pair in range(A_Q // LANES):
            zq = rotary(zq_all[:, pair * LANES:(pair + 1) * LANES]) * (A_HEAD_DIM ** -0.5 * LOG2E)
            zsw = pltpu.roll(zq, A_HEAD_DIM, axis=1)
            zero = jnp.zeros_like(zq)
            if pair < A_HEADS // (2 * A_KV_HEADS):
                even, odd = jnp.where(low, zq, zero), jnp.where(low, zsw, zero)
            else:
                even, odd = jnp.where(low, zero, zsw), jnp.where(low, zero, zq)
            for head, val in ((2 * pair, even), (2 * pair + 1, odd)):
                j = Q_HEAD_ORDER.index(head)
                q_ref[:, j * LANES:(j + 1) * LANES] = val.astype(BF16)
        k_ref[...] = rotary(zkv[:, :A_KV])
        v_ref[...] = zkv[:, A_KV:]

    def mlstm_qk_part():
        for j in range(2):
            zm = _dot(h, w_ref[:, COL_M + j * M_W:COL_M + (j + 1) * M_W])
            if j == 1:
                zm = zm * (M_HEAD_DIM ** -0.5)
            m4_ref[:, j * M_W:(j + 1) * M_W] = zm.astype(BF16)

    def mlstm_vo_part():
        for j in range(2, 4):
            zm = _dot(h, w_ref[:, COL_M + j * M_W:COL_M + (j + 1) * M_W])
            m4_ref[:, j * M_W:(j + 1) * M_W] = zm.astype(BF16)
        zg = _dot(h, w_ref[:, COL_G:COL_G + GATE_COLS]) + gb_ref[...]
        zf = zg[:, LANES:]
        gt_ref[:, :LANES] = zg[:, :LANES]
        gt_ref[:, LANES:] = jnp.minimum(zf, 0.0) - jnp.log1p(jnp.exp(-jnp.abs(zf)))

    return attention_part, mlstm_qk_part, mlstm_vo_part


def _inproj_kernel(*refs):
    for part in _projection_parts(*refs):
        part()


def _inproj(x, g1, w, tabs, gb, tm, layer):
    n = x.shape[0]
    cos, sa, sb = tabs
    nt = cos.shape[0] // tm
    row = lambda i: (i, 0)
    of_layer = lambda i: (layer, 0, 0)
    tab = lambda i: (i % nt, 0)
    return pl.pallas_call(
        _inproj_kernel,
        grid=(n // tm,),
        in_specs=[
            pl.BlockSpec((tm, D_MODEL), row),
            pl.BlockSpec((None, 1, D_MODEL), of_layer),
            pl.BlockSpec((None, D_MODEL, IN_COLS_PAD), of_layer),
            pl.BlockSpec((tm, LANES), tab),
            pl.BlockSpec((tm, LANES), tab),
            pl.BlockSpec((tm, LANES), tab),
            pl.BlockSpec((None, 1, GATE_COLS), of_layer),
        ],
        out_specs=[
            pl.BlockSpec((tm, Q_PAD), row),
            pl.BlockSpec((tm, A_KV), row),
            pl.BlockSpec((tm, A_KV), row),
            pl.BlockSpec((tm, 4 * M_W), row),
            pl.BlockSpec((tm, GATE_COLS), row),
        ],
        out_shape=[
            jax.ShapeDtypeStruct((n, Q_PAD), BF16),
            jax.ShapeDtypeStruct((n, A_KV), F32),
            jax.ShapeDtypeStruct((n, A_KV), F32),
            jax.ShapeDtypeStruct((n, 4 * M_W), BF16),
            jax.ShapeDtypeStruct((n, GATE_COLS), F32),
        ],
        compiler_params=pltpu.CompilerParams(
            dimension_semantics=("arbitrary",), vmem_limit_bytes=VMEM_LIMIT),
        name="inproj",
    )(x, g1, w, cos, sa, sb, gb)


def _split3(x):
    hi = x.astype(BF16)
    r1 = x - hi.astype(F32)
    mid = r1.astype(BF16)
    lo = (r1 - mid.astype(F32)).astype(BF16)
    return hi, mid, lo


def _mixer_stages(t, sink_ref, q_ref, k_ref, v_ref, m4_ref, gt_ref, init_state, mg_ref,
                  mix_ref, c_ref, n_ref, m_ref, kbuf, vt, vs, cst,
                  *, L, steps, halo_valid, layer):
    n_chunks = L // CHUNK
    n_keys = WINDOW + CHUNK
    half = A_HEAD_DIM

    def put_vt(cols, v):
        v_t = v.T
        vt[:, cols] = v_t.astype(BF16)
        vs[:, cols] = jnp.concatenate([v_t[half:], v_t[:half]], axis=0).astype(BF16)

    @pl.when(t == 0)
    def _():
        vt[...] = jnp.zeros_like(vt)
        vs[...] = jnp.zeros_like(vs)
        init_state(put_vt)

    yield

    kbuf[WINDOW:WINDOW + L, :] = k_ref[...].astype(BF16)
    put_vt(pl.ds(WINDOW, L), v_ref[...])

    heads = range(M_HEADS)
    m_slice = lambda j, h: m4_ref[:, j * M_W + h * M_HEAD_DIM:j * M_W + (h + 1) * M_HEAD_DIM]

    scores = []
    for ci in range(n_chunks):
        r0 = ci * CHUNK
        q8 = jnp.concatenate([q_ref[r0:r0 + CHUNK, j * LANES:(j + 1) * LANES]
                              for j in range(A_HEADS)], axis=0)
        scores.append(_dot_nt(kbuf[r0:r0 + n_keys, :], q8))

    gates_i = gt_ref[:, :LANES]
    gates_f = gt_ref[:, LANES:]
    ri = lax.broadcasted_iota(jnp.int32, (L, L), 0)
    ci_ = lax.broadcasted_iota(jnp.int32, (L, L), 1)
    causal = ci_ <= ri
    b3 = _dot(causal.astype(BF16), jnp.concatenate(_split3(gates_f), axis=1))
    b = b3[:, :LANES] + b3[:, LANES:2 * LANES] + b3[:, 2 * LANES:]
    c_old = [cst[h] for h in heads]
    qk = [_dot_nt(m_slice(0, h), m_slice(1, h)) for h in heads]
    qc = [_dot_nt(m_slice(0, h), c_old[h].astype(BF16)) for h in heads]
    yield

    lane_q = lax.broadcasted_iota(jnp.int32, (1, A_HEADS * CHUNK), 1) // CHUNK
    sink_row = jnp.zeros((1, A_HEADS * CHUNK), F32)
    for j, head in enumerate(Q_HEAD_ORDER):
        sink_row = jnp.where(lane_q == j, sink_ref[layer, head] * LOG2E, sink_row)
    key_row = lax.broadcasted_iota(jnp.int32, (n_keys, A_HEADS * CHUNK), 0)
    probs, rdens = [], []
    for ci in range(n_chunks):
        s = scores[ci]
        if not halo_valid and ci < WINDOW // CHUNK:
            first_key = (t * n_chunks + ci - WINDOW // CHUNK) * CHUNK
            s = jnp.where(key_row + first_key >= 0, s, NEG)
        mx = jnp.maximum(jnp.max(s, axis=0, keepdims=True), sink_row)
        p = jnp.exp2(s - mx)
        rdens.append(1.0 / (jnp.sum(p, axis=0, keepdims=True) + jnp.exp2(sink_row - mx)))
        pad = jnp.zeros((CHUNK, A_HEADS * CHUNK), BF16)
        pb = p.astype(BF16)
        probs.append(jnp.concatenate([pb, pad] if ci % 2 == 0 else [pad, pb], axis=0))

    a = gates_i - b
    row_l = lax.broadcasted_iota(jnp.int32, (L, LANES), 0)
    cm = a
    shift = 1
    while shift < L:
        cm = jnp.maximum(cm, jnp.where(row_l >= shift, pltpu.roll(cm, shift, axis=0), NEG))
        shift *= 2
    m_prev = m_ref[0, 0:1, :]
    c = jnp.maximum(cm, m_prev)
    mt = c + b
    w_inter = jnp.exp(m_prev - c)
    e_neg_mt = jnp.exp(-mt)
    m_last = mt[L - 1:L, :]
    b_last = b[L - 1:L, :]
    delta = b_last - m_last
    decay = jnp.exp(b_last + m_prev - m_last)
    a_t = (a * LOG2E).T
    c2 = c * LOG2E
    delta2 = delta * LOG2E
    blk = min(L, LANES)
    n_blk = L // blk
    tri = (lax.broadcasted_iota(jnp.int32, (blk, blk), 1)
           <= lax.broadcasted_iota(jnp.int32, (blk, blk), 0))
    ones = jnp.ones((L, M_HEAD_DIM), BF16)
    out_gate = [jax.nn.sigmoid(m_slice(3, h).astype(F32)) for h in heads]
    s_bf, upd_lhs = [], []
    for h in heads:
        a_row = a_t[h:h + 1, :]
        row_blocks = []
        for rb in range(n_blk):
            rows = slice(rb * blk, (rb + 1) * blk)
            c_col = c2[rows, h:h + 1]
            e = []
            for cb in range(rb + 1):
                d = a_row[:, cb * blk:(cb + 1) * blk] - c_col
                e.append(jnp.exp2(jnp.where(tri, d, NEG) if cb == rb else d))
            s = (qk[h][rows, :(rb + 1) * blk] * jnp.concatenate(e, axis=1)).astype(BF16)
            if rb + 1 < n_blk:
                s = jnp.concatenate([s, jnp.zeros((blk, L - (rb + 1) * blk), BF16)], axis=1)
            row_blocks.append(s)
        s_bf.append(jnp.concatenate(row_blocks, axis=0))
        ws_row = jnp.exp2(a_row + delta2[:, h:h + 1])
        v_t = m_slice(2, h).astype(F32).T
        upd_lhs.append(jnp.concatenate(
            [v_t * ws_row, jnp.broadcast_to(ws_row, (M_HEAD_DIM, L))], axis=0).astype(BF16))

    yield

    pv = []
    for ci in range(n_chunks):
        c0 = ci * CHUNK if ci % 2 == 0 else (ci - 1) * CHUNK
        win = slice(c0, c0 + 2 * LANES)
        pv.append((_dot(vt[:, win], probs[ci][:, :2 * LANES]),
                   _dot(vs[:, win], probs[ci][:, 2 * LANES:])))
    sv = [_dot(s_bf[h], jnp.concatenate([m_slice(2, h), ones], axis=1)) for h in heads]
    c_upd = [_dot(upd_lhs[h], m_slice(1, h)) for h in heads]
    yield

    for ci in range(n_chunks):
        r0 = ci * CHUNK
        ra = pv[ci][0] * rdens[ci][:, :2 * LANES]
        rb = pv[ci][1] * rdens[ci][:, 2 * LANES:]
        o0 = jnp.concatenate([ra[:half, :LANES], rb[half:, :LANES]], axis=0).T.astype(BF16)
        o1 = jnp.concatenate([rb[:half, LANES:], ra[half:, LANES:]], axis=0).T.astype(BF16)
        mix_ref[r0:r0 + CHUNK, 0 * LANES:1 * LANES] = o0[:CHUNK]
        mix_ref[r0:r0 + CHUNK, 1 * LANES:2 * LANES] = o0[CHUNK:]
        mix_ref[r0:r0 + CHUNK, 2 * LANES:3 * LANES] = o1[:CHUNK]
        mix_ref[r0:r0 + CHUNK, 3 * LANES:4 * LANES] = o1[CHUNK:]
    for h in heads:
        tot = sv[h] + w_inter[:, h:h + 1] * qc[h]
        hh = tot[:, :M_HEAD_DIM] / jnp.maximum(jnp.abs(tot[:, M_HEAD_DIM:]), e_neg_mt[:, h:h + 1])
        hn = _rms(hh, mg_ref[:, h * M_HEAD_DIM:(h + 1) * M_HEAD_DIM])
        out = hn * out_gate[h]
        mix_ref[:, A_Q + h * M_HEAD_DIM:A_Q + (h + 1) * M_HEAD_DIM] = out.astype(BF16)
        cst[h] = decay[:, h:h + 1] * c_old[h] + c_upd[h]
    m_ref[0] = jnp.broadcast_to(m_last, (SUBLANES, LANES))
    yield

    def write_state():
        for h in heads:
            c_ref[0, h] = cst[h, :M_HEAD_DIM, :]
            n_ref[0, h:h + 1, :] = cst[h, M_HEAD_DIM:M_HEAD_DIM + 1, :]

    if steps == 1:
        write_state()
    else:
        pl.when(t == steps - 1)(write_state)

    if steps > 1:
        kbuf[0:WINDOW, :] = kbuf[L:L + WINDOW, :]
        vt[:, 0:WINDOW] = vt[:, L:L + WINDOW]
        vs[:, 0:WINDOW] = vs[:, L:L + WINDOW]


def _interleave(gens):
    done = object()
    live = list(gens)
    while live:
        live = [g for g in live if next(g, done) is not done]


def _mixer_kernel(sink_ref, q_ref, k_ref, v_ref, m4_ref, gt_ref, hk_ref, hv_ref, c0_ref, n0_ref,
                  m0_ref, mg_ref, mix_ref, c_ref, n_ref, m_ref, kbuf, vt, vs, cst,
                  *, group, **static):
    def stages(i):
        one = lambda ref: ref.at[pl.ds(i, 1)]

        def init_state(put_vt):
            kbuf[i, 0:WINDOW, :] = hk_ref[i].astype(BF16)
            put_vt(pl.ds(0, WINDOW), hv_ref[i])
            for h in range(M_HEADS):
                cst[i, h, :M_HEAD_DIM, :] = c0_ref[i, h]
                cst[i, h, M_HEAD_DIM:, :] = jnp.broadcast_to(
                    n0_ref[i, h:h + 1, :], (M_HEAD_DIM, M_HEAD_DIM))
            m_ref[i] = m0_ref[i]

        return _mixer_stages(
            pl.program_id(1), sink_ref, q_ref.at[i], k_ref.at[i], v_ref.at[i], m4_ref.at[i],
            gt_ref.at[i], init_state, mg_ref, mix_ref.at[i], one(c_ref), one(n_ref), one(m_ref),
            kbuf.at[i], vt.at[i], vs.at[i], cst.at[i], **static)

    _interleave([stages(i) for i in range(group)])


def _mixer(q, k, v, m4, gt, halo_k, halo_v, c0, n0, m0, sink, mg,
           *, batch, group, L, halo_valid, layer, state_layer):
    n = q.shape[0]
    T = n // batch
    steps = T // L
    assert steps == 1 or L % LANES == 0
    assert batch % group == 0
    tile = lambda b, t: (b, t, 0)
    per_b3 = lambda b, t: (b, 0, 0)
    per_b4 = lambda b, t: (b, 0, 0, 0)
    init3 = lambda b, t: (state_layer, b, 0, 0)
    init4 = lambda b, t: (state_layer, b, 0, 0, 0)
    kv_cols = max(WINDOW + L, 2 * LANES)
    by_seq = lambda a: a.reshape(batch, T, a.shape[-1])
    mix, C, n_out, m_out = pl.pallas_call(
        functools.partial(_mixer_kernel, group=group, L=L, steps=steps, halo_valid=halo_valid,
                          layer=layer),
        grid=(batch // group, steps),
        in_specs=[
            pl.BlockSpec(memory_space=pltpu.SMEM),
            pl.BlockSpec((group, L, Q_PAD), tile),
            pl.BlockSpec((group, L, A_KV), tile),
            pl.BlockSpec((group, L, A_KV), tile),
            pl.BlockSpec((group, L, 4 * M_W), tile),
            pl.BlockSpec((group, L, GATE_COLS), tile),
            pl.BlockSpec((None, group, WINDOW, A_KV), init3),
            pl.BlockSpec((None, group, WINDOW, A_KV), init3),
            pl.BlockSpec((None, group, M_HEADS, M_HEAD_DIM, M_HEAD_DIM), init4),
            pl.BlockSpec((None, group, M_HEADS, M_HEAD_DIM), init3),
            pl.BlockSpec((None, group, SUBLANES, LANES), init3),
            pl.BlockSpec((None, 1, M_W), lambda b, t: (layer, 0, 0)),
        ],
        out_specs=[
            pl.BlockSpec((group, L, D_MODEL), tile),
            pl.BlockSpec((group, M_HEADS, M_HEAD_DIM, M_HEAD_DIM), per_b4),
            pl.BlockSpec((group, M_HEADS, M_HEAD_DIM), per_b3),
            pl.BlockSpec((group, SUBLANES, LANES), per_b3),
        ],
        out_shape=[
            jax.ShapeDtypeStruct((batch, T, D_MODEL), BF16),
            jax.ShapeDtypeStruct((batch, M_HEADS, M_HEAD_DIM, M_HEAD_DIM), F32),
            jax.ShapeDtypeStruct((batch, M_HEADS, M_HEAD_DIM), F32),
            jax.ShapeDtypeStruct((batch, SUBLANES, LANES), F32),
        ],
        scratch_shapes=[
            pltpu.VMEM((group, WINDOW + L, LANES), BF16),
            pltpu.VMEM((group, LANES, kv_cols), BF16),
            pltpu.VMEM((group, LANES, kv_cols), BF16),
            pltpu.VMEM((group, M_HEADS, 2 * M_HEAD_DIM, M_HEAD_DIM), F32),
        ],
        compiler_params=pltpu.CompilerParams(
            dimension_semantics=("arbitrary", "arbitrary"), vmem_limit_bytes=VMEM_LIMIT),
        name="mixer",
    )(sink, by_seq(q), by_seq(k), by_seq(v), by_seq(m4), by_seq(gt), halo_k, halo_v, c0, n0, m0, mg)
    return mix.reshape(n, D_MODEL), C, n_out, m_out


def _ffn_kernel(*refs, final):
    if final:
        x_ref, mix_ref, wo_ref, g2_ref, wu_ref, wd_ref, fg_ref, out_ref = refs
    else:
        x_ref, mix_ref, wo_ref, g2_ref, wu_ref, wd_ref, out_ref = refs
    x1 = x_ref[...] + _dot(mix_ref[...], wo_ref[...])
    xn = _rms(x1, g2_ref[...]).astype(BF16)
    acc = x1
    for c in range(D_FF // FF_CHUNK):
        u = _dot(xn, wu_ref[:, c * FF_CHUNK:(c + 1) * FF_CHUNK])
        a = jnp.square(jnp.maximum(u, 0.0)).astype(BF16)
        acc = acc + _dot(a, wd_ref[c * FF_CHUNK:(c + 1) * FF_CHUNK, :])
    if final:
        acc = _rms(acc, fg_ref[...])
    out_ref[...] = acc


def _ffn(x, mix, wo, g2, wu, wd, fg, tm, layer):
    n = x.shape[0]
    final = fg is not None
    row = lambda i: (i, 0)
    fixed = lambda i: (0, 0)
    of_layer = lambda i: (layer, 0, 0)
    single = pl.Buffered(1)
    in_specs = [
        pl.BlockSpec((tm, D_MODEL), row),
        pl.BlockSpec((tm, D_MODEL), row),
        pl.BlockSpec((None, D_MODEL, D_MODEL), of_layer, pipeline_mode=single),
        pl.BlockSpec((None, 1, D_MODEL), of_layer),
        pl.BlockSpec((None, D_MODEL, D_FF), of_layer, pipeline_mode=single),
        pl.BlockSpec((None, D_FF, D_MODEL), of_layer, pipeline_mode=single),
    ]
    args = [x, mix, wo, g2, wu, wd]
    if final:
        in_specs.append(pl.BlockSpec((1, D_MODEL), fixed))
        args.append(fg)
    return pl.pallas_call(
        functools.partial(_ffn_kernel, final=final),
        grid=(n // tm,),
        in_specs=in_specs,
        out_specs=pl.BlockSpec((tm, D_MODEL), row),
        out_shape=jax.ShapeDtypeStruct((n, D_MODEL), F32),
        compiler_params=pltpu.CompilerParams(
            dimension_semantics=("arbitrary",), vmem_limit_bytes=VMEM_LIMIT),
        name="ffn_final" if final else "ffn",
    )(*args)


def _rotary_tables(pos, rows):
    inv = ROPE_THETA ** (-(jnp.arange(ROPE_HALF, dtype=F32) * 2.0 / ROPE_DIM))
    ang = pos[:, None] * inv[None, :]
    cos, sin = jnp.cos(ang), jnp.sin(ang)
    n = pos.shape[0]
    pad = jnp.zeros((n, A_HEAD_DIM - ROPE_DIM), F32)
    zeros = jnp.zeros_like(sin)
    cos_h = jnp.concatenate([cos, cos, pad + 1.0], axis=1)
    sa_h = jnp.concatenate([-sin, zeros, pad], axis=1)
    sb_h = jnp.concatenate([zeros, sin, pad], axis=1)
    rep = LANES // A_HEAD_DIM
    tile = lambda a: jnp.tile(a, (rows // n, rep))
    return tile(cos_h), tile(sa_h), tile(sb_h)


def _pad_w_in(w):
    gate_pad = jnp.zeros(w.shape[:2] + (LANES - M_HEADS,), w.dtype)
    return jnp.concatenate(
        [w[..., :COL_G], w[..., COL_G:COL_G + M_HEADS], gate_pad, w[..., COL_G + M_HEADS:], gate_pad],
        axis=-1).astype(BF16)


def _pad_gate_bias(gb):
    pad = jnp.zeros((gb.shape[0], LANES - M_HEADS), gb.dtype)
    return jnp.concatenate([gb[:, :M_HEADS], pad, gb[:, M_HEADS:], pad], axis=1)[:, None, :]


def _m_rows(m):
    m_row = jnp.pad(m, [(0, 0)] * (m.ndim - 1) + [(0, LANES - M_HEADS)])
    return jnp.broadcast_to(m_row[..., None, :], m.shape[:-1] + (SUBLANES, LANES))


def kernel(x_prompt, x_sample, cache_k, cache_v, state_C, state_n, state_m, norm1_g, w_in, gate_b,
           attn_sink, mnorm_g, w_out, norm2_g, w_up, w_down, final_g):
    bp, T, _ = x_prompt.shape
    bs, S, _ = x_sample.shape
    depth = w_in.shape[0]
    assert T % PROMPT_MLSTM_CHUNK == 0 and S == CHUNK
    assert T % TOKEN_TILE == 0 and TOKEN_TILE % S == 0 and (bs * S) % TOKEN_TILE == 0

    tabs_p = _rotary_tables(jnp.arange(T, dtype=F32), max(T, TOKEN_TILE))
    tabs_s = _rotary_tables(jnp.arange(S, dtype=F32) + float(PAST_LEN), max(S, TOKEN_TILE))

    zeros_halo = jnp.zeros((1, bp, WINDOW, A_KV), F32)
    zeros_c = jnp.zeros((1, bp, M_HEADS, M_HEAD_DIM, M_HEAD_DIM), F32)
    zeros_n = jnp.zeros((1, bp, M_HEADS, M_HEAD_DIM), F32)
    zeros_m = jnp.zeros((1, bp, SUBLANES, LANES), F32)

    g1 = norm1_g[:, None, :]
    g2 = norm2_g[:, None, :]
    mg = mnorm_g[:, None, :]
    w1 = _pad_w_in(w_in)
    gb = _pad_gate_bias(gate_b)
    wo = w_out.astype(BF16)
    wu = w_up.astype(BF16)
    wd = w_down.astype(BF16)
    halo_k = cache_k.reshape(depth, bs, WINDOW, A_KV)
    halo_v = cache_v.reshape(depth, bs, WINDOW, A_KV)
    m0 = _m_rows(state_m)

    yp = x_prompt.reshape(bp * T, D_MODEL)
    ys = x_sample.reshape(bs * S, D_MODEL)
    outs = {name: [] for name in ("pk", "pv", "pC", "pn", "pm", "sk", "sv", "sC", "sn", "sm")}
    fg = final_g.reshape(1, D_MODEL)
    for l in range(depth):
        last = fg if l == depth - 1 else None

        q, k, v, m4, gt = _inproj(yp, g1, w1, tabs_p, gb, TOKEN_TILE, l)
        mix, C, n, m_rows = _mixer(q, k, v, m4, gt, zeros_halo, zeros_halo, zeros_c, zeros_n,
                                   zeros_m, attn_sink, mg, batch=bp, group=SEQ_GROUP,
                                   L=PROMPT_MLSTM_CHUNK, halo_valid=False, layer=l, state_layer=0)
        yp = _ffn(yp, mix, wo, g2, wu, wd, last, TOKEN_TILE, l)
        m = m_rows[:, 0, :M_HEADS]
        outs["pk"].append(k.reshape(bp, T, A_KV)[:, T - WINDOW:])
        outs["pv"].append(v.reshape(bp, T, A_KV)[:, T - WINDOW:])
        outs["pC"].append(C)
        outs["pn"].append(n)
        outs["pm"].append(m)

        q, k, v, m4, gt = _inproj(ys, g1, w1, tabs_s, gb, TOKEN_TILE, l)
        mix, C, n, m_rows = _mixer(q, k, v, m4, gt, halo_k, halo_v, state_C, state_n, m0,
                                   attn_sink, mg, batch=bs, group=SEQ_GROUP, L=S, halo_valid=True,
                                   layer=l, state_layer=l)
        ys = _ffn(ys, mix, wo, g2, wu, wd, last, TOKEN_TILE, l)
        m = m_rows[:, 0, :M_HEADS]
        outs["sk"].append(k.reshape(bs, S, A_KV))
        outs["sv"].append(v.reshape(bs, S, A_KV))
        outs["sC"].append(C)
        outs["sn"].append(n)
        outs["sm"].append(m)

    st = lambda name: jnp.stack(outs[name])
    kv = lambda name: (lambda a: a.reshape(a.shape[:-1] + (A_KV_HEADS, A_HEAD_DIM)))(st(name))
    return (yp.reshape(bp, T, D_MODEL), ys.reshape(bs, S, D_MODEL),
            kv("pk"), kv("pv"), st("pC"), st("pn"), st("pm"),
            kv("sk"), kv("sv"), st("sC"), st("sn"), st("sm"))
```

```python
import functools

import jax
import jax.numpy as jnp
from jax import lax
from jax.experimental import pallas as pl
from jax.experimental.pallas import tpu as pltpu

D_MODEL = 1024
CHUNK = 64
A_HEADS = 8
A_KV_HEADS = 2
A_HEAD_DIM = 64
WINDOW = 128
ROPE_THETA = 500000.0
ROPE_DIM = A_HEAD_DIM // 4
ROPE_HALF = ROPE_DIM // 2
M_HEADS = 4
M_HEAD_DIM = 128
D_FF = 4 * D_MODEL
NORM_EPS = 1e-6
NEG = -1e30
LOG2E = 1.4426950408889634
PAST_LEN = 4096
A_Q = A_HEADS * A_HEAD_DIM
A_KV = A_KV_HEADS * A_HEAD_DIM
M_W = M_HEADS * M_HEAD_DIM

LANES = 128
SUBLANES = 8
Q_PAD = A_HEADS * LANES
GATE_COLS = 2 * LANES
COL_KV = A_Q
COL_M = A_Q + 2 * A_KV
COL_G = COL_M + 4 * M_W
IN_COLS_PAD = COL_G + GATE_COLS
Q_HEAD_ORDER = (0, 2, 5, 7, 1, 3, 4, 6)

TOKEN_TILE = 512
PROMPT_MLSTM_CHUNK = 256
SEQ_GROUP = 2
FF_CHUNK = 1024
VMEM_LIMIT = 56 * 1024 * 1024

BF16 = jnp.bfloat16
F32 = jnp.float32


def _dot(a, b):
    return jnp.dot(a, b, preferred_element_type=F32)


def _dot_nt(a, b):
    return lax.dot_general(a, b, (((1,), (1,)), ((), ())), preferred_element_type=F32)


def _rms(x, g):
    r = lax.rsqrt(jnp.mean(x * x, axis=-1, keepdims=True) + NORM_EPS)
    return x * r * g


def _projection_parts(x_ref, g_ref, w_ref, cos_ref, sa_ref, sb_ref, gb_ref,
                      q_ref, k_ref, v_ref, m4_ref, gt_ref):
    h = _rms(x_ref[...], g_ref[...]).astype(BF16)
    low = lax.broadcasted_iota(jnp.int32, (1, LANES), 1) < A_HEAD_DIM

    def rotary(z):
        left = pltpu.roll(z, LANES - ROPE_HALF, axis=1)
        right = pltpu.roll(z, ROPE_HALF, axis=1)
        return z * cos_ref[...] + left * sa_ref[...] + right * sb_ref[...]

    def attention_part():
        zq_all = _dot(h, w_ref[:, :A_Q])
        zkv = _dot(h, w_ref[:, COL_KV:COL_KV + 2 * A_KV])
        for pair in range(A_Q // LANES):
            zq = rotary(zq_all[:, pair * LANES:(pair + 1) * LANES]) * (A_HEAD_DIM ** -0.5 * LOG2E)
            zsw = pltpu.roll(zq, A_HEAD_DIM, axis=1)
            zero = jnp.zeros_like(zq)
            if pair < A_HEADS // (2 * A_KV_HEADS):
                even, odd = jnp.where(low, zq, zero), jnp.where(low, zsw, zero)
            else:
                even, odd = jnp.where(low, zero, zsw), jnp.where(low, zero, zq)
            for head, val in ((2 * pair, even), (2 * pair + 1, odd)):
                j = Q_HEAD_ORDER.index(head)
                q_ref[:, j * LANES:(j + 1) * LANES] = val.astype(BF16)
        k_ref[...] = rotary(zkv[:, :A_KV])
        v_ref[...] = zkv[:, A_KV:]

    def mlstm_qk_part():
        for j in range(2):
            zm = _dot(h, w_ref[:, COL_M + j * M_W:COL_M + (j + 1) * M_W])
            if j == 1:
                zm = zm * (M_HEAD_DIM ** -0.5)
            m4_ref[:, j * M_W:(j + 1) * M_W] = zm.astype(BF16)

    def mlstm_vo_part():
        for j in range(2, 4):
            zm = _dot(h, w_ref[:, COL_M + j * M_W:COL_M + (j + 1) * M_W])
            m4_ref[:, j * M_W:(j + 1) * M_W] = zm.astype(BF16)
        zg = _dot(h, w_ref[:, COL_G:COL_G + GATE_COLS]) + gb_ref[...]
        zf = zg[:, LANES:]
        gt_ref[:, :LANES] = zg[:, :LANES]
        gt_ref[:, LANES:] = jnp.minimum(zf, 0.0) - jnp.log1p(jnp.exp(-jnp.abs(zf)))

    return attention_part, mlstm_qk_part, mlstm_vo_part


def _inproj_kernel(*refs):
    for part in _projection_parts(*refs):
        part()


def _inproj(x, g1, w, tabs, gb, tm, layer):
    n = x.shape[0]
    cos, sa, sb = tabs
    nt = cos.shape[0] // tm
    row = lambda i: (i, 0)
    of_layer = lambda i: (layer, 0, 0)
    tab = lambda i: (i % nt, 0)
    return pl.pallas_call(
        _inproj_kernel,
        grid=(n // tm,),
        in_specs=[
            pl.BlockSpec((tm, D_MODEL), row),
            pl.BlockSpec((None, 1, D_MODEL), of_layer),
            pl.BlockSpec((None, D_MODEL, IN_COLS_PAD), of_layer),
            pl.BlockSpec((tm, LANES), tab),
            pl.BlockSpec((tm, LANES), tab),
            pl.BlockSpec((tm, LANES), tab),
            pl.BlockSpec((None, 1, GATE_COLS), of_layer),
        ],
        out_specs=[
            pl.BlockSpec((tm, Q_PAD), row),
            pl.BlockSpec((tm, A_KV), row),
            pl.BlockSpec((tm, A_KV), row),
            pl.BlockSpec((tm, 4 * M_W), row),
            pl.BlockSpec((tm, GATE_COLS), row),
        ],
        out_shape=[
            jax.ShapeDtypeStruct((n, Q_PAD), BF16),
            jax.ShapeDtypeStruct((n, A_KV), F32),
            jax.ShapeDtypeStruct((n, A_KV), F32),
            jax.ShapeDtypeStruct((n, 4 * M_W), BF16),
            jax.ShapeDtypeStruct((n, GATE_COLS), F32),
        ],
        compiler_params=pltpu.CompilerParams(
            dimension_semantics=("arbitrary",), vmem_limit_bytes=VMEM_LIMIT),
        name="inproj",
    )(x, g1, w, cos, sa, sb, gb)


def _split3(x):
    hi = x.astype(BF16)
    r1 = x - hi.astype(F32)
    mid = r1.astype(BF16)
    lo = (r1 - mid.astype(F32)).astype(BF16)
    return hi, mid, lo


def _mixer_stages(t, sink_ref, q_ref, k_ref, v_ref, m4_ref, gt_ref, init_state, mg_ref,
                  mix_ref, c_ref, n_ref, m_ref, kbuf, vt, vs, cst,
                  *, L, steps, halo_valid, layer, write_ok=None, m_out_ref=None):
    n_chunks = L // CHUNK
    n_keys = WINDOW + CHUNK
    half = A_HEAD_DIM

    def put_vt(cols, v):
        v_t = v.T
        vt[:, cols] = v_t.astype(BF16)
        vs[:, cols] = jnp.concatenate([v_t[half:], v_t[:half]], axis=0).astype(BF16)

    @pl.when(t == 0)
    def _():
        vt[...] = jnp.zeros_like(vt)
        vs[...] = jnp.zeros_like(vs)
        init_state(put_vt)

    yield

    kbuf[WINDOW:WINDOW + L, :] = k_ref[...].astype(BF16)
    put_vt(pl.ds(WINDOW, L), v_ref[...])

    heads = range(M_HEADS)
    m_slice = lambda j, h: m4_ref[:, j * M_W + h * M_HEAD_DIM:j * M_W + (h + 1) * M_HEAD_DIM]

    scores = []
    for ci in range(n_chunks):
        r0 = ci * CHUNK
        q8 = jnp.concatenate([q_ref[r0:r0 + CHUNK, j * LANES:(j + 1) * LANES]
                              for j in range(A_HEADS)], axis=0)
        scores.append(_dot_nt(kbuf[r0:r0 + n_keys, :], q8))

    gates_i = gt_ref[:, :LANES]
    gates_f = gt_ref[:, LANES:]
    ri = lax.broadcasted_iota(jnp.int32, (L, L), 0)
    ci_ = lax.broadcasted_iota(jnp.int32, (L, L), 1)
    causal = ci_ <= ri
    b3 = _dot(causal.astype(BF16), jnp.concatenate(_split3(gates_f), axis=1))
    b = b3[:, :LANES] + b3[:, LANES:2 * LANES] + b3[:, 2 * LANES:]
    c_old = [cst[h] for h in heads]
    qk = [_dot_nt(m_slice(0, h), m_slice(1, h)) for h in heads]
    qc = [_dot_nt(m_slice(0, h), c_old[h].astype(BF16)) for h in heads]
    yield

    lane_q = lax.broadcasted_iota(jnp.int32, (1, A_HEADS * CHUNK), 1) // CHUNK
    sink_row = jnp.zeros((1, A_HEADS * CHUNK), F32)
    for j, head in enumerate(Q_HEAD_ORDER):
        sink_row = jnp.where(lane_q == j, sink_ref[layer, head] * LOG2E, sink_row)
    key_row = lax.broadcasted_iota(jnp.int32, (n_keys, A_HEADS * CHUNK), 0)
    probs, rdens = [], []
    for ci in range(n_chunks):
        s = scores[ci]
        if not halo_valid and ci < WINDOW // CHUNK:
            first_key = (t * n_chunks + ci - WINDOW // CHUNK) * CHUNK
            s = jnp.where(key_row + first_key >= 0, s, NEG)
        mx = jnp.maximum(jnp.max(s, axis=0, keepdims=True), sink_row)
        p = jnp.exp2(s - mx)
        rdens.append(1.0 / (jnp.sum(p, axis=0, keepdims=True) + jnp.exp2(sink_row - mx)))
        pad = jnp.zeros((CHUNK, A_HEADS * CHUNK), BF16)
        pb = p.astype(BF16)
        probs.append(jnp.concatenate([pb, pad] if ci % 2 == 0 else [pad, pb], axis=0))

    a = gates_i - b
    row_l = lax.broadcasted_iota(jnp.int32, (L, LANES), 0)
    cm = a
    shift = 1
    while shift < L:
        cm = jnp.maximum(cm, jnp.where(row_l >= shift, pltpu.roll(cm, shift, axis=0), NEG))
        shift *= 2
    m_prev = m_ref[0, 0:1, :]
    c = jnp.maximum(cm, m_prev)
    mt = c + b
    w_inter = jnp.exp(m_prev - c)
    e_neg_mt = jnp.exp(-mt)
    m_last = mt[L - 1:L, :]
    b_last = b[L - 1:L, :]
    delta = b_last - m_last
    decay = jnp.exp(b_last + m_prev - m_last)
    a_t = (a * LOG2E).T
    c2 = c * LOG2E
    delta2 = delta * LOG2E
    blk = min(L, LANES)
    n_blk = L // blk
    tri = (lax.broadcasted_iota(jnp.int32, (blk, blk), 1)
           <= lax.broadcasted_iota(jnp.int32, (blk, blk), 0))
    ones = jnp.ones((L, M_HEAD_DIM), BF16)
    out_gate = [jax.nn.sigmoid(m_slice(3, h).astype(F32)) for h in heads]
    s_bf, upd_lhs = [], []
    for h in heads:
        a_row = a_t[h:h + 1, :]
        row_blocks = []
        for rb in range(n_blk):
            rows = slice(rb * blk, (rb + 1) * blk)
            c_col = c2[rows, h:h + 1]
            e = []
            for cb in range(rb + 1):
                d = a_row[:, cb * blk:(cb + 1) * blk] - c_col
                e.append(jnp.exp2(jnp.where(tri, d, NEG) if cb == rb else d))
            s = (qk[h][rows, :(rb + 1) * blk] * jnp.concatenate(e, axis=1)).astype(BF16)
            if rb + 1 < n_blk:
                s = jnp.concatenate([s, jnp.zeros((blk, L - (rb + 1) * blk), BF16)], axis=1)
            row_blocks.append(s)
        s_bf.append(jnp.concatenate(row_blocks, axis=0))
        ws_row = jnp.exp2(a_row + delta2[:, h:h + 1])
        v_t = m_slice(2, h).astype(F32).T
        upd_lhs.append(jnp.concatenate(
            [v_t * ws_row, jnp.broadcast_to(ws_row, (M_HEAD_DIM, L))], axis=0).astype(BF16))

    yield

    pv = []
    for ci in range(n_chunks):
        c0 = ci * CHUNK if ci % 2 == 0 else (ci - 1) * CHUNK
        win = slice(c0, c0 + 2 * LANES)
        pv.append((_dot(vt[:, win], probs[ci][:, :2 * LANES]),
                   _dot(vs[:, win], probs[ci][:, 2 * LANES:])))
    sv = [_dot(s_bf[h], jnp.concatenate([m_slice(2, h), ones], axis=1)) for h in heads]
    c_upd = [_dot(upd_lhs[h], m_slice(1, h)) for h in heads]
    yield

    for ci in range(n_chunks):
        r0 = ci * CHUNK
        ra = pv[ci][0] * rdens[ci][:, :2 * LANES]
        rb = pv[ci][1] * rdens[ci][:, 2 * LANES:]
        o0 = jnp.concatenate([ra[:half, :LANES], rb[half:, :LANES]], axis=0).T.astype(BF16)
        o1 = jnp.concatenate([rb[:half, LANES:], ra[half:, LANES:]], axis=0).T.astype(BF16)
        mix_ref[r0:r0 + CHUNK, 0 * LANES:1 * LANES] = o0[:CHUNK]
        mix_ref[r0:r0 + CHUNK, 1 * LANES:2 * LANES] = o0[CHUNK:]
        mix_ref[r0:r0 + CHUNK, 2 * LANES:3 * LANES] = o1[:CHUNK]
        mix_ref[r0:r0 + CHUNK, 3 * LANES:4 * LANES] = o1[CHUNK:]
    for h in heads:
        tot = sv[h] + w_inter[:, h:h + 1] * qc[h]
        hh = tot[:, :M_HEAD_DIM] / jnp.maximum(jnp.abs(tot[:, M_HEAD_DIM:]), e_neg_mt[:, h:h + 1])
        hn = _rms(hh, mg_ref[:, h * M_HEAD_DIM:(h + 1) * M_HEAD_DIM])
        out = hn * out_gate[h]
        mix_ref[:, A_Q + h * M_HEAD_DIM:A_Q + (h + 1) * M_HEAD_DIM] = out.astype(BF16)
        cst[h] = decay[:, h:h + 1] * c_old[h] + c_upd[h]
    m_ref[0] = jnp.broadcast_to(m_last, (SUBLANES, LANES))
    yield

    def write_state():
        for h in heads:
            c_ref[0, h] = cst[h, :M_HEAD_DIM, :]
            n_ref[0, h:h + 1, :] = cst[h, M_HEAD_DIM:M_HEAD_DIM + 1, :]
        if m_out_ref is not None:
            m_out_ref[...] = m_ref[...]

    if steps == 1 and write_ok is None:
        write_state()
    else:
        last = t == steps - 1
        pl.when(last if write_ok is None else jnp.logical_and(last, write_ok))(write_state)

    if steps > 1:
        kbuf[0:WINDOW, :] = kbuf[L:L + WINDOW, :]
        vt[:, 0:WINDOW] = vt[:, L:L + WINDOW]
        vs[:, 0:WINDOW] = vs[:, L:L + WINDOW]


def _interleave(gens):
    done = object()
    live = list(gens)
    while live:
        live = [g for g in live if next(g, done) is not done]


def _mixer_kernel(sink_ref, q_ref, k_ref, v_ref, m4_ref, gt_ref, hk_ref, hv_ref, c0_ref, n0_ref,
                  m0_ref, mg_ref, mix_ref, c_ref, n_ref, m_ref, kbuf, vt, vs, cst,
                  *, group, **static):
    def stages(i):
        one = lambda ref: ref.at[pl.ds(i, 1)]

        def init_state(put_vt):
            kbuf[i, 0:WINDOW, :] = hk_ref[i].astype(BF16)
            put_vt(pl.ds(0, WINDOW), hv_ref[i])
            for h in range(M_HEADS):
                cst[i, h, :M_HEAD_DIM, :] = c0_ref[i, h]
                cst[i, h, M_HEAD_DIM:, :] = jnp.broadcast_to(
                    n0_ref[i, h:h + 1, :], (M_HEAD_DIM, M_HEAD_DIM))
            m_ref[i] = m0_ref[i]

        return _mixer_stages(
            pl.program_id(1), sink_ref, q_ref.at[i], k_ref.at[i], v_ref.at[i], m4_ref.at[i],
            gt_ref.at[i], init_state, mg_ref, mix_ref.at[i], one(c_ref), one(n_ref), one(m_ref),
            kbuf.at[i], vt.at[i], vs.at[i], cst.at[i], **static)

    _interleave([stages(i) for i in range(group)])


def _mixer(q, k, v, m4, gt, halo_k, halo_v, c0, n0, m0, sink, mg,
           *, batch, group, L, halo_valid, layer, state_layer):
    n = q.shape[0]
    T = n // batch
    steps = T // L
    assert steps == 1 or L % LANES == 0
    assert batch % group == 0
    tile = lambda b, t: (b, t, 0)
    per_b3 = lambda b, t: (b, 0, 0)
    per_b4 = lambda b, t: (b, 0, 0, 0)
    init3 = lambda b, t: (state_layer, b, 0, 0)
    init4 = lambda b, t: (state_layer, b, 0, 0, 0)
    kv_cols = max(WINDOW + L, 2 * LANES)
    by_seq = lambda a: a.reshape(batch, T, a.shape[-1])
    mix, C, n_out, m_out = pl.pallas_call(
        functools.partial(_mixer_kernel, group=group, L=L, steps=steps, halo_valid=halo_valid,
                          layer=layer),
        grid=(batch // group, steps),
        in_specs=[
            pl.BlockSpec(memory_space=pltpu.SMEM),
            pl.BlockSpec((group, L, Q_PAD), tile),
            pl.BlockSpec((group, L, A_KV), tile),
            pl.BlockSpec((group, L, A_KV), tile),
            pl.BlockSpec((group, L, 4 * M_W), tile),
            pl.BlockSpec((group, L, GATE_COLS), tile),
            pl.BlockSpec((None, group, WINDOW, A_KV), init3),
            pl.BlockSpec((None, group, WINDOW, A_KV), init3),
            pl.BlockSpec((None, group, M_HEADS, M_HEAD_DIM, M_HEAD_DIM), init4),
            pl.BlockSpec((None, group, M_HEADS, M_HEAD_DIM), init3),
            pl.BlockSpec((None, group, SUBLANES, LANES), init3),
            pl.BlockSpec((None, 1, M_W), lambda b, t: (layer, 0, 0)),
        ],
        out_specs=[
            pl.BlockSpec((group, L, D_MODEL), tile),
            pl.BlockSpec((group, M_HEADS, M_HEAD_DIM, M_HEAD_DIM), per_b4),
            pl.BlockSpec((group, M_HEADS, M_HEAD_DIM), per_b3),
            pl.BlockSpec((group, SUBLANES, LANES), per_b3),
        ],
        out_shape=[
            jax.ShapeDtypeStruct((batch, T, D_MODEL), BF16),
            jax.ShapeDtypeStruct((batch, M_HEADS, M_HEAD_DIM, M_HEAD_DIM), F32),
            jax.ShapeDtypeStruct((batch, M_HEADS, M_HEAD_DIM), F32),
            jax.ShapeDtypeStruct((batch, SUBLANES, LANES), F32),
        ],
        scratch_shapes=[
            pltpu.VMEM((group, WINDOW + L, LANES), BF16),
            pltpu.VMEM((group, LANES, kv_cols), BF16),
            pltpu.VMEM((group, LANES, kv_cols), BF16),
            pltpu.VMEM((group, M_HEADS, 2 * M_HEAD_DIM, M_HEAD_DIM), F32),
        ],
        compiler_params=pltpu.CompilerParams(
            dimension_semantics=("arbitrary", "arbitrary"), vmem_limit_bytes=VMEM_LIMIT),
        name="mixer",
    )(sink, by_seq(q), by_seq(k), by_seq(v), by_seq(m4), by_seq(gt), halo_k, halo_v, c0, n0, m0, mg)
    return mix.reshape(n, D_MODEL), C, n_out, m_out


def _ffn_stages(x, mix, wo_ref, g2_ref, wu_ref, wd_ref, fg_ref, store):
    x1 = x + _dot(mix, wo_ref[...])
    xn = _rms(x1, g2_ref[...]).astype(BF16)
    acc = x1
    yield
    for c in range(D_FF // FF_CHUNK):
        u = _dot(xn, wu_ref[:, c * FF_CHUNK:(c + 1) * FF_CHUNK])
        a = jnp.square(jnp.maximum(u, 0.0)).astype(BF16)
        acc = acc + _dot(a, wd_ref[c * FF_CHUNK:(c + 1) * FF_CHUNK, :])
        if c + 1 < D_FF // FF_CHUNK:
            yield
    if fg_ref is not None:
        acc = _rms(acc, fg_ref[...])
    store(acc)


def _ffn_kernel(*refs, final):
    if final:
        x_ref, mix_ref, wo_ref, g2_ref, wu_ref, wd_ref, fg_ref, out_ref = refs
    else:
        x_ref, mix_ref, wo_ref, g2_ref, wu_ref, wd_ref, out_ref = refs
        fg_ref = None

    def store(acc):
        out_ref[...] = acc

    _interleave([_ffn_stages(x_ref[...], mix_ref[...], wo_ref, g2_ref, wu_ref, wd_ref, fg_ref, store)])


def _back_kernel(*refs, group, L, steps, n_steps, layer, final):
    (sink_ref, q_ref, k_ref, v_ref, m4_ref, gt_ref, mg_ref, x_ref, wo_ref, g2_ref, wu_ref,
     wd_ref) = refs[:12]
    fg_ref = refs[12] if final else None
    y_ref, c_ref, n_ref, m_ref, kbuf, vt, vs, cst, mst, mix_z = refs[12 + final:]
    s = pl.program_id(0)
    wslot = s % 2
    rslot = 1 - wslot
    t = jnp.minimum(s, n_steps - 1) % steps
    write_ok = s < n_steps

    @pl.when(s == 0)
    def _():
        mix_z[1] = jnp.zeros(mix_z.shape[1:], mix_z.dtype)

    def mixer(i):
        one = lambda ref: ref.at[pl.ds(i, 1)]

        def init_state(put_vt):
            kbuf[i, 0:WINDOW, :] = jnp.zeros((WINDOW, LANES), BF16)
            cst[i] = jnp.zeros(cst.shape[1:], cst.dtype)
            mst[i] = jnp.zeros(mst.shape[1:], mst.dtype)

        return _mixer_stages(
            t, sink_ref, q_ref.at[i], k_ref.at[i], v_ref.at[i], m4_ref.at[i], gt_ref.at[i],
            init_state, mg_ref, mix_z.at[wslot, i], one(c_ref), one(n_ref), one(mst),
            kbuf.at[i], vt.at[i], vs.at[i], cst.at[i],
            L=L, steps=steps, halo_valid=False, layer=layer, write_ok=write_ok,
            m_out_ref=one(m_ref))

    def store(acc):
        y_ref[...] = acc.reshape(group, L, D_MODEL)

    rows = group * L
    ffn = _ffn_stages(x_ref[...].reshape(rows, D_MODEL), mix_z[rslot].reshape(rows, D_MODEL),
                      wo_ref, g2_ref, wu_ref, wd_ref, fg_ref, store)
    mixers = [mixer(i) for i in range(group)]
    every = lambda: [next(m) for m in mixers]
    every()
    next(ffn)
    every()
    next(ffn)
    every()
    every()
    next(ffn)
    every()
    _interleave([ffn])
    _interleave(mixers)


def _ffn(x, mix, wo, g2, wu, wd, fg, tm, layer):
    n = x.shape[0]
    final = fg is not None
    row = lambda i: (i, 0)
    fixed = lambda i: (0, 0)
    of_layer = lambda i: (layer, 0, 0)
    single = pl.Buffered(1)
    in_specs = [
        pl.BlockSpec((tm, D_MODEL), row),
        pl.BlockSpec((tm, D_MODEL), row),
        pl.BlockSpec((None, D_MODEL, D_MODEL), of_layer, pipeline_mode=single),
        pl.BlockSpec((None, 1, D_MODEL), of_layer),
        pl.BlockSpec((None, D_MODEL, D_FF), of_layer, pipeline_mode=single),
        pl.BlockSpec((None, D_FF, D_MODEL), of_layer, pipeline_mode=single),
    ]
    args = [x, mix, wo, g2, wu, wd]
    if final:
        in_specs.append(pl.BlockSpec((1, D_MODEL), fixed))
        args.append(fg)
    return pl.pallas_call(
        functools.partial(_ffn_kernel, final=final),
        grid=(n // tm,),
        in_specs=in_specs,
        out_specs=pl.BlockSpec((tm, D_MODEL), row),
        out_shape=jax.ShapeDtypeStruct((n, D_MODEL), F32),
        compiler_params=pltpu.CompilerParams(
            dimension_semantics=("arbitrary",), vmem_limit_bytes=VMEM_LIMIT),
        name="ffn_final" if final else "ffn",
    )(*args)


def _back(x, q, k, v, m4, gt, sink, mg, wo, g2, wu, wd, fg, *, batch, group, L, layer):
    n = x.shape[0]
    T = n // batch
    steps = T // L
    n_steps = (batch // group) * steps
    assert steps > 1 and L % LANES == 0 and batch % group == 0
    final = fg is not None
    mixed = lambda s: jnp.minimum(s, n_steps - 1)
    fed = lambda s: jnp.maximum(s - 1, 0)
    mix_tile = lambda s: (mixed(s) // steps, mixed(s) % steps, 0)
    ffn_tile = lambda s: (fed(s) // steps, fed(s) % steps, 0)
    seq3 = lambda s: (mixed(s) // steps, 0, 0)
    of_layer = lambda s: (layer, 0, 0)
    single = pl.Buffered(1)
    by_seq = lambda a: a.reshape(batch, T, a.shape[-1])
    in_specs = [
        pl.BlockSpec(memory_space=pltpu.SMEM),
        pl.BlockSpec((group, L, Q_PAD), mix_tile),
        pl.BlockSpec((group, L, A_KV), mix_tile),
        pl.BlockSpec((group, L, A_KV), mix_tile),
        pl.BlockSpec((group, L, 4 * M_W), mix_tile),
        pl.BlockSpec((group, L, GATE_COLS), mix_tile),
        pl.BlockSpec((None, 1, M_W), of_layer),
        pl.BlockSpec((group, L, D_MODEL), ffn_tile),
        pl.BlockSpec((None, D_MODEL, D_MODEL), of_layer, pipeline_mode=single),
        pl.BlockSpec((None, 1, D_MODEL), of_layer),
        pl.BlockSpec((None, D_MODEL, D_FF), of_layer, pipeline_mode=single),
        pl.BlockSpec((None, D_FF, D_MODEL), of_layer, pipeline_mode=single),
    ]
    args = [sink, by_seq(q), by_seq(k), by_seq(v), by_seq(m4), by_seq(gt), mg, by_seq(x),
            wo, g2, wu, wd]
    if final:
        in_specs.append(pl.BlockSpec((1, D_MODEL), lambda s: (0, 0)))
        args.append(fg)
    kv_cols = WINDOW + L
    y, C, n_out, m_out = pl.pallas_call(
        functools.partial(_back_kernel, group=group, L=L, steps=steps, n_steps=n_steps,
                          layer=layer, final=final),
        grid=(n_steps + 1,),
        in_specs=in_specs,
        out_specs=[
            pl.BlockSpec((group, L, D_MODEL), ffn_tile),
            pl.BlockSpec((group, M_HEADS, M_HEAD_DIM, M_HEAD_DIM), lambda s: (mixed(s) // steps, 0, 0, 0)),
            pl.BlockSpec((group, M_HEADS, M_HEAD_DIM), seq3),
            pl.BlockSpec((group, SUBLANES, LANES), seq3),
        ],
        out_shape=[
            jax.ShapeDtypeStruct((batch, T, D_MODEL), F32),
            jax.ShapeDtypeStruct((batch, M_HEADS, M_HEAD_DIM, M_HEAD_DIM), F32),
            jax.ShapeDtypeStruct((batch, M_HEADS, M_HEAD_DIM), F32),
            jax.ShapeDtypeStruct((batch, SUBLANES, LANES), F32),
        ],
        scratch_shapes=[
            pltpu.VMEM((group, WINDOW + L, LANES), BF16),
            pltpu.VMEM((group, LANES, kv_cols), BF16),
            pltpu.VMEM((group, LANES, kv_cols), BF16),
            pltpu.VMEM((group, M_HEADS, 2 * M_HEAD_DIM, M_HEAD_DIM), F32),
            pltpu.VMEM((group, SUBLANES, LANES), F32),
            pltpu.VMEM((2, group, L, D_MODEL), BF16),
        ],
        compiler_params=pltpu.CompilerParams(
            dimension_semantics=("arbitrary",), vmem_limit_bytes=VMEM_LIMIT),
        name="back_final" if final else "back",
    )(*args)
    return y.reshape(n, D_MODEL), C, n_out, m_out


def _rotary_tables(pos, rows):
    inv = ROPE_THETA ** (-(jnp.arange(ROPE_HALF, dtype=F32) * 2.0 / ROPE_DIM))
    ang = pos[:, None] * inv[None, :]
    cos, sin = jnp.cos(ang), jnp.sin(ang)
    n = pos.shape[0]
    pad = jnp.zeros((n, A_HEAD_DIM - ROPE_DIM), F32)
    zeros = jnp.zeros_like(sin)
    cos_h = jnp.concatenate([cos, cos, pad + 1.0], axis=1)
    sa_h = jnp.concatenate([-sin, zeros, pad], axis=1)
    sb_h = jnp.concatenate([zeros, sin, pad], axis=1)
    rep = LANES // A_HEAD_DIM
    tile = lambda a: jnp.tile(a, (rows // n, rep))
    return tile(cos_h), tile(sa_h), tile(sb_h)


def _pad_w_in(w):
    gate_pad = jnp.zeros(w.shape[:2] + (LANES - M_HEADS,), w.dtype)
    return jnp.concatenate(
        [w[..., :COL_G], w[..., COL_G:COL_G + M_HEADS], gate_pad, w[..., COL_G + M_HEADS:], gate_pad],
        axis=-1).astype(BF16)


def _pad_gate_bias(gb):
    pad = jnp.zeros((gb.shape[0], LANES - M_HEADS), gb.dtype)
    return jnp.concatenate([gb[:, :M_HEADS], pad, gb[:, M_HEADS:], pad], axis=1)[:, None, :]


def _m_rows(m):
    m_row = jnp.pad(m, [(0, 0)] * (m.ndim - 1) + [(0, LANES - M_HEADS)])
    return jnp.broadcast_to(m_row[..., None, :], m.shape[:-1] + (SUBLANES, LANES))


def kernel(x_prompt, x_sample, cache_k, cache_v, state_C, state_n, state_m, norm1_g, w_in, gate_b,
           attn_sink, mnorm_g, w_out, norm2_g, w_up, w_down, final_g):
    bp, T, _ = x_prompt.shape
    bs, S, _ = x_sample.shape
    depth = w_in.shape[0]
    assert T % PROMPT_MLSTM_CHUNK == 0 and S == CHUNK
    assert T % TOKEN_TILE == 0 and TOKEN_TILE % S == 0 and (bs * S) % TOKEN_TILE == 0

    tabs_p = _rotary_tables(jnp.arange(T, dtype=F32), max(T, TOKEN_TILE))
    tabs_s = _rotary_tables(jnp.arange(S, dtype=F32) + float(PAST_LEN), max(S, TOKEN_TILE))

    g1 = norm1_g[:, None, :]
    g2 = norm2_g[:, None, :]
    mg = mnorm_g[:, None, :]
    w1 = _pad_w_in(w_in)
    gb = _pad_gate_bias(gate_b)
    wo = w_out.astype(BF16)
    wu = w_up.astype(BF16)
    wd = w_down.astype(BF16)
    halo_k = cache_k.reshape(depth, bs, WINDOW, A_KV)
    halo_v = cache_v.reshape(depth, bs, WINDOW, A_KV)
    m0 = _m_rows(state_m)

    yp = x_prompt.reshape(bp * T, D_MODEL)
    ys = x_sample.reshape(bs * S, D_MODEL)
    outs = {name: [] for name in ("pk", "pv", "pC", "pn", "pm", "sk", "sv", "sC", "sn", "sm")}
    fg = final_g.reshape(1, D_MODEL)
    for l in range(depth):
        last = fg if l == depth - 1 else None

        q, k, v, m4, gt = _inproj(yp, g1, w1, tabs_p, gb, TOKEN_TILE, l)
        yp, C, n, m_rows = _back(yp, q, k, v, m4, gt, attn_sink, mg, wo, g2, wu, wd, last,
                                 batch=bp, group=SEQ_GROUP, L=PROMPT_MLSTM_CHUNK, layer=l)
        m = m_rows[:, 0, :M_HEADS]
        outs["pk"].append(k.reshape(bp, T, A_KV)[:, T - WINDOW:])
        outs["pv"].append(v.reshape(bp, T, A_KV)[:, T - WINDOW:])
        outs["pC"].append(C)
        outs["pn"].append(n)
        outs["pm"].append(m)

        q, k, v, m4, gt = _inproj(ys, g1, w1, tabs_s, gb, TOKEN_TILE, l)
        mix, C, n, m_rows = _mixer(q, k, v, m4, gt, halo_k, halo_v, state_C, state_n, m0,
                                   attn_sink, mg, batch=bs, group=SEQ_GROUP, L=S, halo_valid=True,
                                   layer=l, state_layer=l)
        ys = _ffn(ys, mix, wo, g2, wu, wd, last, TOKEN_TILE, l)
        m = m_rows[:, 0, :M_HEADS]
        outs["sk"].append(k.reshape(bs, S, A_KV))
        outs["sv"].append(v.reshape(bs, S, A_KV))
        outs["sC"].append(C)
        outs["sn"].append(n)
        outs["sm"].append(m)

    st = lambda name: jnp.stack(outs[name])
    kv = lambda name: (lambda a: a.reshape(a.shape[:-1] + (A_KV_HEADS, A_HEAD_DIM)))(st(name))
    return (yp.reshape(bp, T, D_MODEL), ys.reshape(bs, S, D_MODEL),
            kv("pk"), kv("pv"), st("pC"), st("pn"), st("pm"),
            kv("sk"), kv("sv"), st("sC"), st("sn"), st("sm"))
```

```python
import functools

import jax
import jax.numpy as jnp
from jax import lax
from jax.experimental import pallas as pl
from jax.experimental.pallas import tpu as pltpu

D_MODEL = 1024
CHUNK = 64
A_HEADS = 8
A_KV_HEADS = 2
A_HEAD_DIM = 64
WINDOW = 128
ROPE_THETA = 500000.0
ROPE_DIM = A_HEAD_DIM // 4
ROPE_HALF = ROPE_DIM // 2
M_HEADS = 4
M_HEAD_DIM = 128
D_FF = 4 * D_MODEL
NORM_EPS = 1e-6
NEG = -1e30
LOG2E = 1.4426950408889634
PAST_LEN = 4096
A_Q = A_HEADS * A_HEAD_DIM
A_KV = A_KV_HEADS * A_HEAD_DIM
M_W = M_HEADS * M_HEAD_DIM

LANES = 128
SUBLANES = 8
Q_PAD = A_HEADS * LANES
GATE_COLS = 2 * LANES
COL_KV = A_Q
COL_M = A_Q + 2 * A_KV
COL_G = COL_M + 4 * M_W
Q_HEAD_ORDER = (0, 2, 5, 7, 1, 3, 4, 6)

TOKEN_TILE = 512
PROMPT_MLSTM_CHUNK = 256
SEQ_GROUP = 2
FF_CHUNK = 1024
VMEM_LIMIT = 56 * 1024 * 1024

BF16 = jnp.bfloat16
F32 = jnp.float32


def _dot(a, b):
    return jnp.dot(a, b, preferred_element_type=F32)


def _dot_nt(a, b):
    return lax.dot_general(a, b, (((1,), (1,)), ((), ())), preferred_element_type=F32)


def _rms(x, g):
    r = lax.rsqrt(jnp.mean(x * x, axis=-1, keepdims=True) + NORM_EPS)
    return x * r * g


def _projection_parts(x_ref, g_ref, w_ref, wg_ref, cos_ref, sa_ref, sb_ref, gb_ref,
                      q_ref, k_ref, v_ref, m4_ref, gt_ref):
    h = _rms(x_ref[...], g_ref[...]).astype(BF16)
    low = lax.broadcasted_iota(jnp.int32, (1, LANES), 1) < A_HEAD_DIM

    def rotary(z):
        left = pltpu.roll(z, LANES - ROPE_HALF, axis=1)
        right = pltpu.roll(z, ROPE_HALF, axis=1)
        return z * cos_ref[...] + left * sa_ref[...] + right * sb_ref[...]

    def attention_part():
        zq_all = _dot(h, w_ref[:, :A_Q])
        zkv = _dot(h, w_ref[:, COL_KV:COL_KV + 2 * A_KV])
        for pair in range(A_Q // LANES):
            zq = rotary(zq_all[:, pair * LANES:(pair + 1) * LANES]) * (A_HEAD_DIM ** -0.5 * LOG2E)
            zsw = pltpu.roll(zq, A_HEAD_DIM, axis=1)
            zero = jnp.zeros_like(zq)
            if pair < A_HEADS // (2 * A_KV_HEADS):
                even, odd = jnp.where(low, zq, zero), jnp.where(low, zsw, zero)
            else:
                even, odd = jnp.where(low, zero, zsw), jnp.where(low, zero, zq)
            for head, val in ((2 * pair, even), (2 * pair + 1, odd)):
                j = Q_HEAD_ORDER.index(head)
                q_ref[:, j * LANES:(j + 1) * LANES] = val.astype(BF16)
        k_ref[...] = rotary(zkv[:, :A_KV])
        v_ref[...] = zkv[:, A_KV:]

    def mlstm_qk_part():
        for j in range(2):
            zm = _dot(h, w_ref[:, COL_M + j * M_W:COL_M + (j + 1) * M_W])
            if j == 1:
                zm = zm * (M_HEAD_DIM ** -0.5)
            m4_ref[:, j * M_W:(j + 1) * M_W] = zm.astype(BF16)

    def mlstm_vo_part():
        for j in range(2, 4):
            zm = _dot(h, w_ref[:, COL_M + j * M_W:COL_M + (j + 1) * M_W])
            m4_ref[:, j * M_W:(j + 1) * M_W] = zm.astype(BF16)
        zg = _dot(h, wg_ref[...]) + gb_ref[...]
        zf = zg[:, LANES:]
        gt_ref[:, :LANES] = zg[:, :LANES]
        gt_ref[:, LANES:] = jnp.minimum(zf, 0.0) - jnp.log1p(jnp.exp(-jnp.abs(zf)))

    return attention_part, mlstm_qk_part, mlstm_vo_part


def _inproj_kernel(*refs):
    for part in _projection_parts(*refs):
        part()


def _inproj(x, g1, w, wg, tabs, gb, tm, layer):
    n = x.shape[0]
    cos, sa, sb = tabs
    nt = cos.shape[0] // tm
    row = lambda i: (i, 0)
    of_layer = lambda i: (layer, 0, 0)
    tab = lambda i: (i % nt, 0)
    return pl.pallas_call(
        _inproj_kernel,
        grid=(n // tm,),
        in_specs=[
            pl.BlockSpec((tm, D_MODEL), row),
            pl.BlockSpec((None, 1, D_MODEL), of_layer),
            pl.BlockSpec((None, D_MODEL, COL_G), of_layer),
            pl.BlockSpec((None, D_MODEL, GATE_COLS), of_layer),
            pl.BlockSpec((tm, LANES), tab),
            pl.BlockSpec((tm, LANES), tab),
            pl.BlockSpec((tm, LANES), tab),
            pl.BlockSpec((None, 1, GATE_COLS), of_layer),
        ],
        out_specs=[
            pl.BlockSpec((tm, Q_PAD), row),
            pl.BlockSpec((tm, A_KV), row),
            pl.BlockSpec((tm, A_KV), row),
            pl.BlockSpec((tm, 4 * M_W), row),
            pl.BlockSpec((tm, GATE_COLS), row),
        ],
        out_shape=[
            jax.ShapeDtypeStruct((n, Q_PAD), BF16),
            jax.ShapeDtypeStruct((n, A_KV), F32),
            jax.ShapeDtypeStruct((n, A_KV), F32),
            jax.ShapeDtypeStruct((n, 4 * M_W), BF16),
            jax.ShapeDtypeStruct((n, GATE_COLS), F32),
        ],
        compiler_params=pltpu.CompilerParams(
            dimension_semantics=("arbitrary",), vmem_limit_bytes=VMEM_LIMIT),
        name="inproj",
    )(x, g1, w, wg, cos, sa, sb, gb)


def _split3(x):
    hi = x.astype(BF16)
    r1 = x - hi.astype(F32)
    mid = r1.astype(BF16)
    lo = (r1 - mid.astype(F32)).astype(BF16)
    return hi, mid, lo


def _mixer_stages(t, sink_ref, q_ref, k_ref, v_ref, m4_ref, gt_ref, init_state, mg_ref,
                  mix_ref, c_ref, n_ref, m_ref, kbuf, vt, vs, cst,
                  *, L, steps, halo_valid, layer, write_ok=None, m_out_ref=None):
    n_chunks = L // CHUNK
    n_keys = WINDOW + CHUNK
    half = A_HEAD_DIM

    def put_vt(cols, v):
        v_t = v.T
        vt[:, cols] = v_t.astype(BF16)
        vs[:, cols] = jnp.concatenate([v_t[half:], v_t[:half]], axis=0).astype(BF16)

    @pl.when(t == 0)
    def _():
        vt[...] = jnp.zeros_like(vt)
        vs[...] = jnp.zeros_like(vs)
        init_state(put_vt)

    yield

    kbuf[WINDOW:WINDOW + L, :] = k_ref[...].astype(BF16)
    put_vt(pl.ds(WINDOW, L), v_ref[...])

    heads = range(M_HEADS)
    m_slice = lambda j, h: m4_ref[:, j * M_W + h * M_HEAD_DIM:j * M_W + (h + 1) * M_HEAD_DIM]

    scores = []
    for ci in range(n_chunks):
        r0 = ci * CHUNK
        q8 = jnp.concatenate([q_ref[r0:r0 + CHUNK, j * LANES:(j + 1) * LANES]
                              for j in range(A_HEADS)], axis=0)
        scores.append(_dot_nt(kbuf[r0:r0 + n_keys, :], q8))

    gates_i = gt_ref[:, :LANES]
    gates_f = gt_ref[:, LANES:]
    ri = lax.broadcasted_iota(jnp.int32, (L, L), 0)
    ci_ = lax.broadcasted_iota(jnp.int32, (L, L), 1)
    causal = ci_ <= ri
    b3 = _dot(causal.astype(BF16), jnp.concatenate(_split3(gates_f), axis=1))
    b = b3[:, :LANES] + b3[:, LANES:2 * LANES] + b3[:, 2 * LANES:]
    c_old = [cst[h] for h in heads]
    qk = [_dot_nt(m_slice(0, h), m_slice(1, h)) for h in heads]
    qc = [_dot_nt(m_slice(0, h), c_old[h].astype(BF16)) for h in heads]
    yield

    lane_q = lax.broadcasted_iota(jnp.int32, (1, A_HEADS * CHUNK), 1) // CHUNK
    sink_row = jnp.zeros((1, A_HEADS * CHUNK), F32)
    for j, head in enumerate(Q_HEAD_ORDER):
        sink_row = jnp.where(lane_q == j, sink_ref[layer, head] * LOG2E, sink_row)
    key_row = lax.broadcasted_iota(jnp.int32, (n_keys, A_HEADS * CHUNK), 0)
    probs, rdens = [], []
    for ci in range(n_chunks):
        s = scores[ci]
        if not halo_valid and ci < WINDOW // CHUNK:
            first_key = (t * n_chunks + ci - WINDOW // CHUNK) * CHUNK
            s = jnp.where(key_row + first_key >= 0, s, NEG)
        mx = jnp.maximum(jnp.max(s, axis=0, keepdims=True), sink_row)
        p = jnp.exp2(s - mx)
        rdens.append(1.0 / (jnp.sum(p, axis=0, keepdims=True) + jnp.exp2(sink_row - mx)))
        pad = jnp.zeros((CHUNK, A_HEADS * CHUNK), BF16)
        pb = p.astype(BF16)
        probs.append(jnp.concatenate([pb, pad] if ci % 2 == 0 else [pad, pb], axis=0))

    a = gates_i - b
    row_l = lax.broadcasted_iota(jnp.int32, (L, LANES), 0)
    cm = a
    shift = 1
    while shift < L:
        cm = jnp.maximum(cm, jnp.where(row_l >= shift, pltpu.roll(cm, shift, axis=0), NEG))
        shift *= 2
    m_prev = m_ref[0, 0:1, :]
    c = jnp.maximum(cm, m_prev)
    mt = c + b
    w_inter = jnp.exp(m_prev - c)
    e_neg_mt = jnp.exp(-mt)
    m_last = mt[L - 1:L, :]
    b_last = b[L - 1:L, :]
    delta = b_last - m_last
    decay = jnp.exp(b_last + m_prev - m_last)
    a_t = (a * LOG2E).T
    c2 = c * LOG2E
    delta2 = delta * LOG2E
    blk = min(L, LANES)
    n_blk = L // blk
    tri = (lax.broadcasted_iota(jnp.int32, (blk, blk), 1)
           <= lax.broadcasted_iota(jnp.int32, (blk, blk), 0))
    ones = jnp.ones((L, M_HEAD_DIM), BF16)
    out_gate = [jax.nn.sigmoid(m_slice(3, h).astype(F32)) for h in heads]
    s_bf, upd_lhs = [], []
    for h in heads:
        a_row = a_t[h:h + 1, :]
        row_blocks = []
        for rb in range(n_blk):
            rows = slice(rb * blk, (rb + 1) * blk)
            c_col = c2[rows, h:h + 1]
            e = []
            for cb in range(rb + 1):
                d = a_row[:, cb * blk:(cb + 1) * blk] - c_col
                e.append(jnp.exp2(jnp.where(tri, d, NEG) if cb == rb else d))
            s = (qk[h][rows, :(rb + 1) * blk] * jnp.concatenate(e, axis=1)).astype(BF16)
            if rb + 1 < n_blk:
                s = jnp.concatenate([s, jnp.zeros((blk, L - (rb + 1) * blk), BF16)], axis=1)
            row_blocks.append(s)
        s_bf.append(jnp.concatenate(row_blocks, axis=0))
        ws_row = jnp.exp2(a_row + delta2[:, h:h + 1])
        v_t = m_slice(2, h).astype(F32).T
        upd_lhs.append(jnp.concatenate(
            [v_t * ws_row, jnp.broadcast_to(ws_row, (M_HEAD_DIM, L))], axis=0).astype(BF16))

    yield

    pv = []
    for ci in range(n_chunks):
        c0 = ci * CHUNK if ci % 2 == 0 else (ci - 1) * CHUNK
        win = slice(c0, c0 + 2 * LANES)
        pv.append((_dot(vt[:, win], probs[ci][:, :2 * LANES]),
                   _dot(vs[:, win], probs[ci][:, 2 * LANES:])))
    sv = [_dot(s_bf[h], jnp.concatenate([m_slice(2, h), ones], axis=1)) for h in heads]
    c_upd = [_dot(upd_lhs[h], m_slice(1, h)) for h in heads]
    yield

    for ci in range(n_chunks):
        r0 = ci * CHUNK
        ra = pv[ci][0] * rdens[ci][:, :2 * LANES]
        rb = pv[ci][1] * rdens[ci][:, 2 * LANES:]
        o0 = jnp.concatenate([ra[:half, :LANES], rb[half:, :LANES]], axis=0).T.astype(BF16)
        o1 = jnp.concatenate([rb[:half, LANES:], ra[half:, LANES:]], axis=0).T.astype(BF16)
        mix_ref[r0:r0 + CHUNK, 0 * LANES:1 * LANES] = o0[:CHUNK]
        mix_ref[r0:r0 + CHUNK, 1 * LANES:2 * LANES] = o0[CHUNK:]
        mix_ref[r0:r0 + CHUNK, 2 * LANES:3 * LANES] = o1[:CHUNK]
        mix_ref[r0:r0 + CHUNK, 3 * LANES:4 * LANES] = o1[CHUNK:]
    for h in heads:
        tot = sv[h] + w_inter[:, h:h + 1] * qc[h]
        hh = tot[:, :M_HEAD_DIM] / jnp.maximum(jnp.abs(tot[:, M_HEAD_DIM:]), e_neg_mt[:, h:h + 1])
        hn = _rms(hh, mg_ref[:, h * M_HEAD_DIM:(h + 1) * M_HEAD_DIM])
        out = hn * out_gate[h]
        mix_ref[:, A_Q + h * M_HEAD_DIM:A_Q + (h + 1) * M_HEAD_DIM] = out.astype(BF16)
        cst[h] = decay[:, h:h + 1] * c_old[h] + c_upd[h]
    m_ref[0] = jnp.broadcast_to(m_last, (SUBLANES, LANES))
    yield

    def write_state():
        for h in heads:
            c_ref[0, h] = cst[h, :M_HEAD_DIM, :]
            n_ref[0, h:h + 1, :] = cst[h, M_HEAD_DIM:M_HEAD_DIM + 1, :]
        if m_out_ref is not None:
            m_out_ref[...] = m_ref[...]

    if steps == 1 and write_ok is None:
        write_state()
    else:
        last = t == steps - 1
        pl.when(last if write_ok is None else jnp.logical_and(last, write_ok))(write_state)

    if steps > 1:
        kbuf[0:WINDOW, :] = kbuf[L:L + WINDOW, :]
        vt[:, 0:WINDOW] = vt[:, L:L + WINDOW]
        vs[:, 0:WINDOW] = vs[:, L:L + WINDOW]


def _interleave(gens):
    done = object()
    live = list(gens)
    while live:
        live = [g for g in live if next(g, done) is not done]


def _mixer_kernel(sink_ref, q_ref, k_ref, v_ref, m4_ref, gt_ref, hk_ref, hv_ref, c0_ref, n0_ref,
                  m0_ref, mg_ref, mix_ref, c_ref, n_ref, m_ref, kbuf, vt, vs, cst,
                  *, group, **static):
    def stages(i):
        one = lambda ref: ref.at[pl.ds(i, 1)]

        def init_state(put_vt):
            kbuf[i, 0:WINDOW, :] = hk_ref[i].astype(BF16)
            put_vt(pl.ds(0, WINDOW), hv_ref[i])
            for h in range(M_HEADS):
                cst[i, h, :M_HEAD_DIM, :] = c0_ref[i, h]
                cst[i, h, M_HEAD_DIM:, :] = jnp.broadcast_to(
                    n0_ref[i, h:h + 1, :], (M_HEAD_DIM, M_HEAD_DIM))
            m_ref[i] = m0_ref[i]

        return _mixer_stages(
            pl.program_id(1), sink_ref, q_ref.at[i], k_ref.at[i], v_ref.at[i], m4_ref.at[i],
            gt_ref.at[i], init_state, mg_ref, mix_ref.at[i], one(c_ref), one(n_ref), one(m_ref),
            kbuf.at[i], vt.at[i], vs.at[i], cst.at[i], **static)

    _interleave([stages(i) for i in range(group)])


def _mixer(q, k, v, m4, gt, halo_k, halo_v, c0, n0, m0, sink, mg,
           *, batch, group, L, halo_valid, layer, state_layer):
    n = q.shape[0]
    T = n // batch
    steps = T // L
    assert steps == 1 or L % LANES == 0
    assert batch % group == 0
    tile = lambda b, t: (b, t, 0)
    per_b3 = lambda b, t: (b, 0, 0)
    per_b4 = lambda b, t: (b, 0, 0, 0)
    init3 = lambda b, t: (state_layer, b, 0, 0)
    init4 = lambda b, t: (state_layer, b, 0, 0, 0)
    kv_cols = max(WINDOW + L, 2 * LANES)
    by_seq = lambda a: a.reshape(batch, T, a.shape[-1])
    mix, C, n_out, m_out = pl.pallas_call(
        functools.partial(_mixer_kernel, group=group, L=L, steps=steps, halo_valid=halo_valid,
                          layer=layer),
        grid=(batch // group, steps),
        in_specs=[
            pl.BlockSpec(memory_space=pltpu.SMEM),
            pl.BlockSpec((group, L, Q_PAD), tile),
            pl.BlockSpec((group, L, A_KV), tile),
            pl.BlockSpec((group, L, A_KV), tile),
            pl.BlockSpec((group, L, 4 * M_W), tile),
            pl.BlockSpec((group, L, GATE_COLS), tile),
            pl.BlockSpec((None, group, WINDOW, A_KV), init3),
            pl.BlockSpec((None, group, WINDOW, A_KV), init3),
            pl.BlockSpec((None, group, M_HEADS, M_HEAD_DIM, M_HEAD_DIM), init4),
            pl.BlockSpec((None, group, M_HEADS, M_HEAD_DIM), init3),
            pl.BlockSpec((None, group, SUBLANES, LANES), init3),
            pl.BlockSpec((None, 1, M_W), lambda b, t: (layer, 0, 0)),
        ],
        out_specs=[
            pl.BlockSpec((group, L, D_MODEL), tile),
            pl.BlockSpec((group, M_HEADS, M_HEAD_DIM, M_HEAD_DIM), per_b4),
            pl.BlockSpec((group, M_HEADS, M_HEAD_DIM), per_b3),
            pl.BlockSpec((group, SUBLANES, LANES), per_b3),
        ],
        out_shape=[
            jax.ShapeDtypeStruct((batch, T, D_MODEL), BF16),
            jax.ShapeDtypeStruct((batch, M_HEADS, M_HEAD_DIM, M_HEAD_DIM), F32),
            jax.ShapeDtypeStruct((batch, M_HEADS, M_HEAD_DIM), F32),
            jax.ShapeDtypeStruct((batch, SUBLANES, LANES), F32),
        ],
        scratch_shapes=[
            pltpu.VMEM((group, WINDOW + L, LANES), BF16),
            pltpu.VMEM((group, LANES, kv_cols), BF16),
            pltpu.VMEM((group, LANES, kv_cols), BF16),
            pltpu.VMEM((group, M_HEADS, 2 * M_HEAD_DIM, M_HEAD_DIM), F32),
        ],
        compiler_params=pltpu.CompilerParams(
            dimension_semantics=("arbitrary", "arbitrary"), vmem_limit_bytes=VMEM_LIMIT),
        name="mixer",
    )(sink, by_seq(q), by_seq(k), by_seq(v), by_seq(m4), by_seq(gt), halo_k, halo_v, c0, n0, m0, mg)
    return mix.reshape(n, D_MODEL), C, n_out, m_out


def _ffn_stages(x, mix, wo_ref, g2_ref, wu_ref, wd_ref, fg_ref, store):
    x1 = x + _dot(mix, wo_ref[...])
    xn = _rms(x1, g2_ref[...]).astype(BF16)
    acc = x1
    yield
    for c in range(D_FF // FF_CHUNK):
        u = _dot(xn, wu_ref[:, c * FF_CHUNK:(c + 1) * FF_CHUNK])
        a = jnp.square(jnp.maximum(u, 0.0)).astype(BF16)
        acc = acc + _dot(a, wd_ref[c * FF_CHUNK:(c + 1) * FF_CHUNK, :])
        if c + 1 < D_FF // FF_CHUNK:
            yield
    if fg_ref is not None:
        acc = _rms(acc, fg_ref[...])
    store(acc)


def _ffn_kernel(*refs, final):
    if final:
        x_ref, mix_ref, wo_ref, g2_ref, wu_ref, wd_ref, fg_ref, out_ref = refs
    else:
        x_ref, mix_ref, wo_ref, g2_ref, wu_ref, wd_ref, out_ref = refs
        fg_ref = None

    def store(acc):
        out_ref[...] = acc

    _interleave([_ffn_stages(x_ref[...], mix_ref[...], wo_ref, g2_ref, wu_ref, wd_ref, fg_ref, store)])


def _back_kernel(*refs, group, L, steps, n_steps, layer, final):
    (sink_ref, q_ref, k_ref, v_ref, m4_ref, gt_ref, mg_ref, x_ref, wo_ref, g2_ref, wu_ref,
     wd_ref) = refs[:12]
    fg_ref = refs[12] if final else None
    y_ref, c_ref, n_ref, m_ref, kbuf, vt, vs, cst, mst, mix_z = refs[12 + final:]
    s = pl.program_id(0)
    wslot = s % 2
    rslot = 1 - wslot
    t = jnp.minimum(s, n_steps - 1) % steps
    write_ok = s < n_steps

    @pl.when(s == 0)
    def _():
        mix_z[1] = jnp.zeros(mix_z.shape[1:], mix_z.dtype)

    def mixer(i):
        one = lambda ref: ref.at[pl.ds(i, 1)]

        def init_state(put_vt):
            kbuf[i, 0:WINDOW, :] = jnp.zeros((WINDOW, LANES), BF16)
            cst[i] = jnp.zeros(cst.shape[1:], cst.dtype)
            mst[i] = jnp.zeros(mst.shape[1:], mst.dtype)

        return _mixer_stages(
            t, sink_ref, q_ref.at[i], k_ref.at[i], v_ref.at[i], m4_ref.at[i], gt_ref.at[i],
            init_state, mg_ref, mix_z.at[wslot, i], one(c_ref), one(n_ref), one(mst),
            kbuf.at[i], vt.at[i], vs.at[i], cst.at[i],
            L=L, steps=steps, halo_valid=False, layer=layer, write_ok=write_ok,
            m_out_ref=one(m_ref))

    def store(acc):
        y_ref[...] = acc.reshape(group, L, D_MODEL)

    rows = group * L
    ffn = _ffn_stages(x_ref[...].reshape(rows, D_MODEL), mix_z[rslot].reshape(rows, D_MODEL),
                      wo_ref, g2_ref, wu_ref, wd_ref, fg_ref, store)
    mixers = [mixer(i) for i in range(group)]
    every = lambda: [next(m) for m in mixers]
    every()
    next(ffn)
    every()
    next(ffn)
    every()
    every()
    next(ffn)
    every()
    _interleave([ffn])
    _interleave(mixers)


def _ffn(x, mix, wo, g2, wu, wd, fg, tm, layer):
    n = x.shape[0]
    final = fg is not None
    row = lambda i: (i, 0)
    fixed = lambda i: (0, 0)
    of_layer = lambda i: (layer, 0, 0)
    single = pl.Buffered(1)
    in_specs = [
        pl.BlockSpec((tm, D_MODEL), row),
        pl.BlockSpec((tm, D_MODEL), row),
        pl.BlockSpec((None, D_MODEL, D_MODEL), of_layer, pipeline_mode=single),
        pl.BlockSpec((None, 1, D_MODEL), of_layer),
        pl.BlockSpec((None, D_MODEL, D_FF), of_layer, pipeline_mode=single),
        pl.BlockSpec((None, D_FF, D_MODEL), of_layer, pipeline_mode=single),
    ]
    args = [x, mix, wo, g2, wu, wd]
    if final:
        in_specs.append(pl.BlockSpec((1, D_MODEL), fixed))
        args.append(fg)
    return pl.pallas_call(
        functools.partial(_ffn_kernel, final=final),
        grid=(n // tm,),
        in_specs=in_specs,
        out_specs=pl.BlockSpec((tm, D_MODEL), row),
        out_shape=jax.ShapeDtypeStruct((n, D_MODEL), F32),
        compiler_params=pltpu.CompilerParams(
            dimension_semantics=("arbitrary",), vmem_limit_bytes=VMEM_LIMIT),
        name="ffn_final" if final else "ffn",
    )(*args)


def _back(x, q, k, v, m4, gt, sink, mg, wo, g2, wu, wd, fg, *, batch, group, L, layer):
    n = x.shape[0]
    T = n // batch
    steps = T // L
    n_steps = (batch // group) * steps
    assert steps > 1 and L % LANES == 0 and batch % group == 0
    final = fg is not None
    mixed = lambda s: jnp.minimum(s, n_steps - 1)
    fed = lambda s: jnp.maximum(s - 1, 0)
    mix_tile = lambda s: (mixed(s) // steps, mixed(s) % steps, 0)
    ffn_tile = lambda s: (fed(s) // steps, fed(s) % steps, 0)
    seq3 = lambda s: (mixed(s) // steps, 0, 0)
    of_layer = lambda s: (layer, 0, 0)
    single = pl.Buffered(1)
    by_seq = lambda a: a.reshape(batch, T, a.shape[-1])
    in_specs = [
        pl.BlockSpec(memory_space=pltpu.SMEM),
        pl.BlockSpec((group, L, Q_PAD), mix_tile),
        pl.BlockSpec((group, L, A_KV), mix_tile),
        pl.BlockSpec((group, L, A_KV), mix_tile),
        pl.BlockSpec((group, L, 4 * M_W), mix_tile),
        pl.BlockSpec((group, L, GATE_COLS), mix_tile),
        pl.BlockSpec((None, 1, M_W), of_layer),
        pl.BlockSpec((group, L, D_MODEL), ffn_tile),
        pl.BlockSpec((None, D_MODEL, D_MODEL), of_layer, pipeline_mode=single),
        pl.BlockSpec((None, 1, D_MODEL), of_layer),
        pl.BlockSpec((None, D_MODEL, D_FF), of_layer, pipeline_mode=single),
        pl.BlockSpec((None, D_FF, D_MODEL), of_layer, pipeline_mode=single),
    ]
    args = [sink, by_seq(q), by_seq(k), by_seq(v), by_seq(m4), by_seq(gt), mg, by_seq(x),
            wo, g2, wu, wd]
    if final:
        in_specs.append(pl.BlockSpec((1, D_MODEL), lambda s: (0, 0)))
        args.append(fg)
    kv_cols = WINDOW + L
    y, C, n_out, m_out = pl.pallas_call(
        functools.partial(_back_kernel, group=group, L=L, steps=steps, n_steps=n_steps,
                          layer=layer, final=final),
        grid=(n_steps + 1,),
        in_specs=in_specs,
        out_specs=[
            pl.BlockSpec((group, L, D_MODEL), ffn_tile),
            pl.BlockSpec((group, M_HEADS, M_HEAD_DIM, M_HEAD_DIM), lambda s: (mixed(s) // steps, 0, 0, 0)),
            pl.BlockSpec((group, M_HEADS, M_HEAD_DIM), seq3),
            pl.BlockSpec((group, SUBLANES, LANES), seq3),
        ],
        out_shape=[
            jax.ShapeDtypeStruct((batch, T, D_MODEL), F32),
            jax.ShapeDtypeStruct((batch, M_HEADS, M_HEAD_DIM, M_HEAD_DIM), F32),
            jax.ShapeDtypeStruct((batch, M_HEADS, M_HEAD_DIM), F32),
            jax.ShapeDtypeStruct((batch, SUBLANES, LANES), F32),
        ],
        scratch_shapes=[
            pltpu.VMEM((group, WINDOW + L, LANES), BF16),
            pltpu.VMEM((group, LANES, kv_cols), BF16),
            pltpu.VMEM((group, LANES, kv_cols), BF16),
            pltpu.VMEM((group, M_HEADS, 2 * M_HEAD_DIM, M_HEAD_DIM), F32),
            pltpu.VMEM((group, SUBLANES, LANES), F32),
            pltpu.VMEM((2, group, L, D_MODEL), BF16),
        ],
        compiler_params=pltpu.CompilerParams(
            dimension_semantics=("arbitrary",), vmem_limit_bytes=VMEM_LIMIT),
        name="back_final" if final else "back",
    )(*args)
    return y.reshape(n, D_MODEL), C, n_out, m_out


def _rotary_tables(pos, rows):
    inv = ROPE_THETA ** (-(jnp.arange(ROPE_HALF, dtype=F32) * 2.0 / ROPE_DIM))
    ang = pos[:, None] * inv[None, :]
    cos, sin = jnp.cos(ang), jnp.sin(ang)
    n = pos.shape[0]
    pad = jnp.zeros((n, A_HEAD_DIM - ROPE_DIM), F32)
    zeros = jnp.zeros_like(sin)
    cos_h = jnp.concatenate([cos, cos, pad + 1.0], axis=1)
    sa_h = jnp.concatenate([-sin, zeros, pad], axis=1)
    sb_h = jnp.concatenate([zeros, sin, pad], axis=1)
    rep = LANES // A_HEAD_DIM
    tile = lambda a: jnp.tile(a, (rows // n, rep))
    return tile(cos_h), tile(sa_h), tile(sb_h)


def _gate_weight_tiles(w):
    gate_pad = jnp.zeros(w.shape[:2] + (LANES - M_HEADS,), w.dtype)
    return jnp.concatenate(
        [w[..., COL_G:COL_G + M_HEADS], gate_pad, w[..., COL_G + M_HEADS:], gate_pad],
        axis=-1).astype(BF16)


def _pad_gate_bias(gb):
    pad = jnp.zeros((gb.shape[0], LANES - M_HEADS), gb.dtype)
    return jnp.concatenate([gb[:, :M_HEADS], pad, gb[:, M_HEADS:], pad], axis=1)[:, None, :]


def _m_rows(m):
    m_row = jnp.pad(m, [(0, 0)] * (m.ndim - 1) + [(0, LANES - M_HEADS)])
    return jnp.broadcast_to(m_row[..., None, :], m.shape[:-1] + (SUBLANES, LANES))


def kernel(x_prompt, x_sample, cache_k, cache_v, state_C, state_n, state_m, norm1_g, w_in, gate_b,
           attn_sink, mnorm_g, w_out, norm2_g, w_up, w_down, final_g):
    bp, T, _ = x_prompt.shape
    bs, S, _ = x_sample.shape
    depth = w_in.shape[0]
    assert T % PROMPT_MLSTM_CHUNK == 0 and S == CHUNK
    assert T % TOKEN_TILE == 0 and TOKEN_TILE % S == 0 and (bs * S) % TOKEN_TILE == 0

    tabs_p = _rotary_tables(jnp.arange(T, dtype=F32), max(T, TOKEN_TILE))
    tabs_s = _rotary_tables(jnp.arange(S, dtype=F32) + float(PAST_LEN), max(S, TOKEN_TILE))

    g1 = norm1_g[:, None, :]
    g2 = norm2_g[:, None, :]
    mg = mnorm_g[:, None, :]
    w1 = w_in[..., :COL_G].astype(BF16)
    wg = _gate_weight_tiles(w_in)
    gb = _pad_gate_bias(gate_b)
    wo = w_out.astype(BF16)
    wu = w_up.astype(BF16)
    wd = w_down.astype(BF16)
    halo_k = cache_k.reshape(depth, bs, WINDOW, A_KV)
    halo_v = cache_v.reshape(depth, bs, WINDOW, A_KV)
    m0 = _m_rows(state_m)

    yp = x_prompt.reshape(bp * T, D_MODEL)
    ys = x_sample.reshape(bs * S, D_MODEL)
    outs = {name: [] for name in ("pk", "pv", "pC", "pn", "pm", "sk", "sv", "sC", "sn", "sm")}
    fg = final_g.reshape(1, D_MODEL)
    for l in range(depth):
        last = fg if l == depth - 1 else None

        q, k, v, m4, gt = _inproj(yp, g1, w1, wg, tabs_p, gb, TOKEN_TILE, l)
        yp, C, n, m_rows = _back(yp, q, k, v, m4, gt, attn_sink, mg, wo, g2, wu, wd, last,
                                 batch=bp, group=SEQ_GROUP, L=PROMPT_MLSTM_CHUNK, layer=l)
        m = m_rows[:, 0, :M_HEADS]
        outs["pk"].append(k.reshape(bp, T, A_KV)[:, T - WINDOW:])
        outs["pv"].append(v.reshape(bp, T, A_KV)[:, T - WINDOW:])
        outs["pC"].append(C)
        outs["pn"].append(n)
        outs["pm"].append(m)

        q, k, v, m4, gt = _inproj(ys, g1, w1, wg, tabs_s, gb, TOKEN_TILE, l)
        mix, C, n, m_rows = _mixer(q, k, v, m4, gt, halo_k, halo_v, state_C, state_n, m0,
                                   attn_sink, mg, batch=bs, group=SEQ_GROUP, L=S, halo_valid=True,
                                   layer=l, state_layer=l)
        ys = _ffn(ys, mix, wo, g2, wu, wd, last, TOKEN_TILE, l)
        m = m_rows[:, 0, :M_HEADS]
        outs["sk"].append(k.reshape(bs, S, A_KV))
        outs["sv"].append(v.reshape(bs, S, A_KV))
        outs["sC"].append(C)
        outs["sn"].append(n)
        outs["sm"].append(m)

    st = lambda name: jnp.stack(outs[name])
    kv = lambda name: (lambda a: a.reshape(a.shape[:-1] + (A_KV_HEADS, A_HEAD_DIM)))(st(name))
    return (yp.reshape(bp, T, D_MODEL), ys.reshape(bs, S, D_MODEL),
            kv("pk"), kv("pv"), st("pC"), st("pn"), st("pm"),
            kv("sk"), kv("sv"), st("sC"), st("sn"), st("sm"))
```

```python
import functools

import jax
import jax.numpy as jnp
from jax import lax
from jax.experimental import pallas as pl
from jax.experimental.pallas import tpu as pltpu

D_MODEL = 1024
CHUNK = 64
A_HEADS = 8
A_KV_HEADS = 2
A_HEAD_DIM = 64
WINDOW = 128
ROPE_THETA = 500000.0
ROPE_DIM = A_HEAD_DIM // 4
ROPE_HALF = ROPE_DIM // 2
M_HEADS = 4
M_HEAD_DIM = 128
D_FF = 4 * D_MODEL
NORM_EPS = 1e-6
NEG = -1e30
LOG2E = 1.4426950408889634
PAST_LEN = 4096
A_Q = A_HEADS * A_HEAD_DIM
A_KV = A_KV_HEADS * A_HEAD_DIM
M_W = M_HEADS * M_HEAD_DIM

LANES = 128
SUBLANES = 8
Q_PAD = A_HEADS * LANES
GATE_COLS = 2 * LANES
COL_KV = A_Q
COL_M = A_Q + 2 * A_KV
COL_G = COL_M + 4 * M_W
Q_HEAD_ORDER = (0, 2, 5, 7, 1, 3, 4, 6)

TOKEN_TILE = 512
PROMPT_MLSTM_CHUNK = 256
SEQ_GROUP = 2
FF_CHUNK = 1024
VMEM_LIMIT = 56 * 1024 * 1024

BF16 = jnp.bfloat16
F32 = jnp.float32


def _dot(a, b):
    return jnp.dot(a, b, preferred_element_type=F32)


def _dot_nt(a, b):
    return lax.dot_general(a, b, (((1,), (1,)), ((), ())), preferred_element_type=F32)


def _rms(x, g):
    r = lax.rsqrt(jnp.mean(x * x, axis=-1, keepdims=True) + NORM_EPS)
    return x * r * g


def _projection_parts(x_ref, g_ref, w_ref, wg_ref, cos_ref, sa_ref, sb_ref, gb_ref,
                      q_ref, k_ref, v_ref, m4_ref, gt_ref):
    h = _rms(x_ref[...], g_ref[...]).astype(BF16)
    low = lax.broadcasted_iota(jnp.int32, (1, LANES), 1) < A_HEAD_DIM

    def rotary(z):
        left = pltpu.roll(z, LANES - ROPE_HALF, axis=1)
        right = pltpu.roll(z, ROPE_HALF, axis=1)
        return z * cos_ref[...] + left * sa_ref[...] + right * sb_ref[...]

    def attention_part():
        zq_all = _dot(h, w_ref[:, :A_Q])
        zkv = _dot(h, w_ref[:, COL_KV:COL_KV + 2 * A_KV])
        for pair in range(A_Q // LANES):
            zq = rotary(zq_all[:, pair * LANES:(pair + 1) * LANES]) * (A_HEAD_DIM ** -0.5 * LOG2E)
            zsw = pltpu.roll(zq, A_HEAD_DIM, axis=1)
            zero = jnp.zeros_like(zq)
            if pair < A_HEADS // (2 * A_KV_HEADS):
                even, odd = jnp.where(low, zq, zero), jnp.where(low, zsw, zero)
            else:
                even, odd = jnp.where(low, zero, zsw), jnp.where(low, zero, zq)
            for head, val in ((2 * pair, even), (2 * pair + 1, odd)):
                j = Q_HEAD_ORDER.index(head)
                q_ref[:, j * LANES:(j + 1) * LANES] = val.astype(BF16)
        k_ref[...] = rotary(zkv[:, :A_KV])
        v_ref[...] = zkv[:, A_KV:]

    def mlstm_qk_part():
        for j in range(2):
            zm = _dot(h, w_ref[:, COL_M + j * M_W:COL_M + (j + 1) * M_W])
            if j == 1:
                zm = zm * (M_HEAD_DIM ** -0.5)
            m4_ref[:, j * M_W:(j + 1) * M_W] = zm.astype(BF16)

    def mlstm_vo_part():
        for j in range(2, 4):
            zm = _dot(h, w_ref[:, COL_M + j * M_W:COL_M + (j + 1) * M_W])
            m4_ref[:, j * M_W:(j + 1) * M_W] = zm.astype(BF16)
        zg = _dot(h, wg_ref[...]) + gb_ref[...]
        zf = zg[:, LANES:]
        gt_ref[:, :LANES] = zg[:, :LANES]
        gt_ref[:, LANES:] = jnp.minimum(zf, 0.0) - jnp.log1p(jnp.exp(-jnp.abs(zf)))

    return attention_part, mlstm_qk_part, mlstm_vo_part


def _inproj_kernel(*refs):
    for part in _projection_parts(*refs):
        part()


def _inproj(x, g1, w, wg, tabs, gb, tm, layer):
    n = x.shape[0]
    cos, sa, sb = tabs
    nt = cos.shape[0] // tm
    row = lambda i: (i, 0)
    of_layer = lambda i: (layer, 0, 0)
    tab = lambda i: (i % nt, 0)
    return pl.pallas_call(
        _inproj_kernel,
        grid=(n // tm,),
        in_specs=[
            pl.BlockSpec((tm, D_MODEL), row),
            pl.BlockSpec((None, 1, D_MODEL), of_layer),
            pl.BlockSpec((None, D_MODEL, COL_G), of_layer),
            pl.BlockSpec((None, D_MODEL, GATE_COLS), of_layer),
            pl.BlockSpec((tm, LANES), tab),
            pl.BlockSpec((tm, LANES), tab),
            pl.BlockSpec((tm, LANES), tab),
            pl.BlockSpec((None, 1, GATE_COLS), of_layer),
        ],
        out_specs=[
            pl.BlockSpec((tm, Q_PAD), row),
            pl.BlockSpec((tm, A_KV), row),
            pl.BlockSpec((tm, A_KV), row),
            pl.BlockSpec((tm, 4 * M_W), row),
            pl.BlockSpec((tm, GATE_COLS), row),
        ],
        out_shape=[
            jax.ShapeDtypeStruct((n, Q_PAD), BF16),
            jax.ShapeDtypeStruct((n, A_KV), F32),
            jax.ShapeDtypeStruct((n, A_KV), F32),
            jax.ShapeDtypeStruct((n, 4 * M_W), BF16),
            jax.ShapeDtypeStruct((n, GATE_COLS), F32),
        ],
        compiler_params=pltpu.CompilerParams(
            dimension_semantics=("arbitrary",), vmem_limit_bytes=VMEM_LIMIT),
        name="inproj",
    )(x, g1, w, wg, cos, sa, sb, gb)


def _split3(x):
    hi = x.astype(BF16)
    r1 = x - hi.astype(F32)
    mid = r1.astype(BF16)
    lo = (r1 - mid.astype(F32)).astype(BF16)
    return hi, mid, lo


def _mixer_stages(t, sink_ref, q_ref, k_ref, v_ref, m4_ref, gt_ref, init_state, mg_ref,
                  mix_ref, c_ref, n_ref, m_ref, kbuf, vt, vs, cst,
                  *, L, steps, halo_valid, layer, write_ok=None, m_out_ref=None):
    n_chunks = L // CHUNK
    n_keys = WINDOW + CHUNK
    half = A_HEAD_DIM

    def put_vt(cols, v):
        v_t = v.T
        vt[:, cols] = v_t.astype(BF16)
        vs[:, cols] = jnp.concatenate([v_t[half:], v_t[:half]], axis=0).astype(BF16)

    @pl.when(t == 0)
    def _():
        vt[...] = jnp.zeros_like(vt)
        vs[...] = jnp.zeros_like(vs)
        init_state(put_vt)

    yield

    kbuf[WINDOW:WINDOW + L, :] = k_ref[...].astype(BF16)
    put_vt(pl.ds(WINDOW, L), v_ref[...])

    heads = range(M_HEADS)
    m_slice = lambda j, h: m4_ref[:, j * M_W + h * M_HEAD_DIM:j * M_W + (h + 1) * M_HEAD_DIM]

    scores = []
    for ci in range(n_chunks):
        r0 = ci * CHUNK
        q8 = jnp.concatenate([q_ref[r0:r0 + CHUNK, j * LANES:(j + 1) * LANES]
                              for j in range(A_HEADS)], axis=0)
        scores.append(_dot_nt(kbuf[r0:r0 + n_keys, :], q8))

    gates_i = gt_ref[:, :LANES]
    gates_f = gt_ref[:, LANES:]
    ri = lax.broadcasted_iota(jnp.int32, (L, L), 0)
    ci_ = lax.broadcasted_iota(jnp.int32, (L, L), 1)
    causal = ci_ <= ri
    b3 = _dot(causal.astype(BF16), jnp.concatenate(_split3(gates_f), axis=1))
    b = b3[:, :LANES] + b3[:, LANES:2 * LANES] + b3[:, 2 * LANES:]
    c_old = [cst[h] for h in heads]
    qk = [_dot_nt(m_slice(0, h), m_slice(1, h)) for h in heads]
    qc = [_dot_nt(m_slice(0, h), c_old[h].astype(BF16)) for h in heads]
    yield

    lane_q = lax.broadcasted_iota(jnp.int32, (1, A_HEADS * CHUNK), 1) // CHUNK
    sink_row = jnp.zeros((1, A_HEADS * CHUNK), F32)
    for j, head in enumerate(Q_HEAD_ORDER):
        sink_row = jnp.where(lane_q == j, sink_ref[layer, head] * LOG2E, sink_row)
    key_row = lax.broadcasted_iota(jnp.int32, (n_keys, A_HEADS * CHUNK), 0)
    probs, rdens = [], []
    for ci in range(n_chunks):
        s = scores[ci]
        if not halo_valid and ci < WINDOW // CHUNK:
            first_key = (t * n_chunks + ci - WINDOW // CHUNK) * CHUNK
            s = jnp.where(key_row + first_key >= 0, s, NEG)
        mx = jnp.maximum(jnp.max(s, axis=0, keepdims=True), sink_row)
        p = jnp.exp2(s - mx)
        rdens.append(1.0 / (jnp.sum(p, axis=0, keepdims=True) + jnp.exp2(sink_row - mx)))
        pad = jnp.zeros((CHUNK, A_HEADS * CHUNK), BF16)
        pb = p.astype(BF16)
        probs.append(jnp.concatenate([pb, pad] if ci % 2 == 0 else [pad, pb], axis=0))

    a = gates_i - b
    row_l = lax.broadcasted_iota(jnp.int32, (L, LANES), 0)
    cm = a
    shift = 1
    while shift < L:
        cm = jnp.maximum(cm, jnp.where(row_l >= shift, pltpu.roll(cm, shift, axis=0), NEG))
        shift *= 2
    m_prev = m_ref[0, 0:1, :]
    c = jnp.maximum(cm, m_prev)
    mt = c + b
    w_inter = jnp.exp(m_prev - c)
    e_neg_mt = jnp.exp(-mt)
    m_last = mt[L - 1:L, :]
    b_last = b[L - 1:L, :]
    delta = b_last - m_last
    decay = jnp.exp(b_last + m_prev - m_last)
    a_t = (a * LOG2E).T
    c2 = c * LOG2E
    delta2 = delta * LOG2E
    blk = min(L, LANES)
    n_blk = L // blk
    tri = (lax.broadcasted_iota(jnp.int32, (blk, blk), 1)
           <= lax.broadcasted_iota(jnp.int32, (blk, blk), 0))
    ones = jnp.ones((L, M_HEAD_DIM), BF16)
    out_gate = [jax.nn.sigmoid(m_slice(3, h).astype(F32)) for h in heads]
    s_bf, upd_lhs = [], []
    for h in heads:
        a_row = a_t[h:h + 1, :]
        row_blocks = []
        for rb in range(n_blk):
            rows = slice(rb * blk, (rb + 1) * blk)
            c_col = c2[rows, h:h + 1]
            e = []
            for cb in range(rb + 1):
                d = a_row[:, cb * blk:(cb + 1) * blk] - c_col
                e.append(jnp.exp2(jnp.where(tri, d, NEG) if cb == rb else d))
            s = (qk[h][rows, :(rb + 1) * blk] * jnp.concatenate(e, axis=1)).astype(BF16)
            if rb + 1 < n_blk:
                s = jnp.concatenate([s, jnp.zeros((blk, L - (rb + 1) * blk), BF16)], axis=1)
            row_blocks.append(s)
        s_bf.append(jnp.concatenate(row_blocks, axis=0))
        ws_row = jnp.exp2(a_row + delta2[:, h:h + 1])
        v_t = m_slice(2, h).astype(F32).T
        upd_lhs.append(jnp.concatenate(
            [v_t * ws_row, jnp.broadcast_to(ws_row, (M_HEAD_DIM, L))], axis=0).astype(BF16))

    yield

    pv = []
    for ci in range(n_chunks):
        c0 = ci * CHUNK if ci % 2 == 0 else (ci - 1) * CHUNK
        win = slice(c0, c0 + 2 * LANES)
        pv.append((_dot(vt[:, win], probs[ci][:, :2 * LANES]),
                   _dot(vs[:, win], probs[ci][:, 2 * LANES:])))
    sv = [_dot(s_bf[h], jnp.concatenate([m_slice(2, h), ones], axis=1)) for h in heads]
    c_upd = [_dot(upd_lhs[h], m_slice(1, h)) for h in heads]
    yield

    for ci in range(n_chunks):
        r0 = ci * CHUNK
        ra = pv[ci][0] * rdens[ci][:, :2 * LANES]
        rb = pv[ci][1] * rdens[ci][:, 2 * LANES:]
        o0 = jnp.concatenate([ra[:half, :LANES], rb[half:, :LANES]], axis=0).T.astype(BF16)
        o1 = jnp.concatenate([rb[:half, LANES:], ra[half:, LANES:]], axis=0).T.astype(BF16)
        mix_ref[r0:r0 + CHUNK, 0 * LANES:1 * LANES] = o0[:CHUNK]
        mix_ref[r0:r0 + CHUNK, 1 * LANES:2 * LANES] = o0[CHUNK:]
        mix_ref[r0:r0 + CHUNK, 2 * LANES:3 * LANES] = o1[:CHUNK]
        mix_ref[r0:r0 + CHUNK, 3 * LANES:4 * LANES] = o1[CHUNK:]
    for h in heads:
        tot = sv[h] + w_inter[:, h:h + 1] * qc[h]
        hh = tot[:, :M_HEAD_DIM] / jnp.maximum(jnp.abs(tot[:, M_HEAD_DIM:]), e_neg_mt[:, h:h + 1])
        hn = _rms(hh, mg_ref[:, h * M_HEAD_DIM:(h + 1) * M_HEAD_DIM])
        out = hn * out_gate[h]
        mix_ref[:, A_Q + h * M_HEAD_DIM:A_Q + (h + 1) * M_HEAD_DIM] = out.astype(BF16)
        cst[h] = decay[:, h:h + 1] * c_old[h] + c_upd[h]
    m_ref[0] = jnp.broadcast_to(m_last, (SUBLANES, LANES))
    yield

    def write_state():
        for h in heads:
            c_ref[0, h] = cst[h, :M_HEAD_DIM, :]
            n_ref[0, h:h + 1, :] = cst[h, M_HEAD_DIM:M_HEAD_DIM + 1, :]
        if m_out_ref is not None:
            m_out_ref[...] = m_ref[...]

    if steps == 1 and write_ok is None:
        write_state()
    else:
        last = t == steps - 1
        pl.when(last if write_ok is None else jnp.logical_and(last, write_ok))(write_state)

    if steps > 1:
        kbuf[0:WINDOW, :] = kbuf[L:L + WINDOW, :]
        vt[:, 0:WINDOW] = vt[:, L:L + WINDOW]
        vs[:, 0:WINDOW] = vs[:, L:L + WINDOW]


def _interleave(gens):
    done = object()
    live = list(gens)
    while live:
        live = [g for g in live if next(g, done) is not done]


def _mixer_kernel(sink_ref, q_ref, k_ref, v_ref, m4_ref, gt_ref, hk_ref, hv_ref, c0_ref, n0_ref,
                  m0_ref, mg_ref, mix_ref, c_ref, n_ref, m_ref, kbuf, vt, vs, cst,
                  *, group, **static):
    def stages(i):
        one = lambda ref: ref.at[pl.ds(i, 1)]

        def init_state(put_vt):
            kbuf[i, 0:WINDOW, :] = hk_ref[i].astype(BF16)
            put_vt(pl.ds(0, WINDOW), hv_ref[i])
            for h in range(M_HEADS):
                cst[i, h, :M_HEAD_DIM, :] = c0_ref[i, h]
                cst[i, h, M_HEAD_DIM:, :] = jnp.broadcast_to(
                    n0_ref[i, h:h + 1, :], (M_HEAD_DIM, M_HEAD_DIM))
            m_ref[i] = m0_ref[i]

        return _mixer_stages(
            pl.program_id(1), sink_ref, q_ref.at[i], k_ref.at[i], v_ref.at[i], m4_ref.at[i],
            gt_ref.at[i], init_state, mg_ref, mix_ref.at[i], one(c_ref), one(n_ref), one(m_ref),
            kbuf.at[i], vt.at[i], vs.at[i], cst.at[i], **static)

    _interleave([stages(i) for i in range(group)])


def _mixer(q, k, v, m4, gt, halo_k, halo_v, c0, n0, m0, sink, mg,
           *, batch, group, L, halo_valid, layer, state_layer):
    n = q.shape[0]
    T = n // batch
    steps = T // L
    assert steps == 1 or L % LANES == 0
    assert batch % group == 0
    tile = lambda b, t: (b, t, 0)
    per_b3 = lambda b, t: (b, 0, 0)
    per_b4 = lambda b, t: (b, 0, 0, 0)
    init3 = lambda b, t: (state_layer, b, 0, 0)
    init4 = lambda b, t: (state_layer, b, 0, 0, 0)
    kv_cols = max(WINDOW + L, 2 * LANES)
    by_seq = lambda a: a.reshape(batch, T, a.shape[-1])
    mix, C, n_out, m_out = pl.pallas_call(
        functools.partial(_mixer_kernel, group=group, L=L, steps=steps, halo_valid=halo_valid,
                          layer=layer),
        grid=(batch // group, steps),
        in_specs=[
            pl.BlockSpec(memory_space=pltpu.SMEM),
            pl.BlockSpec((group, L, Q_PAD), tile),
            pl.BlockSpec((group, L, A_KV), tile),
            pl.BlockSpec((group, L, A_KV), tile),
            pl.BlockSpec((group, L, 4 * M_W), tile),
            pl.BlockSpec((group, L, GATE_COLS), tile),
            pl.BlockSpec((None, group, WINDOW, A_KV), init3),
            pl.BlockSpec((None, group, WINDOW, A_KV), init3),
            pl.BlockSpec((None, group, M_HEADS, M_HEAD_DIM, M_HEAD_DIM), init4),
            pl.BlockSpec((None, group, M_HEADS, M_HEAD_DIM), init3),
            pl.BlockSpec((None, group, SUBLANES, LANES), init3),
            pl.BlockSpec((None, 1, M_W), lambda b, t: (layer, 0, 0)),
        ],
        out_specs=[
            pl.BlockSpec((group, L, D_MODEL), tile),
            pl.BlockSpec((group, M_HEADS, M_HEAD_DIM, M_HEAD_DIM), per_b4),
            pl.BlockSpec((group, M_HEADS, M_HEAD_DIM), per_b3),
            pl.BlockSpec((group, SUBLANES, LANES), per_b3),
        ],
        out_shape=[
            jax.ShapeDtypeStruct((batch, T, D_MODEL), BF16),
            jax.ShapeDtypeStruct((batch, M_HEADS, M_HEAD_DIM, M_HEAD_DIM), F32),
            jax.ShapeDtypeStruct((batch, M_HEADS, M_HEAD_DIM), F32),
            jax.ShapeDtypeStruct((batch, SUBLANES, LANES), F32),
        ],
        scratch_shapes=[
            pltpu.VMEM((group, WINDOW + L, LANES), BF16),
            pltpu.VMEM((group, LANES, kv_cols), BF16),
            pltpu.VMEM((group, LANES, kv_cols), BF16),
            pltpu.VMEM((group, M_HEADS, 2 * M_HEAD_DIM, M_HEAD_DIM), F32),
        ],
        compiler_params=pltpu.CompilerParams(
            dimension_semantics=("arbitrary", "arbitrary"), vmem_limit_bytes=VMEM_LIMIT),
        name="mixer",
    )(sink, by_seq(q), by_seq(k), by_seq(v), by_seq(m4), by_seq(gt), halo_k, halo_v, c0, n0, m0, mg)
    return mix.reshape(n, D_MODEL), C, n_out, m_out


def _ffn_stages(x, mix, wo_ref, g2_ref, wu_ref, wd_ref, fg_ref, store):
    x1 = x + _dot(mix, wo_ref[...])
    xn = _rms(x1, g2_ref[...]).astype(BF16)
    acc = x1
    yield
    for c in range(D_FF // FF_CHUNK):
        u = _dot(xn, wu_ref[:, c * FF_CHUNK:(c + 1) * FF_CHUNK])
        a = jnp.square(jnp.maximum(u, 0.0)).astype(BF16)
        acc = acc + _dot(a, wd_ref[c * FF_CHUNK:(c + 1) * FF_CHUNK, :])
        if c + 1 < D_FF // FF_CHUNK:
            yield
    if fg_ref is not None:
        acc = _rms(acc, fg_ref[...])
    store(acc)


def _ffn_kernel(*refs, final):
    if final:
        x_ref, mix_ref, wo_ref, g2_ref, wu_ref, wd_ref, fg_ref, out_ref = refs
    else:
        x_ref, mix_ref, wo_ref, g2_ref, wu_ref, wd_ref, out_ref = refs
        fg_ref = None

    def store(acc):
        out_ref[...] = acc

    _interleave([_ffn_stages(x_ref[...], mix_ref[...], wo_ref, g2_ref, wu_ref, wd_ref, fg_ref, store)])


def _back_kernel(*refs, group, L, steps, n_steps, layer, final):
    (sink_ref, q_ref, k_ref, v_ref, m4_ref, gt_ref, mg_ref, x_ref, wo_ref, g2_ref, wu_ref,
     wd_ref) = refs[:12]
    fg_ref = refs[12] if final else None
    y_ref, c_ref, n_ref, m_ref, kbuf, vt, vs, cst, mst, mix_z = refs[12 + final:]
    s = pl.program_id(0)
    wslot = s % 2
    rslot = 1 - wslot
    t = jnp.minimum(s, n_steps - 1) % steps
    write_ok = s < n_steps

    @pl.when(s == 0)
    def _():
        mix_z[1] = jnp.zeros(mix_z.shape[1:], mix_z.dtype)

    def mixer(i):
        one = lambda ref: ref.at[pl.ds(i, 1)]

        def init_state(put_vt):
            kbuf[i, 0:WINDOW, :] = jnp.zeros((WINDOW, LANES), BF16)
            cst[i] = jnp.zeros(cst.shape[1:], cst.dtype)
            mst[i] = jnp.zeros(mst.shape[1:], mst.dtype)

        return _mixer_stages(
            t, sink_ref, q_ref.at[i], k_ref.at[i], v_ref.at[i], m4_ref.at[i], gt_ref.at[i],
            init_state, mg_ref, mix_z.at[wslot, i], one(c_ref), one(n_ref), one(mst),
            kbuf.at[i], vt.at[i], vs.at[i], cst.at[i],
            L=L, steps=steps, halo_valid=False, layer=layer, write_ok=write_ok,
            m_out_ref=one(m_ref))

    def store(acc):
        y_ref[...] = acc.reshape(group, L, D_MODEL)

    rows = group * L
    ffn = _ffn_stages(x_ref[...].reshape(rows, D_MODEL), mix_z[rslot].reshape(rows, D_MODEL),
                      wo_ref, g2_ref, wu_ref, wd_ref, fg_ref, store)
    mixers = [mixer(i) for i in range(group)]
    every = lambda: [next(m) for m in mixers]
    quarter = lambda: [next(ffn, None) for _ in range(D_FF // FF_CHUNK // 4)]
    every()
    every()
    next(ffn)
    quarter()
    every()
    every()
    quarter()
    every()
    _interleave([ffn])
    _interleave(mixers)


def _ffn(x, mix, wo, g2, wu, wd, fg, tm, layer):
    n = x.shape[0]
    final = fg is not None
    row = lambda i: (i, 0)
    fixed = lambda i: (0, 0)
    of_layer = lambda i: (layer, 0, 0)
    single = pl.Buffered(1)
    in_specs = [
        pl.BlockSpec((tm, D_MODEL), row),
        pl.BlockSpec((tm, D_MODEL), row),
        pl.BlockSpec((None, D_MODEL, D_MODEL), of_layer, pipeline_mode=single),
        pl.BlockSpec((None, 1, D_MODEL), of_layer),
        pl.BlockSpec((None, D_MODEL, D_FF), of_layer, pipeline_mode=single),
        pl.BlockSpec((None, D_FF, D_MODEL), of_layer, pipeline_mode=single),
    ]
    args = [x, mix, wo, g2, wu, wd]
    if final:
        in_specs.append(pl.BlockSpec((1, D_MODEL), fixed))
        args.append(fg)
    return pl.pallas_call(
        functools.partial(_ffn_kernel, final=final),
        grid=(n // tm,),
        in_specs=in_specs,
        out_specs=pl.BlockSpec((tm, D_MODEL), row),
        out_shape=jax.ShapeDtypeStruct((n, D_MODEL), F32),
        compiler_params=pltpu.CompilerParams(
            dimension_semantics=("arbitrary",), vmem_limit_bytes=VMEM_LIMIT),
        name="ffn_final" if final else "ffn",
    )(*args)


def _back(x, q, k, v, m4, gt, sink, mg, wo, g2, wu, wd, fg, *, batch, group, L, layer):
    n = x.shape[0]
    T = n // batch
    steps = T // L
    n_steps = (batch // group) * steps
    assert steps > 1 and L % LANES == 0 and batch % group == 0
    final = fg is not None
    mixed = lambda s: jnp.minimum(s, n_steps - 1)
    fed = lambda s: jnp.maximum(s - 1, 0)
    mix_tile = lambda s: (mixed(s) // steps, mixed(s) % steps, 0)
    ffn_tile = lambda s: (fed(s) // steps, fed(s) % steps, 0)
    seq3 = lambda s: (mixed(s) // steps, 0, 0)
    of_layer = lambda s: (layer, 0, 0)
    single = pl.Buffered(1)
    by_seq = lambda a: a.reshape(batch, T, a.shape[-1])
    in_specs = [
        pl.BlockSpec(memory_space=pltpu.SMEM),
        pl.BlockSpec((group, L, Q_PAD), mix_tile),
        pl.BlockSpec((group, L, A_KV), mix_tile),
        pl.BlockSpec((group, L, A_KV), mix_tile),
        pl.BlockSpec((group, L, 4 * M_W), mix_tile),
        pl.BlockSpec((group, L, GATE_COLS), mix_tile),
        pl.BlockSpec((None, 1, M_W), of_layer),
        pl.BlockSpec((group, L, D_MODEL), ffn_tile),
        pl.BlockSpec((None, D_MODEL, D_MODEL), of_layer, pipeline_mode=single),
        pl.BlockSpec((None, 1, D_MODEL), of_layer),
        pl.BlockSpec((None, D_MODEL, D_FF), of_layer, pipeline_mode=single),
        pl.BlockSpec((None, D_FF, D_MODEL), of_layer, pipeline_mode=single),
    ]
    args = [sink, by_seq(q), by_seq(k), by_seq(v), by_seq(m4), by_seq(gt), mg, by_seq(x),
            wo, g2, wu, wd]
    if final:
        in_specs.append(pl.BlockSpec((1, D_MODEL), lambda s: (0, 0)))
        args.append(fg)
    kv_cols = WINDOW + L
    y, C, n_out, m_out = pl.pallas_call(
        functools.partial(_back_kernel, group=group, L=L, steps=steps, n_steps=n_steps,
                          layer=layer, final=final),
        grid=(n_steps + 1,),
        in_specs=in_specs,
        out_specs=[
            pl.BlockSpec((group, L, D_MODEL), ffn_tile),
            pl.BlockSpec((group, M_HEADS, M_HEAD_DIM, M_HEAD_DIM), lambda s: (mixed(s) // steps, 0, 0, 0)),
            pl.BlockSpec((group, M_HEADS, M_HEAD_DIM), seq3),
            pl.BlockSpec((group, SUBLANES, LANES), seq3),
        ],
        out_shape=[
            jax.ShapeDtypeStruct((batch, T, D_MODEL), F32),
            jax.ShapeDtypeStruct((batch, M_HEADS, M_HEAD_DIM, M_HEAD_DIM), F32),
            jax.ShapeDtypeStruct((batch, M_HEADS, M_HEAD_DIM), F32),
            jax.ShapeDtypeStruct((batch, SUBLANES, LANES), F32),
        ],
        scratch_shapes=[
            pltpu.VMEM((group, WINDOW + L, LANES), BF16),
            pltpu.VMEM((group, LANES, kv_cols), BF16),
            pltpu.VMEM((group, LANES, kv_cols), BF16),
            pltpu.VMEM((group, M_HEADS, 2 * M_HEAD_DIM, M_HEAD_DIM), F32),
            pltpu.VMEM((group, SUBLANES, LANES), F32),
            pltpu.VMEM((2, group, L, D_MODEL), BF16),
        ],
        compiler_params=pltpu.CompilerParams(
            dimension_semantics=("arbitrary",), vmem_limit_bytes=VMEM_LIMIT),
        name="back_final" if final else "back",
    )(*args)
    return y.reshape(n, D_MODEL), C, n_out, m_out


def _rotary_tables(pos, rows):
    inv = ROPE_THETA ** (-(jnp.arange(ROPE_HALF, dtype=F32) * 2.0 / ROPE_DIM))
    ang = pos[:, None] * inv[None, :]
    cos, sin = jnp.cos(ang), jnp.sin(ang)
    n = pos.shape[0]
    pad = jnp.zeros((n, A_HEAD_DIM - ROPE_DIM), F32)
    zeros = jnp.zeros_like(sin)
    cos_h = jnp.concatenate([cos, cos, pad + 1.0], axis=1)
    sa_h = jnp.concatenate([-sin, zeros, pad], axis=1)
    sb_h = jnp.concatenate([zeros, sin, pad], axis=1)
    rep = LANES // A_HEAD_DIM
    tile = lambda a: jnp.tile(a, (rows // n, rep))
    return tile(cos_h), tile(sa_h), tile(sb_h)


def _gate_weight_tiles(w):
    gate_pad = jnp.zeros(w.shape[:2] + (LANES - M_HEADS,), w.dtype)
    return jnp.concatenate(
        [w[..., COL_G:COL_G + M_HEADS], gate_pad, w[..., COL_G + M_HEADS:], gate_pad],
        axis=-1).astype(BF16)


def _pad_gate_bias(gb):
    pad = jnp.zeros((gb.shape[0], LANES - M_HEADS), gb.dtype)
    return jnp.concatenate([gb[:, :M_HEADS], pad, gb[:, M_HEADS:], pad], axis=1)[:, None, :]


def _m_rows(m):
    m_row = jnp.pad(m, [(0, 0)] * (m.ndim - 1) + [(0, LANES - M_HEADS)])
    return jnp.broadcast_to(m_row[..., None, :], m.shape[:-1] + (SUBLANES, LANES))


def kernel(x_prompt, x_sample, cache_k, cache_v, state_C, state_n, state_m, norm1_g, w_in, gate_b,
           attn_sink, mnorm_g, w_out, norm2_g, w_up, w_down, final_g):
    bp, T, _ = x_prompt.shape
    bs, S, _ = x_sample.shape
    depth = w_in.shape[0]
    assert T % PROMPT_MLSTM_CHUNK == 0 and S == CHUNK
    assert T % TOKEN_TILE == 0 and TOKEN_TILE % S == 0 and (bs * S) % TOKEN_TILE == 0

    tabs_p = _rotary_tables(jnp.arange(T, dtype=F32), max(T, TOKEN_TILE))
    tabs_s = _rotary_tables(jnp.arange(S, dtype=F32) + float(PAST_LEN), max(S, TOKEN_TILE))

    g1 = norm1_g[:, None, :]
    g2 = norm2_g[:, None, :]
    mg = mnorm_g[:, None, :]
    w1 = w_in.astype(BF16)
    wg = _gate_weight_tiles(w_in)
    gb = _pad_gate_bias(gate_b)
    wo = w_out.astype(BF16)
    wu = w_up.astype(BF16)
    wd = w_down.astype(BF16)
    halo_k = cache_k.reshape(depth, bs, WINDOW, A_KV)
    halo_v = cache_v.reshape(depth, bs, WINDOW, A_KV)
    m0 = _m_rows(state_m)

    yp = x_prompt.reshape(bp * T, D_MODEL)
    ys = x_sample.reshape(bs * S, D_MODEL)
    outs = {name: [] for name in ("pk", "pv", "pC", "pn", "pm", "sk", "sv", "sC", "sn", "sm")}
    fg = final_g.reshape(1, D_MODEL)
    for l in range(depth):
        last = fg if l == depth - 1 else None

        q, k, v, m4, gt = _inproj(yp, g1, w1, wg, tabs_p, gb, TOKEN_TILE, l)
        yp, C, n, m_rows = _back(yp, q, k, v, m4, gt, attn_sink, mg, wo, g2, wu, wd, last,
                                 batch=bp, group=SEQ_GROUP, L=PROMPT_MLSTM_CHUNK, layer=l)
        m = m_rows[:, 0, :M_HEADS]
        outs["pk"].append(k.reshape(bp, T, A_KV)[:, T - WINDOW:])
        outs["pv"].append(v.reshape(bp, T, A_KV)[:, T - WINDOW:])
        outs["pC"].append(C)
        outs["pn"].append(n)
        outs["pm"].append(m)

        q, k, v, m4, gt = _inproj(ys, g1, w1, wg, tabs_s, gb, TOKEN_TILE, l)
        mix, C, n, m_rows = _mixer(q, k, v, m4, gt, halo_k, halo_v, state_C, state_n, m0,
                                   attn_sink, mg, batch=bs, group=SEQ_GROUP, L=S, halo_valid=True,
                                   layer=l, state_layer=l)
        ys = _ffn(ys, mix, wo, g2, wu, wd, last, TOKEN_TILE, l)
        m = m_rows[:, 0, :M_HEADS]
        outs["sk"].append(k.reshape(bs, S, A_KV))
        outs["sv"].append(v.reshape(bs, S, A_KV))
        outs["sC"].append(C)
        outs["sn"].append(n)
        outs["sm"].append(m)

    st = lambda name: jnp.stack(outs[name])
    kv = lambda name: (lambda a: a.reshape(a.shape[:-1] + (A_KV_HEADS, A_HEAD_DIM)))(st(name))
    return (yp.reshape(bp, T, D_MODEL), ys.reshape(bs, S, D_MODEL),
            kv("pk"), kv("pv"), st("pC"), st("pn"), st("pm"),
            kv("sk"), kv("sv"), st("sC"), st("sn"), st("sm"))
```

```python
import functools

import jax
import jax.numpy as jnp
from jax import lax
from jax.experimental import pallas as pl
from jax.experimental.pallas import tpu as pltpu

D_MODEL = 1024
CHUNK = 64
A_HEADS = 8
A_KV_HEADS = 2
A_HEAD_DIM = 64
WINDOW = 128
ROPE_THETA = 500000.0
ROPE_DIM = A_HEAD_DIM // 4
ROPE_HALF = ROPE_DIM // 2
M_HEADS = 4
M_HEAD_DIM = 128
D_FF = 4 * D_MODEL
NORM_EPS = 1e-6
NEG = -1e30
LOG2E = 1.4426950408889634
PAST_LEN = 4096
A_Q = A_HEADS * A_HEAD_DIM
A_KV = A_KV_HEADS * A_HEAD_DIM
M_W = M_HEADS * M_HEAD_DIM

LANES = 128
SUBLANES = 8
Q_PAD = A_HEADS * LANES
GATE_COLS = 2 * LANES
COL_KV = A_Q
COL_M = A_Q + 2 * A_KV
COL_G = COL_M + 4 * M_W
Q_HEAD_ORDER = (0, 2, 5, 7, 1, 3, 4, 6)

TOKEN_TILE = 512
PROMPT_MLSTM_CHUNK = 256
SEQ_GROUP = 2
FF_CHUNK = 1024
VMEM_LIMIT = 56 * 1024 * 1024

BF16 = jnp.bfloat16
F32 = jnp.float32


def _dot(a, b):
    return jnp.dot(a, b, preferred_element_type=F32)


def _dot_nt(a, b):
    return lax.dot_general(a, b, (((1,), (1,)), ((), ())), preferred_element_type=F32)


def _rms(x, g):
    r = lax.rsqrt(jnp.mean(x * x, axis=-1, keepdims=True) + NORM_EPS)
    return x * r * g


def _projection_parts(x_ref, g_ref, w_ref, wg_ref, cos_ref, sa_ref, sb_ref, gb_ref,
                      q_ref, k_ref, v_ref, m4_ref, gt_ref):
    h = _rms(x_ref[...], g_ref[...]).astype(BF16)
    low = lax.broadcasted_iota(jnp.int32, (1, LANES), 1) < A_HEAD_DIM

    def rotary(z):
        left = pltpu.roll(z, LANES - ROPE_HALF, axis=1)
        right = pltpu.roll(z, ROPE_HALF, axis=1)
        return z * cos_ref[...] + left * sa_ref[...] + right * sb_ref[...]

    def attention_part():
        zq_all = _dot(h, w_ref[:, :A_Q])
        zkv = _dot(h, w_ref[:, COL_KV:COL_KV + 2 * A_KV])
        for pair in range(A_Q // LANES):
            zq = rotary(zq_all[:, pair * LANES:(pair + 1) * LANES]) * (A_HEAD_DIM ** -0.5 * LOG2E)
            zsw = pltpu.roll(zq, A_HEAD_DIM, axis=1)
            zero = jnp.zeros_like(zq)
            if pair < A_HEADS // (2 * A_KV_HEADS):
                even, odd = jnp.where(low, zq, zero), jnp.where(low, zsw, zero)
            else:
                even, odd = jnp.where(low, zero, zsw), jnp.where(low, zero, zq)
            for head, val in ((2 * pair, even), (2 * pair + 1, odd)):
                j = Q_HEAD_ORDER.index(head)
                q_ref[:, j * LANES:(j + 1) * LANES] = val.astype(BF16)
        k_ref[...] = rotary(zkv[:, :A_KV])
        v_ref[...] = zkv[:, A_KV:]

    def mlstm_qk_part():
        for j in range(2):
            zm = _dot(h, w_ref[:, COL_M + j * M_W:COL_M + (j + 1) * M_W])
            if j == 1:
                zm = zm * (M_HEAD_DIM ** -0.5)
            m4_ref[:, j * M_W:(j + 1) * M_W] = zm.astype(BF16)

    def mlstm_vo_part():
        for j in range(2, 4):
            zm = _dot(h, w_ref[:, COL_M + j * M_W:COL_M + (j + 1) * M_W])
            m4_ref[:, j * M_W:(j + 1) * M_W] = zm.astype(BF16)
        zg = _dot(h, wg_ref[...]) + gb_ref[...]
        zf = zg[:, LANES:]
        gt_ref[:, :LANES] = zg[:, :LANES]
        gt_ref[:, LANES:] = jnp.minimum(zf, 0.0) - jnp.log1p(jnp.exp(-jnp.abs(zf)))

    return attention_part, mlstm_qk_part, mlstm_vo_part


def _inproj_kernel(*refs):
    for part in _projection_parts(*refs):
        part()


def _inproj(x, g1, w, wg, tabs, gb, tm, layer):
    n = x.shape[0]
    cos, sa, sb = tabs
    nt = cos.shape[0] // tm
    row = lambda i: (i, 0)
    of_layer = lambda i: (layer, 0, 0)
    tab = lambda i: (i % nt, 0)
    return pl.pallas_call(
        _inproj_kernel,
        grid=(n // tm,),
        in_specs=[
            pl.BlockSpec((tm, D_MODEL), row),
            pl.BlockSpec((None, 1, D_MODEL), of_layer),
            pl.BlockSpec((None, D_MODEL, COL_G), of_layer),
            pl.BlockSpec((None, D_MODEL, GATE_COLS), of_layer),
            pl.BlockSpec((tm, LANES), tab),
            pl.BlockSpec((tm, LANES), tab),
            pl.BlockSpec((tm, LANES), tab),
            pl.BlockSpec((None, 1, GATE_COLS), of_layer),
        ],
        out_specs=[
            pl.BlockSpec((tm, Q_PAD), row),
            pl.BlockSpec((tm, A_KV), row),
            pl.BlockSpec((tm, A_KV), row),
            pl.BlockSpec((tm, 4 * M_W), row),
            pl.BlockSpec((tm, GATE_COLS), row),
        ],
        out_shape=[
            jax.ShapeDtypeStruct((n, Q_PAD), BF16),
            jax.ShapeDtypeStruct((n, A_KV), F32),
            jax.ShapeDtypeStruct((n, A_KV), F32),
            jax.ShapeDtypeStruct((n, 4 * M_W), BF16),
            jax.ShapeDtypeStruct((n, GATE_COLS), F32),
        ],
        compiler_params=pltpu.CompilerParams(
            dimension_semantics=("arbitrary",), vmem_limit_bytes=VMEM_LIMIT),
        name="inproj",
    )(x, g1, w, wg, cos, sa, sb, gb)


def _split3(x):
    hi = x.astype(BF16)
    r1 = x - hi.astype(F32)
    mid = r1.astype(BF16)
    lo = (r1 - mid.astype(F32)).astype(BF16)
    return hi, mid, lo


def _mixer_stages(t, sink_ref, q_ref, k_ref, v_ref, m4_ref, gt_ref, init_state, mg_ref,
                  mix_ref, c_ref, n_ref, m_ref, kbuf, vt, vs, cst,
                  *, L, steps, halo_valid, layer, write_ok=None, m_out_ref=None):
    n_chunks = L // CHUNK
    n_keys = WINDOW + CHUNK
    half = A_HEAD_DIM

    def put_vt(cols, v):
        v_t = v.T
        vt[:, cols] = v_t.astype(BF16)
        vs[:, cols] = jnp.concatenate([v_t[half:], v_t[:half]], axis=0).astype(BF16)

    @pl.when(t == 0)
    def _():
        vt[...] = jnp.zeros_like(vt)
        vs[...] = jnp.zeros_like(vs)
        init_state(put_vt)

    yield

    kbuf[WINDOW:WINDOW + L, :] = k_ref[...].astype(BF16)
    put_vt(pl.ds(WINDOW, L), v_ref[...])

    heads = range(M_HEADS)
    m_slice = lambda j, h: m4_ref[:, j * M_W + h * M_HEAD_DIM:j * M_W + (h + 1) * M_HEAD_DIM]

    scores = []
    for ci in range(n_chunks):
        r0 = ci * CHUNK
        q8 = jnp.concatenate([q_ref[r0:r0 + CHUNK, j * LANES:(j + 1) * LANES]
                              for j in range(A_HEADS)], axis=0)
        scores.append(_dot_nt(kbuf[r0:r0 + n_keys, :], q8))

    gates_i = gt_ref[:, :LANES]
    gates_f = gt_ref[:, LANES:]
    ri = lax.broadcasted_iota(jnp.int32, (L, L), 0)
    ci_ = lax.broadcasted_iota(jnp.int32, (L, L), 1)
    causal = ci_ <= ri
    b3 = _dot(causal.astype(BF16), jnp.concatenate(_split3(gates_f), axis=1))
    b = b3[:, :LANES] + b3[:, LANES:2 * LANES] + b3[:, 2 * LANES:]
    c_old = [cst[h] for h in heads]
    qk = [_dot_nt(m_slice(0, h), m_slice(1, h)) for h in heads]
    qc = [_dot_nt(m_slice(0, h), c_old[h].astype(BF16)) for h in heads]
    yield

    lane_q = lax.broadcasted_iota(jnp.int32, (1, A_HEADS * CHUNK), 1) // CHUNK
    sink_row = jnp.zeros((1, A_HEADS * CHUNK), F32)
    for j, head in enumerate(Q_HEAD_ORDER):
        sink_row = jnp.where(lane_q == j, sink_ref[layer, head] * LOG2E, sink_row)
    key_row = lax.broadcasted_iota(jnp.int32, (n_keys, A_HEADS * CHUNK), 0)
    probs, rdens = [], []
    for ci in range(n_chunks):
        s = scores[ci]
        if not halo_valid and ci < WINDOW // CHUNK:
            first_key = (t * n_chunks + ci - WINDOW // CHUNK) * CHUNK
            s = jnp.where(key_row + first_key >= 0, s, NEG)
        mx = jnp.maximum(jnp.max(s, axis=0, keepdims=True), sink_row)
        p = jnp.exp2(s - mx)
        rdens.append(1.0 / (jnp.sum(p, axis=0, keepdims=True) + jnp.exp2(sink_row - mx)))
        pad = jnp.zeros((CHUNK, A_HEADS * CHUNK), BF16)
        pb = p.astype(BF16)
        probs.append(jnp.concatenate([pb, pad] if ci % 2 == 0 else [pad, pb], axis=0))

    a = gates_i - b
    row_l = lax.broadcasted_iota(jnp.int32, (L, LANES), 0)
    cm = a
    shift = 1
    while shift < L:
        cm = jnp.maximum(cm, jnp.where(row_l >= shift, pltpu.roll(cm, shift, axis=0), NEG))
        shift *= 2
    m_prev = m_ref[0, 0:1, :]
    c = jnp.maximum(cm, m_prev)
    mt = c + b
    w_inter = jnp.exp(m_prev - c)
    e_neg_mt = jnp.exp(-mt)
    m_last = mt[L - 1:L, :]
    b_last = b[L - 1:L, :]
    delta = b_last - m_last
    decay = jnp.exp(b_last + m_prev - m_last)
    a_t = (a * LOG2E).T
    c2 = c * LOG2E
    delta2 = delta * LOG2E
    blk = min(L, LANES)
    n_blk = L // blk
    tri = (lax.broadcasted_iota(jnp.int32, (blk, blk), 1)
           <= lax.broadcasted_iota(jnp.int32, (blk, blk), 0))
    ones = jnp.ones((L, M_HEAD_DIM), BF16)
    out_gate = [jax.nn.sigmoid(m_slice(3, h).astype(F32)) for h in heads]
    s_bf, upd_lhs = [], []
    for h in heads:
        a_row = a_t[h:h + 1, :]
        row_blocks = []
        for rb in range(n_blk):
            rows = slice(rb * blk, (rb + 1) * blk)
            c_col = c2[rows, h:h + 1]
            e = []
            for cb in range(rb + 1):
                d = a_row[:, cb * blk:(cb + 1) * blk] - c_col
                e.append(jnp.exp2(jnp.where(tri, d, NEG) if cb == rb else d))
            s = (qk[h][rows, :(rb + 1) * blk] * jnp.concatenate(e, axis=1)).astype(BF16)
            if rb + 1 < n_blk:
                s = jnp.concatenate([s, jnp.zeros((blk, L - (rb + 1) * blk), BF16)], axis=1)
            row_blocks.append(s)
        s_bf.append(jnp.concatenate(row_blocks, axis=0))
        ws_row = jnp.exp2(a_row + delta2[:, h:h + 1])
        v_t = m_slice(2, h).astype(F32).T
        upd_lhs.append(jnp.concatenate(
            [v_t * ws_row, jnp.broadcast_to(ws_row, (M_HEAD_DIM, L))], axis=0).astype(BF16))

    yield

    pv = []
    for ci in range(n_chunks):
        c0 = ci * CHUNK if ci % 2 == 0 else (ci - 1) * CHUNK
        win = slice(c0, c0 + 2 * LANES)
        pv.append((_dot(vt[:, win], probs[ci][:, :2 * LANES]),
                   _dot(vs[:, win], probs[ci][:, 2 * LANES:])))
    sv = [_dot(s_bf[h], jnp.concatenate([m_slice(2, h), ones], axis=1)) for h in heads]
    c_upd = [_dot(upd_lhs[h], m_slice(1, h)) for h in heads]
    yield

    for ci in range(n_chunks):
        r0 = ci * CHUNK
        ra = pv[ci][0] * rdens[ci][:, :2 * LANES]
        rb = pv[ci][1] * rdens[ci][:, 2 * LANES:]
        o0 = jnp.concatenate([ra[:half, :LANES], rb[half:, :LANES]], axis=0).T.astype(BF16)
        o1 = jnp.concatenate([rb[:half, LANES:], ra[half:, LANES:]], axis=0).T.astype(BF16)
        mix_ref[r0:r0 + CHUNK, 0 * LANES:1 * LANES] = o0[:CHUNK]
        mix_ref[r0:r0 + CHUNK, 1 * LANES:2 * LANES] = o0[CHUNK:]
        mix_ref[r0:r0 + CHUNK, 2 * LANES:3 * LANES] = o1[:CHUNK]
        mix_ref[r0:r0 + CHUNK, 3 * LANES:4 * LANES] = o1[CHUNK:]
    for h in heads:
        tot = sv[h] + w_inter[:, h:h + 1] * qc[h]
        hh = tot[:, :M_HEAD_DIM] / jnp.maximum(jnp.abs(tot[:, M_HEAD_DIM:]), e_neg_mt[:, h:h + 1])
        hn = _rms(hh, mg_ref[:, h * M_HEAD_DIM:(h + 1) * M_HEAD_DIM])
        out = hn * out_gate[h]
        mix_ref[:, A_Q + h * M_HEAD_DIM:A_Q + (h + 1) * M_HEAD_DIM] = out.astype(BF16)
        cst[h] = decay[:, h:h + 1] * c_old[h] + c_upd[h]
    m_ref[0] = jnp.broadcast_to(m_last, (SUBLANES, LANES))
    yield

    def write_state():
        for h in heads:
            c_ref[0, h] = cst[h, :M_HEAD_DIM, :]
            n_ref[0, h:h + 1, :] = cst[h, M_HEAD_DIM:M_HEAD_DIM + 1, :]
        if m_out_ref is not None:
            m_out_ref[...] = m_ref[...]

    if steps == 1 and write_ok is None:
        write_state()
    else:
        last = t == steps - 1
        pl.when(last if write_ok is None else jnp.logical_and(last, write_ok))(write_state)

    if steps > 1:
        kbuf[0:WINDOW, :] = kbuf[L:L + WINDOW, :]
        vt[:, 0:WINDOW] = vt[:, L:L + WINDOW]
        vs[:, 0:WINDOW] = vs[:, L:L + WINDOW]


def _interleave(gens):
    done = object()
    live = list(gens)
    while live:
        live = [g for g in live if next(g, done) is not done]


def _mixer_kernel(sink_ref, q_ref, k_ref, v_ref, m4_ref, gt_ref, hk_ref, hv_ref, c0_ref, n0_ref,
                  m0_ref, mg_ref, mix_ref, c_ref, n_ref, m_ref, kbuf, vt, vs, cst,
                  *, group, **static):
    def stages(i):
        one = lambda ref: ref.at[pl.ds(i, 1)]

        def init_state(put_vt):
            kbuf[i, 0:WINDOW, :] = hk_ref[i].astype(BF16)
            put_vt(pl.ds(0, WINDOW), hv_ref[i])
            for h in range(M_HEADS):
                cst[i, h, :M_HEAD_DIM, :] = c0_ref[i, h]
                cst[i, h, M_HEAD_DIM:, :] = jnp.broadcast_to(
                    n0_ref[i, h:h + 1, :], (M_HEAD_DIM, M_HEAD_DIM))
            m_ref[i] = m0_ref[i]

        return _mixer_stages(
            pl.program_id(1), sink_ref, q_ref.at[i], k_ref.at[i], v_ref.at[i], m4_ref.at[i],
            gt_ref.at[i], init_state, mg_ref, mix_ref.at[i], one(c_ref), one(n_ref), one(m_ref),
            kbuf.at[i], vt.at[i], vs.at[i], cst.at[i], **static)

    _interleave([stages(i) for i in range(group)])


def _mixer(q, k, v, m4, gt, halo_k, halo_v, c0, n0, m0, sink, mg,
           *, batch, group, L, halo_valid, layer, state_layer):
    n = q.shape[0]
    T = n // batch
    steps = T // L
    assert steps == 1 or L % LANES == 0
    assert batch % group == 0
    tile = lambda b, t: (b, t, 0)
    per_b3 = lambda b, t: (b, 0, 0)
    per_b4 = lambda b, t: (b, 0, 0, 0)
    init3 = lambda b, t: (state_layer, b, 0, 0)
    init4 = lambda b, t: (state_layer, b, 0, 0, 0)
    kv_cols = max(WINDOW + L, 2 * LANES)
    by_seq = lambda a: a.reshape(batch, T, a.shape[-1])
    mix, C, n_out, m_out = pl.pallas_call(
        functools.partial(_mixer_kernel, group=group, L=L, steps=steps, halo_valid=halo_valid,
                          layer=layer),
        grid=(batch // group, steps),
        in_specs=[
            pl.BlockSpec(memory_space=pltpu.SMEM),
            pl.BlockSpec((group, L, Q_PAD), tile),
            pl.BlockSpec((group, L, A_KV), tile),
            pl.BlockSpec((group, L, A_KV), tile),
            pl.BlockSpec((group, L, 4 * M_W), tile),
            pl.BlockSpec((group, L, GATE_COLS), tile),
            pl.BlockSpec((None, group, WINDOW, A_KV), init3),
            pl.BlockSpec((None, group, WINDOW, A_KV), init3),
            pl.BlockSpec((None, group, M_HEADS, M_HEAD_DIM, M_HEAD_DIM), init4),
            pl.BlockSpec((None, group, M_HEADS, M_HEAD_DIM), init3),
            pl.BlockSpec((None, group, SUBLANES, LANES), init3),
            pl.BlockSpec((None, 1, M_W), lambda b, t: (layer, 0, 0)),
        ],
        out_specs=[
            pl.BlockSpec((group, L, D_MODEL), tile),
            pl.BlockSpec((group, M_HEADS, M_HEAD_DIM, M_HEAD_DIM), per_b4),
            pl.BlockSpec((group, M_HEADS, M_HEAD_DIM), per_b3),
            pl.BlockSpec((group, SUBLANES, LANES), per_b3),
        ],
        out_shape=[
            jax.ShapeDtypeStruct((batch, T, D_MODEL), BF16),
            jax.ShapeDtypeStruct((batch, M_HEADS, M_HEAD_DIM, M_HEAD_DIM), F32),
            jax.ShapeDtypeStruct((batch, M_HEADS, M_HEAD_DIM), F32),
            jax.ShapeDtypeStruct((batch, SUBLANES, LANES), F32),
        ],
        scratch_shapes=[
            pltpu.VMEM((group, WINDOW + L, LANES), BF16),
            pltpu.VMEM((group, LANES, kv_cols), BF16),
            pltpu.VMEM((group, LANES, kv_cols), BF16),
            pltpu.VMEM((group, M_HEADS, 2 * M_HEAD_DIM, M_HEAD_DIM), F32),
        ],
        compiler_params=pltpu.CompilerParams(
            dimension_semantics=("arbitrary", "arbitrary"), vmem_limit_bytes=VMEM_LIMIT),
        name="mixer",
    )(sink, by_seq(q), by_seq(k), by_seq(v), by_seq(m4), by_seq(gt), halo_k, halo_v, c0, n0, m0, mg)
    return mix.reshape(n, D_MODEL), C, n_out, m_out


def _ffn_stages(x, mix, wo_ref, g2_ref, wu_ref, wd_ref, fg_ref, store):
    x1 = x + _dot(mix, wo_ref[...])
    xn = _rms(x1, g2_ref[...]).astype(BF16)
    acc = x1
    yield
    for c in range(D_FF // FF_CHUNK):
        u = _dot(xn, wu_ref[:, c * FF_CHUNK:(c + 1) * FF_CHUNK])
        a = jnp.square(jnp.maximum(u, 0.0)).astype(BF16)
        acc = acc + _dot(a, wd_ref[c * FF_CHUNK:(c + 1) * FF_CHUNK, :])
        if c + 1 < D_FF // FF_CHUNK:
            yield
    if fg_ref is not None:
        acc = _rms(acc, fg_ref[...])
    store(acc)


def _ffn_kernel(*refs, final):
    if final:
        x_ref, mix_ref, wo_ref, g2_ref, wu_ref, wd_ref, fg_ref, out_ref = refs
    else:
        x_ref, mix_ref, wo_ref, g2_ref, wu_ref, wd_ref, out_ref = refs
        fg_ref = None

    def store(acc):
        out_ref[...] = acc

    _interleave([_ffn_stages(x_ref[...], mix_ref[...], wo_ref, g2_ref, wu_ref, wd_ref, fg_ref, store)])


def _back_kernel(*refs, group, L, steps, n_steps, layer, final):
    (sink_ref, q_ref, k_ref, v_ref, m4_ref, gt_ref, mg_ref, x_ref, wo_ref, g2_ref, wu_ref,
     wd_ref) = refs[:12]
    fg_ref = refs[12] if final else None
    y_ref, c_ref, n_ref, m_ref, kbuf, vt, vs, cst, mst, mix_z = refs[12 + final:]
    s = pl.program_id(0)
    wslot = s % 2
    rslot = 1 - wslot
    t = jnp.minimum(s, n_steps - 1) % steps
    write_ok = s < n_steps

    @pl.when(s == 0)
    def _():
        mix_z[1] = jnp.zeros(mix_z.shape[1:], mix_z.dtype)

    def mixer(i):
        one = lambda ref: ref.at[pl.ds(i, 1)]

        def init_state(put_vt):
            kbuf[i, 0:WINDOW, :] = jnp.zeros((WINDOW, LANES), BF16)
            cst[i] = jnp.zeros(cst.shape[1:], cst.dtype)
            mst[i] = jnp.zeros(mst.shape[1:], mst.dtype)

        return _mixer_stages(
            t, sink_ref, q_ref.at[i], k_ref.at[i], v_ref.at[i], m4_ref.at[i], gt_ref.at[i],
            init_state, mg_ref, mix_z.at[wslot, i], one(c_ref), one(n_ref), one(mst),
            kbuf.at[i], vt.at[i], vs.at[i], cst.at[i],
            L=L, steps=steps, halo_valid=False, layer=layer, write_ok=write_ok,
            m_out_ref=one(m_ref))

    def store(acc):
        y_ref[...] = acc.reshape(group, L, D_MODEL)

    rows = group * L
    ffn = _ffn_stages(x_ref[...].reshape(rows, D_MODEL), mix_z[rslot].reshape(rows, D_MODEL),
                      wo_ref, g2_ref, wu_ref, wd_ref, fg_ref, store)
    mixers = [mixer(i) for i in range(group)]
    every = lambda: [next(m) for m in mixers]
    quarter = lambda: [next(ffn, None) for _ in range(D_FF // FF_CHUNK // 4)]
    every()
    next(ffn)
    every()
    quarter()
    every()
    every()
    quarter()
    every()
    _interleave([ffn])
    _interleave(mixers)


def _ffn(x, mix, wo, g2, wu, wd, fg, tm, layer):
    n = x.shape[0]
    final = fg is not None
    row = lambda i: (i, 0)
    fixed = lambda i: (0, 0)
    of_layer = lambda i: (layer, 0, 0)
    single = pl.Buffered(1)
    in_specs = [
        pl.BlockSpec((tm, D_MODEL), row),
        pl.BlockSpec((tm, D_MODEL), row),
        pl.BlockSpec((None, D_MODEL, D_MODEL), of_layer, pipeline_mode=single),
        pl.BlockSpec((None, 1, D_MODEL), of_layer),
        pl.BlockSpec((None, D_MODEL, D_FF), of_layer, pipeline_mode=single),
        pl.BlockSpec((None, D_FF, D_MODEL), of_layer, pipeline_mode=single),
    ]
    args = [x, mix, wo, g2, wu, wd]
    if final:
        in_specs.append(pl.BlockSpec((1, D_MODEL), fixed))
        args.append(fg)
    return pl.pallas_call(
        functools.partial(_ffn_kernel, final=final),
        grid=(n // tm,),
        in_specs=in_specs,
        out_specs=pl.BlockSpec((tm, D_MODEL), row),
        out_shape=jax.ShapeDtypeStruct((n, D_MODEL), F32),
        compiler_params=pltpu.CompilerParams(
            dimension_semantics=("arbitrary",), vmem_limit_bytes=VMEM_LIMIT),
        name="ffn_final" if final else "ffn",
    )(*args)


def _back(x, q, k, v, m4, gt, sink, mg, wo, g2, wu, wd, fg, *, batch, group, L, layer):
    n = x.shape[0]
    T = n // batch
    steps = T // L
    n_steps = (batch // group) * steps
    assert steps > 1 and L % LANES == 0 and batch % group == 0
    final = fg is not None
    mixed = lambda s: jnp.minimum(s, n_steps - 1)
    fed = lambda s: jnp.maximum(s - 1, 0)
    mix_tile = lambda s: (mixed(s) // steps, mixed(s) % steps, 0)
    ffn_tile = lambda s: (fed(s) // steps, fed(s) % steps, 0)
    seq3 = lambda s: (mixed(s) // steps, 0, 0)
    of_layer = lambda s: (layer, 0, 0)
    single = pl.Buffered(1)
    by_seq = lambda a: a.reshape(batch, T, a.shape[-1])
    in_specs = [
        pl.BlockSpec(memory_space=pltpu.SMEM),
        pl.BlockSpec((group, L, Q_PAD), mix_tile),
        pl.BlockSpec((group, L, A_KV), mix_tile),
        pl.BlockSpec((group, L, A_KV), mix_tile),
        pl.BlockSpec((group, L, 4 * M_W), mix_tile),
        pl.BlockSpec((group, L, GATE_COLS), mix_tile),
        pl.BlockSpec((None, 1, M_W), of_layer),
        pl.BlockSpec((group, L, D_MODEL), ffn_tile),
        pl.BlockSpec((None, D_MODEL, D_MODEL), of_layer, pipeline_mode=single),
        pl.BlockSpec((None, 1, D_MODEL), of_layer),
        pl.BlockSpec((None, D_MODEL, D_FF), of_layer, pipeline_mode=single),
        pl.BlockSpec((None, D_FF, D_MODEL), of_layer, pipeline_mode=single),
    ]
    args = [sink, by_seq(q), by_seq(k), by_seq(v), by_seq(m4), by_seq(gt), mg, by_seq(x),
            wo, g2, wu, wd]
    if final:
        in_specs.append(pl.BlockSpec((1, D_MODEL), lambda s: (0, 0)))
        args.append(fg)
    kv_cols = WINDOW + L
    y, C, n_out, m_out = pl.pallas_call(
        functools.partial(_back_kernel, group=group, L=L, steps=steps, n_steps=n_steps,
                          layer=layer, final=final),
        grid=(n_steps + 1,),
        in_specs=in_specs,
        out_specs=[
            pl.BlockSpec((group, L, D_MODEL), ffn_tile),
            pl.BlockSpec((group, M_HEADS, M_HEAD_DIM, M_HEAD_DIM), lambda s: (mixed(s) // steps, 0, 0, 0)),
            pl.BlockSpec((group, M_HEADS, M_HEAD_DIM), seq3),
            pl.BlockSpec((group, SUBLANES, LANES), seq3),
        ],
        out_shape=[
            jax.ShapeDtypeStruct((batch, T, D_MODEL), F32),
            jax.ShapeDtypeStruct((batch, M_HEADS, M_HEAD_DIM, M_HEAD_DIM), F32),
            jax.ShapeDtypeStruct((batch, M_HEADS, M_HEAD_DIM), F32),
            jax.ShapeDtypeStruct((batch, SUBLANES, LANES), F32),
        ],
        scratch_shapes=[
            pltpu.VMEM((group, WINDOW + L, LANES), BF16),
            pltpu.VMEM((group, LANES, kv_cols), BF16),
            pltpu.VMEM((group, LANES, kv_cols), BF16),
            pltpu.VMEM((group, M_HEADS, 2 * M_HEAD_DIM, M_HEAD_DIM), F32),
            pltpu.VMEM((group, SUBLANES, LANES), F32),
            pltpu.VMEM((2, group, L, D_MODEL), BF16),
        ],
        compiler_params=pltpu.CompilerParams(
            dimension_semantics=("arbitrary",), vmem_limit_bytes=VMEM_LIMIT),
        name="back_final" if final else "back",
    )(*args)
    return y.reshape(n, D_MODEL), C, n_out, m_out


def _rotary_tables(pos, rows):
    inv = ROPE_THETA ** (-(jnp.arange(ROPE_HALF, dtype=F32) * 2.0 / ROPE_DIM))
    ang = pos[:, None] * inv[None, :]
    cos, sin = jnp.cos(ang), jnp.sin(ang)
    n = pos.shape[0]
    pad = jnp.zeros((n, A_HEAD_DIM - ROPE_DIM), F32)
    zeros = jnp.zeros_like(sin)
    cos_h = jnp.concatenate([cos, cos, pad + 1.0], axis=1)
    sa_h = jnp.concatenate([-sin, zeros, pad], axis=1)
    sb_h = jnp.concatenate([zeros, sin, pad], axis=1)
    rep = LANES // A_HEAD_DIM
    tile = lambda a: jnp.tile(a, (rows // n, rep))
    return tile(cos_h), tile(sa_h), tile(sb_h)


def _gate_weight_tiles(w):
    gate_pad = jnp.zeros(w.shape[:2] + (LANES - M_HEADS,), w.dtype)
    return jnp.concatenate(
        [w[..., COL_G:COL_G + M_HEADS], gate_pad, w[..., COL_G + M_HEADS:], gate_pad],
        axis=-1).astype(BF16)


def _pad_gate_bias(gb):
    pad = jnp.zeros((gb.shape[0], LANES - M_HEADS), gb.dtype)
    return jnp.concatenate([gb[:, :M_HEADS], pad, gb[:, M_HEADS:], pad], axis=1)[:, None, :]


def _m_rows(m):
    m_row = jnp.pad(m, [(0, 0)] * (m.ndim - 1) + [(0, LANES - M_HEADS)])
    return jnp.broadcast_to(m_row[..., None, :], m.shape[:-1] + (SUBLANES, LANES))


def kernel(x_prompt, x_sample, cache_k, cache_v, state_C, state_n, state_m, norm1_g, w_in, gate_b,
           attn_sink, mnorm_g, w_out, norm2_g, w_up, w_down, final_g):
    bp, T, _ = x_prompt.shape
    bs, S, _ = x_sample.shape
    depth = w_in.shape[0]
    assert T % PROMPT_MLSTM_CHUNK == 0 and S == CHUNK
    assert T % TOKEN_TILE == 0 and TOKEN_TILE % S == 0 and (bs * S) % TOKEN_TILE == 0

    tabs_p = _rotary_tables(jnp.arange(T, dtype=F32), max(T, TOKEN_TILE))
    tabs_s = _rotary_tables(jnp.arange(S, dtype=F32) + float(PAST_LEN), max(S, TOKEN_TILE))

    g1 = norm1_g[:, None, :]
    g2 = norm2_g[:, None, :]
    mg = mnorm_g[:, None, :]
    w1 = w_in.astype(BF16)
    wg = _gate_weight_tiles(w_in)
    gb = _pad_gate_bias(gate_b)
    wo = w_out.astype(BF16)
    wu = w_up.astype(BF16)
    wd = w_down.astype(BF16)
    halo_k = cache_k.reshape(depth, bs, WINDOW, A_KV)
    halo_v = cache_v.reshape(depth, bs, WINDOW, A_KV)
    m0 = _m_rows(state_m)

    yp = x_prompt.reshape(bp * T, D_MODEL)
    ys = x_sample.reshape(bs * S, D_MODEL)
    outs = {name: [] for name in ("pk", "pv", "pC", "pn", "pm", "sk", "sv", "sC", "sn", "sm")}
    fg = final_g.reshape(1, D_MODEL)
    for l in range(depth):
        last = fg if l == depth - 1 else None

        q, k, v, m4, gt = _inproj(yp, g1, w1, wg, tabs_p, gb, TOKEN_TILE, l)
        yp, C, n, m_rows = _back(yp, q, k, v, m4, gt, attn_sink, mg, wo, g2, wu, wd, last,
                                 batch=bp, group=SEQ_GROUP, L=PROMPT_MLSTM_CHUNK, layer=l)
        m = m_rows[:, 0, :M_HEADS]
        outs["pk"].append(k.reshape(bp, T, A_KV)[:, T - WINDOW:])
        outs["pv"].append(v.reshape(bp, T, A_KV)[:, T - WINDOW:])
        outs["pC"].append(C)
        outs["pn"].append(n)
        outs["pm"].append(m)

        q, k, v, m4, gt = _inproj(ys, g1, w1, wg, tabs_s, gb, TOKEN_TILE, l)
        mix, C, n, m_rows = _mixer(q, k, v, m4, gt, halo_k, halo_v, state_C, state_n, m0,
                                   attn_sink, mg, batch=bs, group=SEQ_GROUP, L=S, halo_valid=True,
                                   layer=l, state_layer=l)
        ys = _ffn(ys, mix, wo, g2, wu, wd, last, TOKEN_TILE, l)
        m = m_rows[:, 0, :M_HEADS]
        outs["sk"].append(k.reshape(bs, S, A_KV))
        outs["sv"].append(v.reshape(bs, S, A_KV))
        outs["sC"].append(C)
        outs["sn"].append(n)
        outs["sm"].append(m)

    st = lambda name: jnp.stack(outs[name])
    kv = lambda name: (lambda a: a.reshape(a.shape[:-1] + (A_KV_HEADS, A_HEAD_DIM)))(st(name))
    return (yp.reshape(bp, T, D_MODEL), ys.reshape(bs, S, D_MODEL),
            kv("pk"), kv("pv"), st("pC"), st("pn"), st("pm"),
            kv("sk"), kv("sv"), st("sC"), st("sn"), st("sm"))
```

```python
import functools

import jax
import jax.numpy as jnp
from jax import lax
from jax.experimental import pallas as pl
from jax.experimental.pallas import tpu as pltpu

D_MODEL = 1024
CHUNK = 64
A_HEADS = 8
A_KV_HEADS = 2
A_HEAD_DIM = 64
WINDOW = 128
ROPE_THETA = 500000.0
ROPE_DIM = A_HEAD_DIM // 4
ROPE_HALF = ROPE_DIM // 2
M_HEADS = 4
M_HEAD_DIM = 128
D_FF = 4 * D_MODEL
NORM_EPS = 1e-6
NEG = -1e30
LOG2E = 1.4426950408889634
PAST_LEN = 4096
A_Q = A_HEADS * A_HEAD_DIM
A_KV = A_KV_HEADS * A_HEAD_DIM
M_W = M_HEADS * M_HEAD_DIM

LANES = 128
SUBLANES = 8
Q_PAD = A_HEADS * LANES
GATE_COLS = 2 * LANES
COL_KV = A_Q
COL_M = A_Q + 2 * A_KV
COL_G = COL_M + 4 * M_W
Q_HEAD_ORDER = (0, 2, 5, 7, 1, 3, 4, 6)

TOKEN_TILE = 512
PROMPT_MLSTM_CHUNK = 256
SEQ_GROUP = 2
FF_CHUNK = 1024
VMEM_LIMIT = 56 * 1024 * 1024

BF16 = jnp.bfloat16
F32 = jnp.float32


def _dot(a, b):
    return jnp.dot(a, b, preferred_element_type=F32)


def _dot_nt(a, b):
    return lax.dot_general(a, b, (((1,), (1,)), ((), ())), preferred_element_type=F32)


def _rms(x, g):
    r = lax.rsqrt(jnp.mean(x * x, axis=-1, keepdims=True) + NORM_EPS)
    return x * r * g


def _projection_parts(x_ref, g_ref, w_ref, wg_ref, cos_ref, sa_ref, sb_ref, gb_ref,
                      q_ref, k_ref, v_ref, m4_ref, gt_ref):
    h = _rms(x_ref[...], g_ref[...]).astype(BF16)
    low = lax.broadcasted_iota(jnp.int32, (1, LANES), 1) < A_HEAD_DIM

    def rotary(z):
        left = pltpu.roll(z, LANES - ROPE_HALF, axis=1)
        right = pltpu.roll(z, ROPE_HALF, axis=1)
        return z * cos_ref[...] + left * sa_ref[...] + right * sb_ref[...]

    def attention_part():
        zq_all = _dot(h, w_ref[:, :A_Q])
        zkv = _dot(h, w_ref[:, COL_KV:COL_KV + 2 * A_KV])
        for pair in range(A_Q // LANES):
            zq = rotary(zq_all[:, pair * LANES:(pair + 1) * LANES]) * (A_HEAD_DIM ** -0.5 * LOG2E)
            zsw = pltpu.roll(zq, A_HEAD_DIM, axis=1)
            zero = jnp.zeros_like(zq)
            if pair < A_HEADS // (2 * A_KV_HEADS):
                even, odd = jnp.where(low, zq, zero), jnp.where(low, zsw, zero)
            else:
                even, odd = jnp.where(low, zero, zsw), jnp.where(low, zero, zq)
            for head, val in ((2 * pair, even), (2 * pair + 1, odd)):
                j = Q_HEAD_ORDER.index(head)
                q_ref[:, j * LANES:(j + 1) * LANES] = val.astype(BF16)
        k_ref[...] = rotary(zkv[:, :A_KV])
        v_ref[...] = zkv[:, A_KV:]

    def mlstm_qk_part():
        for j in range(2):
            zm = _dot(h, w_ref[:, COL_M + j * M_W:COL_M + (j + 1) * M_W])
            if j == 1:
                zm = zm * (M_HEAD_DIM ** -0.5)
            m4_ref[:, j * M_W:(j + 1) * M_W] = zm.astype(BF16)

    def mlstm_vo_part():
        for j in range(2, 4):
            zm = _dot(h, w_ref[:, COL_M + j * M_W:COL_M + (j + 1) * M_W])
            m4_ref[:, j * M_W:(j + 1) * M_W] = zm.astype(BF16)
        zg = _dot(h, wg_ref[...]) + gb_ref[...]
        zf = zg[:, LANES:]
        gt_ref[:, :LANES] = zg[:, :LANES]
        gt_ref[:, LANES:] = jnp.minimum(zf, 0.0) - jnp.log1p(jnp.exp(-jnp.abs(zf)))

    return attention_part, mlstm_qk_part, mlstm_vo_part


def _inproj_kernel(*refs):
    for part in _projection_parts(*refs):
        part()


def _inproj(x, g1, w, wg, tabs, gb, tm, layer):
    n = x.shape[0]
    cos, sa, sb = tabs
    nt = cos.shape[0] // tm
    row = lambda i: (i, 0)
    of_layer = lambda i: (layer, 0, 0)
    tab = lambda i: (i % nt, 0)
    return pl.pallas_call(
        _inproj_kernel,
        grid=(n // tm,),
        in_specs=[
            pl.BlockSpec((tm, D_MODEL), row),
            pl.BlockSpec((None, 1, D_MODEL), of_layer),
            pl.BlockSpec((None, D_MODEL, COL_G), of_layer),
            pl.BlockSpec((None, D_MODEL, GATE_COLS), of_layer),
            pl.BlockSpec((tm, LANES), tab),
            pl.BlockSpec((tm, LANES), tab),
            pl.BlockSpec((tm, LANES), tab),
            pl.BlockSpec((None, 1, GATE_COLS), of_layer),
        ],
        out_specs=[
            pl.BlockSpec((tm, Q_PAD), row),
            pl.BlockSpec((tm, A_KV), row),
            pl.BlockSpec((tm, A_KV), row),
            pl.BlockSpec((tm, 4 * M_W), row),
            pl.BlockSpec((tm, GATE_COLS), row),
        ],
        out_shape=[
            jax.ShapeDtypeStruct((n, Q_PAD), BF16),
            jax.ShapeDtypeStruct((n, A_KV), F32),
            jax.ShapeDtypeStruct((n, A_KV), F32),
            jax.ShapeDtypeStruct((n, 4 * M_W), BF16),
            jax.ShapeDtypeStruct((n, GATE_COLS), F32),
        ],
        compiler_params=pltpu.CompilerParams(
            dimension_semantics=("arbitrary",), vmem_limit_bytes=VMEM_LIMIT),
        name="inproj",
    )(x, g1, w, wg, cos, sa, sb, gb)


def _split3(x):
    hi = x.astype(BF16)
    r1 = x - hi.astype(F32)
    mid = r1.astype(BF16)
    lo = (r1 - mid.astype(F32)).astype(BF16)
    return hi, mid, lo


def _mixer_stages(t, sink_ref, q_ref, k_ref, v_ref, m4_ref, gt_ref, init_state, mg_ref,
                  mix_ref, c_ref, n_ref, m_ref, kbuf, vt, vs, cst,
                  *, L, steps, halo_valid, layer, m_out_ref=None):
    n_chunks = L // CHUNK
    n_keys = WINDOW + CHUNK
    half = A_HEAD_DIM

    def put_vt(cols, v):
        v_t = v.T
        vt[:, cols] = v_t.astype(BF16)
        vs[:, cols] = jnp.concatenate([v_t[half:], v_t[:half]], axis=0).astype(BF16)

    @pl.when(t == 0)
    def _():
        vt[...] = jnp.zeros_like(vt)
        vs[...] = jnp.zeros_like(vs)
        init_state(put_vt)

    yield

    kbuf[WINDOW:WINDOW + L, :] = k_ref[...].astype(BF16)
    put_vt(pl.ds(WINDOW, L), v_ref[...])

    heads = range(M_HEADS)
    m_slice = lambda j, h: m4_ref[:, j * M_W + h * M_HEAD_DIM:j * M_W + (h + 1) * M_HEAD_DIM]

    scores = []
    for ci in range(n_chunks):
        r0 = ci * CHUNK
        q8 = jnp.concatenate([q_ref[r0:r0 + CHUNK, j * LANES:(j + 1) * LANES]
                              for j in range(A_HEADS)], axis=0)
        scores.append(_dot_nt(kbuf[r0:r0 + n_keys, :], q8))

    gates_i = gt_ref[:, :LANES]
    gates_f = gt_ref[:, LANES:]
    ri = lax.broadcasted_iota(jnp.int32, (L, L), 0)
    ci_ = lax.broadcasted_iota(jnp.int32, (L, L), 1)
    causal = ci_ <= ri
    b3 = _dot(causal.astype(BF16), jnp.concatenate(_split3(gates_f), axis=1))
    b = b3[:, :LANES] + b3[:, LANES:2 * LANES] + b3[:, 2 * LANES:]
    c_old = [cst[h] for h in heads]
    qk = [_dot_nt(m_slice(0, h), m_slice(1, h)) for h in heads]
    qc = [_dot_nt(m_slice(0, h), c_old[h].astype(BF16)) for h in heads]
    yield

    lane_q = lax.broadcasted_iota(jnp.int32, (1, A_HEADS * CHUNK), 1) // CHUNK
    sink_row = jnp.zeros((1, A_HEADS * CHUNK), F32)
    for j, head in enumerate(Q_HEAD_ORDER):
        sink_row = jnp.where(lane_q == j, sink_ref[layer, head] * LOG2E, sink_row)
    key_row = lax.broadcasted_iota(jnp.int32, (n_keys, A_HEADS * CHUNK), 0)
    probs, rdens = [], []
    for ci in range(n_chunks):
        s = scores[ci]
        if not halo_valid and ci < WINDOW // CHUNK:
            first_key = (t * n_chunks + ci - WINDOW // CHUNK) * CHUNK
            s = jnp.where(key_row + first_key >= 0, s, NEG)
        mx = jnp.maximum(jnp.max(s, axis=0, keepdims=True), sink_row)
        p = jnp.exp2(s - mx)
        rdens.append(1.0 / (jnp.sum(p, axis=0, keepdims=True) + jnp.exp2(sink_row - mx)))
        pad = jnp.zeros((CHUNK, A_HEADS * CHUNK), BF16)
        pb = p.astype(BF16)
        probs.append(jnp.concatenate([pb, pad] if ci % 2 == 0 else [pad, pb], axis=0))

    a = gates_i - b
    row_l = lax.broadcasted_iota(jnp.int32, (L, LANES), 0)
    cm = a
    shift = 1
    while shift < L:
        cm = jnp.maximum(cm, jnp.where(row_l >= shift, pltpu.roll(cm, shift, axis=0), NEG))
        shift *= 2
    m_prev = m_ref[0, 0:1, :]
    c = jnp.maximum(cm, m_prev)
    mt = c + b
    w_inter = jnp.exp(m_prev - c)
    e_neg_mt = jnp.exp(-mt)
    m_last = mt[L - 1:L, :]
    b_last = b[L - 1:L, :]
    delta = b_last - m_last
    decay = jnp.exp(b_last + m_prev - m_last)
    a_t = (a * LOG2E).T
    c2 = c * LOG2E
    delta2 = delta * LOG2E
    blk = min(L, LANES)
    n_blk = L // blk
    tri = (lax.broadcasted_iota(jnp.int32, (blk, blk), 1)
           <= lax.broadcasted_iota(jnp.int32, (blk, blk), 0))
    ones = jnp.ones((L, M_HEAD_DIM), BF16)
    out_gate = [jax.nn.sigmoid(m_slice(3, h).astype(F32)) for h in heads]
    s_bf, upd_lhs = [], []
    for h in heads:
        a_row = a_t[h:h + 1, :]
        row_blocks = []
        for rb in range(n_blk):
            rows = slice(rb * blk, (rb + 1) * blk)
            c_col = c2[rows, h:h + 1]
            e = []
            for cb in range(rb + 1):
                d = a_row[:, cb * blk:(cb + 1) * blk] - c_col
                e.append(jnp.exp2(jnp.where(tri, d, NEG) if cb == rb else d))
            s = (qk[h][rows, :(rb + 1) * blk] * jnp.concatenate(e, axis=1)).astype(BF16)
            if rb + 1 < n_blk:
                s = jnp.concatenate([s, jnp.zeros((blk, L - (rb + 1) * blk), BF16)], axis=1)
            row_blocks.append(s)
        s_bf.append(jnp.concatenate(row_blocks, axis=0))
        ws_row = jnp.exp2(a_row + delta2[:, h:h + 1])
        v_t = m_slice(2, h).astype(F32).T
        upd_lhs.append(jnp.concatenate(
            [v_t * ws_row, jnp.broadcast_to(ws_row, (M_HEAD_DIM, L))], axis=0).astype(BF16))

    yield

    pv = []
    for ci in range(n_chunks):
        c0 = ci * CHUNK if ci % 2 == 0 else (ci - 1) * CHUNK
        win = slice(c0, c0 + 2 * LANES)
        pv.append((_dot(vt[:, win], probs[ci][:, :2 * LANES]),
                   _dot(vs[:, win], probs[ci][:, 2 * LANES:])))
    sv = [_dot(s_bf[h], jnp.concatenate([m_slice(2, h), ones], axis=1)) for h in heads]
    c_upd = [_dot(upd_lhs[h], m_slice(1, h)) for h in heads]
    yield

    for ci in range(n_chunks):
        r0 = ci * CHUNK
        ra = pv[ci][0] * rdens[ci][:, :2 * LANES]
        rb = pv[ci][1] * rdens[ci][:, 2 * LANES:]
        o0 = jnp.concatenate([ra[:half, :LANES], rb[half:, :LANES]], axis=0).T.astype(BF16)
        o1 = jnp.concatenate([rb[:half, LANES:], ra[half:, LANES:]], axis=0).T.astype(BF16)
        mix_ref[r0:r0 + CHUNK, 0 * LANES:1 * LANES] = o0[:CHUNK]
        mix_ref[r0:r0 + CHUNK, 1 * LANES:2 * LANES] = o0[CHUNK:]
        mix_ref[r0:r0 + CHUNK, 2 * LANES:3 * LANES] = o1[:CHUNK]
        mix_ref[r0:r0 + CHUNK, 3 * LANES:4 * LANES] = o1[CHUNK:]
    for h in heads:
        tot = sv[h] + w_inter[:, h:h + 1] * qc[h]
        hh = tot[:, :M_HEAD_DIM] / jnp.maximum(jnp.abs(tot[:, M_HEAD_DIM:]), e_neg_mt[:, h:h + 1])
        hn = _rms(hh, mg_ref[:, h * M_HEAD_DIM:(h + 1) * M_HEAD_DIM])
        out = hn * out_gate[h]
        mix_ref[:, A_Q + h * M_HEAD_DIM:A_Q + (h + 1) * M_HEAD_DIM] = out.astype(BF16)
        cst[h] = decay[:, h:h + 1] * c_old[h] + c_upd[h]
    m_ref[0] = jnp.broadcast_to(m_last, (SUBLANES, LANES))
    yield

    def write_state():
        for h in heads:
            c_ref[0, h] = cst[h, :M_HEAD_DIM, :]
            n_ref[0, h:h + 1, :] = cst[h, M_HEAD_DIM:M_HEAD_DIM + 1, :]
        if m_out_ref is not None:
            m_out_ref[...] = m_ref[...]

    if steps == 1:
        write_state()
    else:
        pl.when(t == steps - 1)(write_state)

    if steps > 1:
        kbuf[0:WINDOW, :] = kbuf[L:L + WINDOW, :]
        vt[:, 0:WINDOW] = vt[:, L:L + WINDOW]
        vs[:, 0:WINDOW] = vs[:, L:L + WINDOW]


def _interleave(gens):
    done = object()
    live = list(gens)
    while live:
        live = [g for g in live if next(g, done) is not done]


def _mixer_kernel(sink_ref, q_ref, k_ref, v_ref, m4_ref, gt_ref, hk_ref, hv_ref, c0_ref, n0_ref,
                  m0_ref, mg_ref, mix_ref, c_ref, n_ref, m_ref, kbuf, vt, vs, cst,
                  *, group, **static):
    def stages(i):
        one = lambda ref: ref.at[pl.ds(i, 1)]

        def init_state(put_vt):
            kbuf[i, 0:WINDOW, :] = hk_ref[i].astype(BF16)
            put_vt(pl.ds(0, WINDOW), hv_ref[i])
            for h in range(M_HEADS):
                cst[i, h, :M_HEAD_DIM, :] = c0_ref[i, h]
                cst[i, h, M_HEAD_DIM:, :] = jnp.broadcast_to(
                    n0_ref[i, h:h + 1, :], (M_HEAD_DIM, M_HEAD_DIM))
            m_ref[i] = m0_ref[i]

        return _mixer_stages(
            pl.program_id(1), sink_ref, q_ref.at[i], k_ref.at[i], v_ref.at[i], m4_ref.at[i],
            gt_ref.at[i], init_state, mg_ref, mix_ref.at[i], one(c_ref), one(n_ref), one(m_ref),
            kbuf.at[i], vt.at[i], vs.at[i], cst.at[i], **static)

    _interleave([stages(i) for i in range(group)])


def _mixer(q, k, v, m4, gt, halo_k, halo_v, c0, n0, m0, sink, mg,
           *, batch, group, L, halo_valid, layer, state_layer):
    n = q.shape[0]
    T = n // batch
    steps = T // L
    assert steps == 1 or L % LANES == 0
    assert batch % group == 0
    tile = lambda b, t: (b, t, 0)
    per_b3 = lambda b, t: (b, 0, 0)
    per_b4 = lambda b, t: (b, 0, 0, 0)
    init3 = lambda b, t: (state_layer, b, 0, 0)
    init4 = lambda b, t: (state_layer, b, 0, 0, 0)
    kv_cols = max(WINDOW + L, 2 * LANES)
    by_seq = lambda a: a.reshape(batch, T, a.shape[-1])
    mix, C, n_out, m_out = pl.pallas_call(
        functools.partial(_mixer_kernel, group=group, L=L, steps=steps, halo_valid=halo_valid,
                          layer=layer),
        grid=(batch // group, steps),
        in_specs=[
            pl.BlockSpec(memory_space=pltpu.SMEM),
            pl.BlockSpec((group, L, Q_PAD), tile),
            pl.BlockSpec((group, L, A_KV), tile),
            pl.BlockSpec((group, L, A_KV), tile),
            pl.BlockSpec((group, L, 4 * M_W), tile),
            pl.BlockSpec((group, L, GATE_COLS), tile),
            pl.BlockSpec((None, group, WINDOW, A_KV), init3),
            pl.BlockSpec((None, group, WINDOW, A_KV), init3),
            pl.BlockSpec((None, group, M_HEADS, M_HEAD_DIM, M_HEAD_DIM), init4),
            pl.BlockSpec((None, group, M_HEADS, M_HEAD_DIM), init3),
            pl.BlockSpec((None, group, SUBLANES, LANES), init3),
            pl.BlockSpec((None, 1, M_W), lambda b, t: (layer, 0, 0)),
        ],
        out_specs=[
            pl.BlockSpec((group, L, D_MODEL), tile),
            pl.BlockSpec((group, M_HEADS, M_HEAD_DIM, M_HEAD_DIM), per_b4),
            pl.BlockSpec((group, M_HEADS, M_HEAD_DIM), per_b3),
            pl.BlockSpec((group, SUBLANES, LANES), per_b3),
        ],
        out_shape=[
            jax.ShapeDtypeStruct((batch, T, D_MODEL), BF16),
            jax.ShapeDtypeStruct((batch, M_HEADS, M_HEAD_DIM, M_HEAD_DIM), F32),
            jax.ShapeDtypeStruct((batch, M_HEADS, M_HEAD_DIM), F32),
            jax.ShapeDtypeStruct((batch, SUBLANES, LANES), F32),
        ],
        scratch_shapes=[
            pltpu.VMEM((group, WINDOW + L, LANES), BF16),
            pltpu.VMEM((group, LANES, kv_cols), BF16),
            pltpu.VMEM((group, LANES, kv_cols), BF16),
            pltpu.VMEM((group, M_HEADS, 2 * M_HEAD_DIM, M_HEAD_DIM), F32),
        ],
        compiler_params=pltpu.CompilerParams(
            dimension_semantics=("arbitrary", "arbitrary"), vmem_limit_bytes=VMEM_LIMIT),
        name="mixer",
    )(sink, by_seq(q), by_seq(k), by_seq(v), by_seq(m4), by_seq(gt), halo_k, halo_v, c0, n0, m0, mg)
    return mix.reshape(n, D_MODEL), C, n_out, m_out


def _ffn_stages(x, mix, wo_ref, g2_ref, wu_ref, wd_ref, fg_ref, store):
    x1 = x + _dot(mix, wo_ref[...])
    xn = _rms(x1, g2_ref[...]).astype(BF16)
    acc = x1
    yield
    for c in range(D_FF // FF_CHUNK):
        u = _dot(xn, wu_ref[:, c * FF_CHUNK:(c + 1) * FF_CHUNK])
        a = jnp.square(jnp.maximum(u, 0.0)).astype(BF16)
        acc = acc + _dot(a, wd_ref[c * FF_CHUNK:(c + 1) * FF_CHUNK, :])
        if c + 1 < D_FF // FF_CHUNK:
            yield
    if fg_ref is not None:
        acc = _rms(acc, fg_ref[...])
    store(acc)


def _ffn_kernel(*refs, final):
    if final:
        x_ref, mix_ref, wo_ref, g2_ref, wu_ref, wd_ref, fg_ref, out_ref = refs
    else:
        x_ref, mix_ref, wo_ref, g2_ref, wu_ref, wd_ref, out_ref = refs
        fg_ref = None

    def store(acc):
        out_ref[...] = acc

    _interleave([_ffn_stages(x_ref[...], mix_ref[...], wo_ref, g2_ref, wu_ref, wd_ref, fg_ref, store)])


def _back_kernel(*refs, group, L, steps, n_steps, layer, final):
    (sink_ref, q_ref, k_ref, v_ref, m4_ref, gt_ref, mg_ref, x_ref, wo_ref, g2_ref, wu_ref,
     wd_ref) = refs[:12]
    fg_ref = refs[12] if final else None
    y_ref, c_ref, n_ref, m_ref, kbuf, vt, vs, cst, mst, mix_z = refs[12 + final:]
    s = pl.program_id(0)
    wslot = s % 2
    rslot = 1 - wslot
    t = s % steps

    def mixer(i):
        one = lambda ref: ref.at[pl.ds(i, 1)]

        def init_state(put_vt):
            kbuf[i, 0:WINDOW, :] = jnp.zeros((WINDOW, LANES), BF16)
            cst[i] = jnp.zeros(cst.shape[1:], cst.dtype)
            mst[i] = jnp.zeros(mst.shape[1:], mst.dtype)

        return _mixer_stages(
            t, sink_ref, q_ref.at[i], k_ref.at[i], v_ref.at[i], m4_ref.at[i], gt_ref.at[i],
            init_state, mg_ref, mix_z.at[wslot, i], one(c_ref), one(n_ref), one(mst),
            kbuf.at[i], vt.at[i], vs.at[i], cst.at[i],
            L=L, steps=steps, halo_valid=False, layer=layer, m_out_ref=one(m_ref))

    def store(acc):
        y_ref[...] = acc.reshape(group, L, D_MODEL)

    def body(mix, feed):
        rows = group * L
        ffn = iter(()) if not feed else _ffn_stages(
            x_ref[...].reshape(rows, D_MODEL), mix_z[rslot].reshape(rows, D_MODEL),
            wo_ref, g2_ref, wu_ref, wd_ref, fg_ref, store)
        mixers = [mixer(i) for i in range(group)] if mix else []
        every = lambda: [next(m) for m in mixers]
        quarter = lambda: [next(ffn, None) for _ in range(D_FF // FF_CHUNK // 4)]
        every()
        next(ffn, None)
        every()
        quarter()
        every()
        every()
        quarter()
        every()
        _interleave([ffn])
        _interleave(mixers)

    pl.when(s == 0)(lambda: body(True, False))
    pl.when(jnp.logical_and(s > 0, s < n_steps))(lambda: body(True, True))
    pl.when(s == n_steps)(lambda: body(False, True))


def _ffn(x, mix, wo, g2, wu, wd, fg, tm, layer):
    n = x.shape[0]
    final = fg is not None
    row = lambda i: (i, 0)
    fixed = lambda i: (0, 0)
    of_layer = lambda i: (layer, 0, 0)
    single = pl.Buffered(1)
    in_specs = [
        pl.BlockSpec((tm, D_MODEL), row),
        pl.BlockSpec((tm, D_MODEL), row),
        pl.BlockSpec((None, D_MODEL, D_MODEL), of_layer, pipeline_mode=single),
        pl.BlockSpec((None, 1, D_MODEL), of_layer),
        pl.BlockSpec((None, D_MODEL, D_FF), of_layer, pipeline_mode=single),
        pl.BlockSpec((None, D_FF, D_MODEL), of_layer, pipeline_mode=single),
    ]
    args = [x, mix, wo, g2, wu, wd]
    if final:
        in_specs.append(pl.BlockSpec((1, D_MODEL), fixed))
        args.append(fg)
    return pl.pallas_call(
        functools.partial(_ffn_kernel, final=final),
        grid=(n // tm,),
        in_specs=in_specs,
        out_specs=pl.BlockSpec((tm, D_MODEL), row),
        out_shape=jax.ShapeDtypeStruct((n, D_MODEL), F32),
        compiler_params=pltpu.CompilerParams(
            dimension_semantics=("arbitrary",), vmem_limit_bytes=VMEM_LIMIT),
        name="ffn_final" if final else "ffn",
    )(*args)


def _back(x, q, k, v, m4, gt, sink, mg, wo, g2, wu, wd, fg, *, batch, group, L, layer):
    n = x.shape[0]
    T = n // batch
    steps = T // L
    n_steps = (batch // group) * steps
    assert steps > 1 and L % LANES == 0 and batch % group == 0
    final = fg is not None
    mixed = lambda s: jnp.minimum(s, n_steps - 1)
    fed = lambda s: jnp.maximum(s - 1, 0)
    mix_tile = lambda s: (mixed(s) // steps, mixed(s) % steps, 0)
    ffn_tile = lambda s: (fed(s) // steps, fed(s) % steps, 0)
    seq3 = lambda s: (mixed(s) // steps, 0, 0)
    of_layer = lambda s: (layer, 0, 0)
    single = pl.Buffered(1)
    by_seq = lambda a: a.reshape(batch, T, a.shape[-1])
    in_specs = [
        pl.BlockSpec(memory_space=pltpu.SMEM),
        pl.BlockSpec((group, L, Q_PAD), mix_tile),
        pl.BlockSpec((group, L, A_KV), mix_tile),
        pl.BlockSpec((group, L, A_KV), mix_tile),
        pl.BlockSpec((group, L, 4 * M_W), mix_tile),
        pl.BlockSpec((group, L, GATE_COLS), mix_tile),
        pl.BlockSpec((None, 1, M_W), of_layer),
        pl.BlockSpec((group, L, D_MODEL), ffn_tile),
        pl.BlockSpec((None, D_MODEL, D_MODEL), of_layer, pipeline_mode=single),
        pl.BlockSpec((None, 1, D_MODEL), of_layer),
        pl.BlockSpec((None, D_MODEL, D_FF), of_layer, pipeline_mode=single),
        pl.BlockSpec((None, D_FF, D_MODEL), of_layer, pipeline_mode=single),
    ]
    args = [sink, by_seq(q), by_seq(k), by_seq(v), by_seq(m4), by_seq(gt), mg, by_seq(x),
            wo, g2, wu, wd]
    if final:
        in_specs.append(pl.BlockSpec((1, D_MODEL), lambda s: (0, 0)))
        args.append(fg)
    kv_cols = WINDOW + L
    y, C, n_out, m_out = pl.pallas_call(
        functools.partial(_back_kernel, group=group, L=L, steps=steps, n_steps=n_steps,
                          layer=layer, final=final),
        grid=(n_steps + 1,),
        in_specs=in_specs,
        out_specs=[
            pl.BlockSpec((group, L, D_MODEL), ffn_tile),
            pl.BlockSpec((group, M_HEADS, M_HEAD_DIM, M_HEAD_DIM), lambda s: (mixed(s) // steps, 0, 0, 0)),
            pl.BlockSpec((group, M_HEADS, M_HEAD_DIM), seq3),
            pl.BlockSpec((group, SUBLANES, LANES), seq3),
        ],
        out_shape=[
            jax.ShapeDtypeStruct((batch, T, D_MODEL), F32),
            jax.ShapeDtypeStruct((batch, M_HEADS, M_HEAD_DIM, M_HEAD_DIM), F32),
            jax.ShapeDtypeStruct((batch, M_HEADS, M_HEAD_DIM), F32),
            jax.ShapeDtypeStruct((batch, SUBLANES, LANES), F32),
        ],
        scratch_shapes=[
            pltpu.VMEM((group, WINDOW + L, LANES), BF16),
            pltpu.VMEM((group, LANES, kv_cols), BF16),
            pltpu.VMEM((group, LANES, kv_cols), BF16),
            pltpu.VMEM((group, M_HEADS, 2 * M_HEAD_DIM, M_HEAD_DIM), F32),
            pltpu.VMEM((group, SUBLANES, LANES), F32),
            pltpu.VMEM((2, group, L, D_MODEL), BF16),
        ],
        compiler_params=pltpu.CompilerParams(
            dimension_semantics=("arbitrary",), vmem_limit_bytes=VMEM_LIMIT),
        name="back_final" if final else "back",
    )(*args)
    return y.reshape(n, D_MODEL), C, n_out, m_out


def _rotary_tables(pos, rows):
    inv = ROPE_THETA ** (-(jnp.arange(ROPE_HALF, dtype=F32) * 2.0 / ROPE_DIM))
    ang = pos[:, None] * inv[None, :]
    cos, sin = jnp.cos(ang), jnp.sin(ang)
    n = pos.shape[0]
    pad = jnp.zeros((n, A_HEAD_DIM - ROPE_DIM), F32)
    zeros = jnp.zeros_like(sin)
    cos_h = jnp.concatenate([cos, cos, pad + 1.0], axis=1)
    sa_h = jnp.concatenate([-sin, zeros, pad], axis=1)
    sb_h = jnp.concatenate([zeros, sin, pad], axis=1)
    rep = LANES // A_HEAD_DIM
    tile = lambda a: jnp.tile(a, (rows // n, rep))
    return tile(cos_h), tile(sa_h), tile(sb_h)


def _gate_weight_tiles(w):
    gate_pad = jnp.zeros(w.shape[:2] + (LANES - M_HEADS,), w.dtype)
    return jnp.concatenate(
        [w[..., COL_G:COL_G + M_HEADS], gate_pad, w[..., COL_G + M_HEADS:], gate_pad],
        axis=-1).astype(BF16)


def _pad_gate_bias(gb):
    pad = jnp.zeros((gb.shape[0], LANES - M_HEADS), gb.dtype)
    return jnp.concatenate([gb[:, :M_HEADS], pad, gb[:, M_HEADS:], pad], axis=1)[:, None, :]


def _m_rows(m):
    m_row = jnp.pad(m, [(0, 0)] * (m.ndim - 1) + [(0, LANES - M_HEADS)])
    return jnp.broadcast_to(m_row[..., None, :], m.shape[:-1] + (SUBLANES, LANES))


def kernel(x_prompt, x_sample, cache_k, cache_v, state_C, state_n, state_m, norm1_g, w_in, gate_b,
           attn_sink, mnorm_g, w_out, norm2_g, w_up, w_down, final_g):
    bp, T, _ = x_prompt.shape
    bs, S, _ = x_sample.shape
    depth = w_in.shape[0]
    assert T % PROMPT_MLSTM_CHUNK == 0 and S == CHUNK
    assert T % TOKEN_TILE == 0 and TOKEN_TILE % S == 0 and (bs * S) % TOKEN_TILE == 0

    tabs_p = _rotary_tables(jnp.arange(T, dtype=F32), max(T, TOKEN_TILE))
    tabs_s = _rotary_tables(jnp.arange(S, dtype=F32) + float(PAST_LEN), max(S, TOKEN_TILE))

    g1 = norm1_g[:, None, :]
    g2 = norm2_g[:, None, :]
    mg = mnorm_g[:, None, :]
    w1 = w_in.astype(BF16)
    wg = _gate_weight_tiles(w_in)
    gb = _pad_gate_bias(gate_b)
    wo = w_out.astype(BF16)
    wu = w_up.astype(BF16)
    wd = w_down.astype(BF16)
    halo_k = cache_k.reshape(depth, bs, WINDOW, A_KV)
    halo_v = cache_v.reshape(depth, bs, WINDOW, A_KV)
    m0 = _m_rows(state_m)

    yp = x_prompt.reshape(bp * T, D_MODEL)
    ys = x_sample.reshape(bs * S, D_MODEL)
    outs = {name: [] for name in ("pk", "pv", "pC", "pn", "pm", "sk", "sv", "sC", "sn", "sm")}
    fg = final_g.reshape(1, D_MODEL)
    for l in range(depth):
        last = fg if l == depth - 1 else None

        q, k, v, m4, gt = _inproj(yp, g1, w1, wg, tabs_p, gb, TOKEN_TILE, l)
        yp, C, n, m_rows = _back(yp, q, k, v, m4, gt, attn_sink, mg, wo, g2, wu, wd, last,
                                 batch=bp, group=SEQ_GROUP, L=PROMPT_MLSTM_CHUNK, layer=l)
        m = m_rows[:, 0, :M_HEADS]
        outs["pk"].append(k.reshape(bp, T, A_KV)[:, T - WINDOW:])
        outs["pv"].append(v.reshape(bp, T, A_KV)[:, T - WINDOW:])
        outs["pC"].append(C)
        outs["pn"].append(n)
        outs["pm"].append(m)

        q, k, v, m4, gt = _inproj(ys, g1, w1, wg, tabs_s, gb, TOKEN_TILE, l)
        mix, C, n, m_rows = _mixer(q, k, v, m4, gt, halo_k, halo_v, state_C, state_n, m0,
                                   attn_sink, mg, batch=bs, group=SEQ_GROUP, L=S, halo_valid=True,
                                   layer=l, state_layer=l)
        ys = _ffn(ys, mix, wo, g2, wu, wd, last, TOKEN_TILE, l)
        m = m_rows[:, 0, :M_HEADS]
        outs["sk"].append(k.reshape(bs, S, A_KV))
        outs["sv"].append(v.reshape(bs, S, A_KV))
        outs["sC"].append(C)
        outs["sn"].append(n)
        outs["sm"].append(m)

    st = lambda name: jnp.stack(outs[name])
    kv = lambda name: (lambda a: a.reshape(a.shape[:-1] + (A_KV_HEADS, A_HEAD_DIM)))(st(name))
    return (yp.reshape(bp, T, D_MODEL), ys.reshape(bs, S, D_MODEL),
            kv("pk"), kv("pv"), st("pC"), st("pn"), st("pm"),
            kv("sk"), kv("sv"), st("sC"), st("sn"), st("sm"))
```

```python
import functools

import jax
import jax.numpy as jnp
from jax import lax
from jax.experimental import pallas as pl
from jax.experimental.pallas import tpu as pltpu

D_MODEL = 1024
CHUNK = 64
A_HEADS = 8
A_KV_HEADS = 2
A_HEAD_DIM = 64
WINDOW = 128
ROPE_THETA = 500000.0
ROPE_DIM = A_HEAD_DIM // 4
ROPE_HALF = ROPE_DIM // 2
M_HEADS = 4
M_HEAD_DIM = 128
D_FF = 4 * D_MODEL
NORM_EPS = 1e-6
NEG = -1e30
LOG2E = 1.4426950408889634
PAST_LEN = 4096
A_Q = A_HEADS * A_HEAD_DIM
A_KV = A_KV_HEADS * A_HEAD_DIM
M_W = M_HEADS * M_HEAD_DIM

LANES = 128
SUBLANES = 8
Q_PAD = A_HEADS * LANES
GATE_COLS = 2 * LANES
COL_KV = A_Q
COL_M = A_Q + 2 * A_KV
COL_G = COL_M + 4 * M_W
Q_HEAD_ORDER = (0, 2, 5, 7, 1, 3, 4, 6)

TOKEN_TILE = 512
PROJ_TILE = 1024
PROMPT_MLSTM_CHUNK = 256
SEQ_GROUP = 2
FF_CHUNK = 1024
VMEM_LIMIT = 56 * 1024 * 1024

BF16 = jnp.bfloat16
F32 = jnp.float32


def _dot(a, b):
    return jnp.dot(a, b, preferred_element_type=F32)


def _dot_nt(a, b):
    return lax.dot_general(a, b, (((1,), (1,)), ((), ())), preferred_element_type=F32)


def _rms(x, g):
    r = lax.rsqrt(jnp.mean(x * x, axis=-1, keepdims=True) + NORM_EPS)
    return x * r * g


def _projection_parts(x_ref, g_ref, w_ref, wg_ref, cos_ref, sa_ref, sb_ref, gb_ref,
                      q_ref, k_ref, v_ref, m4_ref, gt_ref):
    h = _rms(x_ref[...], g_ref[...]).astype(BF16)
    low = lax.broadcasted_iota(jnp.int32, (1, LANES), 1) < A_HEAD_DIM

    def rotary(z):
        left = pltpu.roll(z, LANES - ROPE_HALF, axis=1)
        right = pltpu.roll(z, ROPE_HALF, axis=1)
        return z * cos_ref[...] + left * sa_ref[...] + right * sb_ref[...]

    def attention_part():
        zq_all = _dot(h, w_ref[:, :A_Q])
        zkv = _dot(h, w_ref[:, COL_KV:COL_KV + 2 * A_KV])
        for pair in range(A_Q // LANES):
            zq = rotary(zq_all[:, pair * LANES:(pair + 1) * LANES]) * (A_HEAD_DIM ** -0.5 * LOG2E)
            zsw = pltpu.roll(zq, A_HEAD_DIM, axis=1)
            zero = jnp.zeros_like(zq)
            if pair < A_HEADS // (2 * A_KV_HEADS):
                even, odd = jnp.where(low, zq, zero), jnp.where(low, zsw, zero)
            else:
                even, odd = jnp.where(low, zero, zsw), jnp.where(low, zero, zq)
            for head, val in ((2 * pair, even), (2 * pair + 1, odd)):
                j = Q_HEAD_ORDER.index(head)
                q_ref[:, j * LANES:(j + 1) * LANES] = val.astype(BF16)
        k_ref[...] = rotary(zkv[:, :A_KV])
        v_ref[...] = zkv[:, A_KV:]

    def mlstm_qk_part():
        for j in range(2):
            zm = _dot(h, w_ref[:, COL_M + j * M_W:COL_M + (j + 1) * M_W])
            if j == 1:
                zm = zm * (M_HEAD_DIM ** -0.5)
            m4_ref[:, j * M_W:(j + 1) * M_W] = zm.astype(BF16)

    def mlstm_vo_part():
        for j in range(2, 4):
            zm = _dot(h, w_ref[:, COL_M + j * M_W:COL_M + (j + 1) * M_W])
            m4_ref[:, j * M_W:(j + 1) * M_W] = zm.astype(BF16)
        zg = _dot(h, wg_ref[...]) + gb_ref[...]
        zf = zg[:, LANES:]
        gt_ref[:, :LANES] = zg[:, :LANES]
        gt_ref[:, LANES:] = jnp.minimum(zf, 0.0) - jnp.log1p(jnp.exp(-jnp.abs(zf)))

    return attention_part, mlstm_qk_part, mlstm_vo_part


def _inproj_kernel(*refs):
    for part in _projection_parts(*refs):
        part()


def _inproj(x, g1, w, wg, tabs, gb, tm, layer):
    n = x.shape[0]
    cos, sa, sb = tabs
    nt = cos.shape[0] // tm
    row = lambda i: (i, 0)
    of_layer = lambda i: (layer, 0, 0)
    tab = lambda i: (i % nt, 0)
    return pl.pallas_call(
        _inproj_kernel,
        grid=(n // tm,),
        in_specs=[
            pl.BlockSpec((tm, D_MODEL), row),
            pl.BlockSpec((None, 1, D_MODEL), of_layer),
            pl.BlockSpec((None, D_MODEL, COL_G), of_layer),
            pl.BlockSpec((None, D_MODEL, GATE_COLS), of_layer),
            pl.BlockSpec((tm, LANES), tab),
            pl.BlockSpec((tm, LANES), tab),
            pl.BlockSpec((tm, LANES), tab),
            pl.BlockSpec((None, 1, GATE_COLS), of_layer),
        ],
        out_specs=[
            pl.BlockSpec((tm, Q_PAD), row),
            pl.BlockSpec((tm, A_KV), row),
            pl.BlockSpec((tm, A_KV), row),
            pl.BlockSpec((tm, 4 * M_W), row),
            pl.BlockSpec((tm, GATE_COLS), row),
        ],
        out_shape=[
            jax.ShapeDtypeStruct((n, Q_PAD), BF16),
            jax.ShapeDtypeStruct((n, A_KV), F32),
            jax.ShapeDtypeStruct((n, A_KV), F32),
            jax.ShapeDtypeStruct((n, 4 * M_W), BF16),
            jax.ShapeDtypeStruct((n, GATE_COLS), F32),
        ],
        compiler_params=pltpu.CompilerParams(
            dimension_semantics=("arbitrary",), vmem_limit_bytes=VMEM_LIMIT),
        name="inproj",
    )(x, g1, w, wg, cos, sa, sb, gb)


def _split3(x):
    hi = x.astype(BF16)
    r1 = x - hi.astype(F32)
    mid = r1.astype(BF16)
    lo = (r1 - mid.astype(F32)).astype(BF16)
    return hi, mid, lo


def _mixer_stages(t, sink_ref, q_ref, k_ref, v_ref, m4_ref, gt_ref, init_state, mg_ref,
                  mix_ref, c_ref, n_ref, m_ref, kbuf, vt, vs, cst,
                  *, L, steps, halo_valid, layer, m_out_ref=None):
    n_chunks = L // CHUNK
    n_keys = WINDOW + CHUNK
    half = A_HEAD_DIM

    def put_vt(cols, v):
        v_t = v.T
        vt[:, cols] = v_t.astype(BF16)
        vs[:, cols] = jnp.concatenate([v_t[half:], v_t[:half]], axis=0).astype(BF16)

    @pl.when(t == 0)
    def _():
        vt[...] = jnp.zeros_like(vt)
        vs[...] = jnp.zeros_like(vs)
        init_state(put_vt)

    yield

    kbuf[WINDOW:WINDOW + L, :] = k_ref[...].astype(BF16)
    put_vt(pl.ds(WINDOW, L), v_ref[...])

    heads = range(M_HEADS)
    m_slice = lambda j, h: m4_ref[:, j * M_W + h * M_HEAD_DIM:j * M_W + (h + 1) * M_HEAD_DIM]

    scores = []
    for ci in range(n_chunks):
        r0 = ci * CHUNK
        q8 = jnp.concatenate([q_ref[r0:r0 + CHUNK, j * LANES:(j + 1) * LANES]
                              for j in range(A_HEADS)], axis=0)
        scores.append(_dot_nt(kbuf[r0:r0 + n_keys, :], q8))

    gates_i = gt_ref[:, :LANES]
    gates_f = gt_ref[:, LANES:]
    ri = lax.broadcasted_iota(jnp.int32, (L, L), 0)
    ci_ = lax.broadcasted_iota(jnp.int32, (L, L), 1)
    causal = ci_ <= ri
    b3 = _dot(causal.astype(BF16), jnp.concatenate(_split3(gates_f), axis=1))
    b = b3[:, :LANES] + b3[:, LANES:2 * LANES] + b3[:, 2 * LANES:]
    c_old = [cst[h] for h in heads]
    qk = [_dot_nt(m_slice(0, h), m_slice(1, h)) for h in heads]
    qc = [_dot_nt(m_slice(0, h), c_old[h].astype(BF16)) for h in heads]
    yield

    lane_q = lax.broadcasted_iota(jnp.int32, (1, A_HEADS * CHUNK), 1) // CHUNK
    sink_row = jnp.zeros((1, A_HEADS * CHUNK), F32)
    for j, head in enumerate(Q_HEAD_ORDER):
        sink_row = jnp.where(lane_q == j, sink_ref[layer, head] * LOG2E, sink_row)
    key_row = lax.broadcasted_iota(jnp.int32, (n_keys, A_HEADS * CHUNK), 0)
    probs, rdens = [], []
    for ci in range(n_chunks):
        s = scores[ci]
        if not halo_valid and ci < WINDOW // CHUNK:
            first_key = (t * n_chunks + ci - WINDOW // CHUNK) * CHUNK
            s = jnp.where(key_row + first_key >= 0, s, NEG)
        mx = jnp.maximum(jnp.max(s, axis=0, keepdims=True), sink_row)
        p = jnp.exp2(s - mx)
        rdens.append(1.0 / (jnp.sum(p, axis=0, keepdims=True) + jnp.exp2(sink_row - mx)))
        pad = jnp.zeros((CHUNK, A_HEADS * CHUNK), BF16)
        pb = p.astype(BF16)
        probs.append(jnp.concatenate([pb, pad] if ci % 2 == 0 else [pad, pb], axis=0))

    a = gates_i - b
    row_l = lax.broadcasted_iota(jnp.int32, (L, LANES), 0)
    cm = a
    shift = 1
    while shift < L:
        cm = jnp.maximum(cm, jnp.where(row_l >= shift, pltpu.roll(cm, shift, axis=0), NEG))
        shift *= 2
    m_prev = m_ref[0, 0:1, :]
    c = jnp.maximum(cm, m_prev)
    mt = c + b
    w_inter = jnp.exp(m_prev - c)
    e_neg_mt = jnp.exp(-mt)
    m_last = mt[L - 1:L, :]
    b_last = b[L - 1:L, :]
    delta = b_last - m_last
    decay = jnp.exp(b_last + m_prev - m_last)
    a_t = (a * LOG2E).T
    c2 = c * LOG2E
    delta2 = delta * LOG2E
    blk = min(L, LANES)
    n_blk = L // blk
    tri = (lax.broadcasted_iota(jnp.int32, (blk, blk), 1)
           <= lax.broadcasted_iota(jnp.int32, (blk, blk), 0))
    ones = jnp.ones((L, M_HEAD_DIM), BF16)
    out_gate = [jax.nn.sigmoid(m_slice(3, h).astype(F32)) for h in heads]
    s_bf, upd_lhs = [], []
    for h in heads:
        a_row = a_t[h:h + 1, :]
        row_blocks = []
        for rb in range(n_blk):
            rows = slice(rb * blk, (rb + 1) * blk)
            c_col = c2[rows, h:h + 1]
            e = []
            for cb in range(rb + 1):
                d = a_row[:, cb * blk:(cb + 1) * blk] - c_col
                e.append(jnp.exp2(jnp.where(tri, d, NEG) if cb == rb else d))
            s = (qk[h][rows, :(rb + 1) * blk] * jnp.concatenate(e, axis=1)).astype(BF16)
            if rb + 1 < n_blk:
                s = jnp.concatenate([s, jnp.zeros((blk, L - (rb + 1) * blk), BF16)], axis=1)
            row_blocks.append(s)
        s_bf.append(jnp.concatenate(row_blocks, axis=0))
        ws_row = jnp.exp2(a_row + delta2[:, h:h + 1])
        v_t = m_slice(2, h).astype(F32).T
        upd_lhs.append(jnp.concatenate(
            [v_t * ws_row, jnp.broadcast_to(ws_row, (M_HEAD_DIM, L))], axis=0).astype(BF16))

    yield

    pv = []
    for ci in range(n_chunks):
        c0 = ci * CHUNK if ci % 2 == 0 else (ci - 1) * CHUNK
        win = slice(c0, c0 + 2 * LANES)
        pv.append((_dot(vt[:, win], probs[ci][:, :2 * LANES]),
                   _dot(vs[:, win], probs[ci][:, 2 * LANES:])))
    sv = [_dot(s_bf[h], jnp.concatenate([m_slice(2, h), ones], axis=1)) for h in heads]
    c_upd = [_dot(upd_lhs[h], m_slice(1, h)) for h in heads]
    yield

    for ci in range(n_chunks):
        r0 = ci * CHUNK
        ra = pv[ci][0] * rdens[ci][:, :2 * LANES]
        rb = pv[ci][1] * rdens[ci][:, 2 * LANES:]
        o0 = jnp.concatenate([ra[:half, :LANES], rb[half:, :LANES]], axis=0).T.astype(BF16)
        o1 = jnp.concatenate([rb[:half, LANES:], ra[half:, LANES:]], axis=0).T.astype(BF16)
        mix_ref[r0:r0 + CHUNK, 0 * LANES:1 * LANES] = o0[:CHUNK]
        mix_ref[r0:r0 + CHUNK, 1 * LANES:2 * LANES] = o0[CHUNK:]
        mix_ref[r0:r0 + CHUNK, 2 * LANES:3 * LANES] = o1[:CHUNK]
        mix_ref[r0:r0 + CHUNK, 3 * LANES:4 * LANES] = o1[CHUNK:]
    for h in heads:
        tot = sv[h] + w_inter[:, h:h + 1] * qc[h]
        hh = tot[:, :M_HEAD_DIM] / jnp.maximum(jnp.abs(tot[:, M_HEAD_DIM:]), e_neg_mt[:, h:h + 1])
        hn = _rms(hh, mg_ref[:, h * M_HEAD_DIM:(h + 1) * M_HEAD_DIM])
        out = hn * out_gate[h]
        mix_ref[:, A_Q + h * M_HEAD_DIM:A_Q + (h + 1) * M_HEAD_DIM] = out.astype(BF16)
        cst[h] = decay[:, h:h + 1] * c_old[h] + c_upd[h]
    m_ref[0] = jnp.broadcast_to(m_last, (SUBLANES, LANES))
    yield

    def write_state():
        for h in heads:
            c_ref[0, h] = cst[h, :M_HEAD_DIM, :]
            n_ref[0, h:h + 1, :] = cst[h, M_HEAD_DIM:M_HEAD_DIM + 1, :]
        if m_out_ref is not None:
            m_out_ref[...] = m_ref[...]

    if steps == 1:
        write_state()
    else:
        pl.when(t == steps - 1)(write_state)

    if steps > 1:
        kbuf[0:WINDOW, :] = kbuf[L:L + WINDOW, :]
        vt[:, 0:WINDOW] = vt[:, L:L + WINDOW]
        vs[:, 0:WINDOW] = vs[:, L:L + WINDOW]


def _interleave(gens):
    done = object()
    live = list(gens)
    while live:
        live = [g for g in live if next(g, done) is not done]


def _mixer_kernel(sink_ref, q_ref, k_ref, v_ref, m4_ref, gt_ref, hk_ref, hv_ref, c0_ref, n0_ref,
                  m0_ref, mg_ref, mix_ref, c_ref, n_ref, m_ref, kbuf, vt, vs, cst,
                  *, group, **static):
    def stages(i):
        one = lambda ref: ref.at[pl.ds(i, 1)]

        def init_state(put_vt):
            kbuf[i, 0:WINDOW, :] = hk_ref[i].astype(BF16)
            put_vt(pl.ds(0, WINDOW), hv_ref[i])
            for h in range(M_HEADS):
                cst[i, h, :M_HEAD_DIM, :] = c0_ref[i, h]
                cst[i, h, M_HEAD_DIM:, :] = jnp.broadcast_to(
                    n0_ref[i, h:h + 1, :], (M_HEAD_DIM, M_HEAD_DIM))
            m_ref[i] = m0_ref[i]

        return _mixer_stages(
            pl.program_id(1), sink_ref, q_ref.at[i], k_ref.at[i], v_ref.at[i], m4_ref.at[i],
            gt_ref.at[i], init_state, mg_ref, mix_ref.at[i], one(c_ref), one(n_ref), one(m_ref),
            kbuf.at[i], vt.at[i], vs.at[i], cst.at[i], **static)

    _interleave([stages(i) for i in range(group)])


def _mixer(q, k, v, m4, gt, halo_k, halo_v, c0, n0, m0, sink, mg,
           *, batch, group, L, halo_valid, layer, state_layer):
    n = q.shape[0]
    T = n // batch
    steps = T // L
    assert steps == 1 or L % LANES == 0
    assert batch % group == 0
    tile = lambda b, t: (b, t, 0)
    per_b3 = lambda b, t: (b, 0, 0)
    per_b4 = lambda b, t: (b, 0, 0, 0)
    init3 = lambda b, t: (state_layer, b, 0, 0)
    init4 = lambda b, t: (state_layer, b, 0, 0, 0)
    kv_cols = max(WINDOW + L, 2 * LANES)
    by_seq = lambda a: a.reshape(batch, T, a.shape[-1])
    mix, C, n_out, m_out = pl.pallas_call(
        functools.partial(_mixer_kernel, group=group, L=L, steps=steps, halo_valid=halo_valid,
                          layer=layer),
        grid=(batch // group, steps),
        in_specs=[
            pl.BlockSpec(memory_space=pltpu.SMEM),
            pl.BlockSpec((group, L, Q_PAD), tile),
            pl.BlockSpec((group, L, A_KV), tile),
            pl.BlockSpec((group, L, A_KV), tile),
            pl.BlockSpec((group, L, 4 * M_W), tile),
            pl.BlockSpec((group, L, GATE_COLS), tile),
            pl.BlockSpec((None, group, WINDOW, A_KV), init3),
            pl.BlockSpec((None, group, WINDOW, A_KV), init3),
            pl.BlockSpec((None, group, M_HEADS, M_HEAD_DIM, M_HEAD_DIM), init4),
            pl.BlockSpec((None, group, M_HEADS, M_HEAD_DIM), init3),
            pl.BlockSpec((None, group, SUBLANES, LANES), init3),
            pl.BlockSpec((None, 1, M_W), lambda b, t: (layer, 0, 0)),
        ],
        out_specs=[
            pl.BlockSpec((group, L, D_MODEL), tile),
            pl.BlockSpec((group, M_HEADS, M_HEAD_DIM, M_HEAD_DIM), per_b4),
            pl.BlockSpec((group, M_HEADS, M_HEAD_DIM), per_b3),
            pl.BlockSpec((group, SUBLANES, LANES), per_b3),
        ],
        out_shape=[
            jax.ShapeDtypeStruct((batch, T, D_MODEL), BF16),
            jax.ShapeDtypeStruct((batch, M_HEADS, M_HEAD_DIM, M_HEAD_DIM), F32),
            jax.ShapeDtypeStruct((batch, M_HEADS, M_HEAD_DIM), F32),
            jax.ShapeDtypeStruct((batch, SUBLANES, LANES), F32),
        ],
        scratch_shapes=[
            pltpu.VMEM((group, WINDOW + L, LANES), BF16),
            pltpu.VMEM((group, LANES, kv_cols), BF16),
            pltpu.VMEM((group, LANES, kv_cols), BF16),
            pltpu.VMEM((group, M_HEADS, 2 * M_HEAD_DIM, M_HEAD_DIM), F32),
        ],
        compiler_params=pltpu.CompilerParams(
            dimension_semantics=("arbitrary", "arbitrary"), vmem_limit_bytes=VMEM_LIMIT),
        name="mixer",
    )(sink, by_seq(q), by_seq(k), by_seq(v), by_seq(m4), by_seq(gt), halo_k, halo_v, c0, n0, m0, mg)
    return mix.reshape(n, D_MODEL), C, n_out, m_out


def _ffn_stages(x, mix, wo_ref, g2_ref, wu_ref, wd_ref, fg_ref, store):
    x1 = x + _dot(mix, wo_ref[...])
    xn = _rms(x1, g2_ref[...]).astype(BF16)
    acc = x1
    yield
    for c in range(D_FF // FF_CHUNK):
        u = _dot(xn, wu_ref[:, c * FF_CHUNK:(c + 1) * FF_CHUNK])
        a = jnp.square(jnp.maximum(u, 0.0)).astype(BF16)
        acc = acc + _dot(a, wd_ref[c * FF_CHUNK:(c + 1) * FF_CHUNK, :])
        if c + 1 < D_FF // FF_CHUNK:
            yield
    if fg_ref is not None:
        acc = _rms(acc, fg_ref[...])
    store(acc)


def _ffn_kernel(*refs, final):
    if final:
        x_ref, mix_ref, wo_ref, g2_ref, wu_ref, wd_ref, fg_ref, out_ref = refs
    else:
        x_ref, mix_ref, wo_ref, g2_ref, wu_ref, wd_ref, out_ref = refs
        fg_ref = None

    def store(acc):
        out_ref[...] = acc

    _interleave([_ffn_stages(x_ref[...], mix_ref[...], wo_ref, g2_ref, wu_ref, wd_ref, fg_ref, store)])


def _back_kernel(*refs, group, L, steps, n_steps, layer, final):
    (sink_ref, q_ref, k_ref, v_ref, m4_ref, gt_ref, mg_ref, x_ref, wo_ref, g2_ref, wu_ref,
     wd_ref) = refs[:12]
    fg_ref = refs[12] if final else None
    y_ref, c_ref, n_ref, m_ref, kbuf, vt, vs, cst, mst, mix_z = refs[12 + final:]
    s = pl.program_id(0)
    wslot = s % 2
    rslot = 1 - wslot
    t = s % steps

    def mixer(i):
        one = lambda ref: ref.at[pl.ds(i, 1)]

        def init_state(put_vt):
            kbuf[i, 0:WINDOW, :] = jnp.zeros((WINDOW, LANES), BF16)
            cst[i] = jnp.zeros(cst.shape[1:], cst.dtype)
            mst[i] = jnp.zeros(mst.shape[1:], mst.dtype)

        return _mixer_stages(
            t, sink_ref, q_ref.at[i], k_ref.at[i], v_ref.at[i], m4_ref.at[i], gt_ref.at[i],
            init_state, mg_ref, mix_z.at[wslot, i], one(c_ref), one(n_ref), one(mst),
            kbuf.at[i], vt.at[i], vs.at[i], cst.at[i],
            L=L, steps=steps, halo_valid=False, layer=layer, m_out_ref=one(m_ref))

    def store(acc):
        y_ref[...] = acc.reshape(group, L, D_MODEL)

    def body(mix, feed):
        rows = group * L
        ffn = iter(()) if not feed else _ffn_stages(
            x_ref[...].reshape(rows, D_MODEL), mix_z[rslot].reshape(rows, D_MODEL),
            wo_ref, g2_ref, wu_ref, wd_ref, fg_ref, store)
        mixers = [mixer(i) for i in range(group)] if mix else []
        every = lambda: [next(m) for m in mixers]
        quarter = lambda: [next(ffn, None) for _ in range(D_FF // FF_CHUNK // 4)]
        every()
        next(ffn, None)
        every()
        quarter()
        every()
        every()
        quarter()
        every()
        _interleave([ffn])
        _interleave(mixers)

    pl.when(s == 0)(lambda: body(True, False))
    pl.when(jnp.logical_and(s > 0, s < n_steps))(lambda: body(True, True))
    pl.when(s == n_steps)(lambda: body(False, True))


def _ffn(x, mix, wo, g2, wu, wd, fg, tm, layer):
    n = x.shape[0]
    final = fg is not None
    row = lambda i: (i, 0)
    fixed = lambda i: (0, 0)
    of_layer = lambda i: (layer, 0, 0)
    single = pl.Buffered(1)
    in_specs = [
        pl.BlockSpec((tm, D_MODEL), row),
        pl.BlockSpec((tm, D_MODEL), row),
        pl.BlockSpec((None, D_MODEL, D_MODEL), of_layer, pipeline_mode=single),
        pl.BlockSpec((None, 1, D_MODEL), of_layer),
        pl.BlockSpec((None, D_MODEL, D_FF), of_layer, pipeline_mode=single),
        pl.BlockSpec((None, D_FF, D_MODEL), of_layer, pipeline_mode=single),
    ]
    args = [x, mix, wo, g2, wu, wd]
    if final:
        in_specs.append(pl.BlockSpec((1, D_MODEL), fixed))
        args.append(fg)
    return pl.pallas_call(
        functools.partial(_ffn_kernel, final=final),
        grid=(n // tm,),
        in_specs=in_specs,
        out_specs=pl.BlockSpec((tm, D_MODEL), row),
        out_shape=jax.ShapeDtypeStruct((n, D_MODEL), F32),
        compiler_params=pltpu.CompilerParams(
            dimension_semantics=("arbitrary",), vmem_limit_bytes=VMEM_LIMIT),
        name="ffn_final" if final else "ffn",
    )(*args)


def _back(x, q, k, v, m4, gt, sink, mg, wo, g2, wu, wd, fg, *, batch, group, L, layer):
    n = x.shape[0]
    T = n // batch
    steps = T // L
    n_steps = (batch // group) * steps
    assert steps > 1 and L % LANES == 0 and batch % group == 0
    final = fg is not None
    mixed = lambda s: jnp.minimum(s, n_steps - 1)
    fed = lambda s: jnp.maximum(s - 1, 0)
    mix_tile = lambda s: (mixed(s) // steps, mixed(s) % steps, 0)
    ffn_tile = lambda s: (fed(s) // steps, fed(s) % steps, 0)
    seq3 = lambda s: (mixed(s) // steps, 0, 0)
    of_layer = lambda s: (layer, 0, 0)
    single = pl.Buffered(1)
    by_seq = lambda a: a.reshape(batch, T, a.shape[-1])
    in_specs = [
        pl.BlockSpec(memory_space=pltpu.SMEM),
        pl.BlockSpec((group, L, Q_PAD), mix_tile),
        pl.BlockSpec((group, L, A_KV), mix_tile),
        pl.BlockSpec((group, L, A_KV), mix_tile),
        pl.BlockSpec((group, L, 4 * M_W), mix_tile),
        pl.BlockSpec((group, L, GATE_COLS), mix_tile),
        pl.BlockSpec((None, 1, M_W), of_layer),
        pl.BlockSpec((group, L, D_MODEL), ffn_tile),
        pl.BlockSpec((None, D_MODEL, D_MODEL), of_layer, pipeline_mode=single),
        pl.BlockSpec((None, 1, D_MODEL), of_layer),
        pl.BlockSpec((None, D_MODEL, D_FF), of_layer, pipeline_mode=single),
        pl.BlockSpec((None, D_FF, D_MODEL), of_layer, pipeline_mode=single),
    ]
    args = [sink, by_seq(q), by_seq(k), by_seq(v), by_seq(m4), by_seq(gt), mg, by_seq(x),
            wo, g2, wu, wd]
    if final:
        in_specs.append(pl.BlockSpec((1, D_MODEL), lambda s: (0, 0)))
        args.append(fg)
    kv_cols = WINDOW + L
    y, C, n_out, m_out = pl.pallas_call(
        functools.partial(_back_kernel, group=group, L=L, steps=steps, n_steps=n_steps,
                          layer=layer, final=final),
        grid=(n_steps + 1,),
        in_specs=in_specs,
        out_specs=[
            pl.BlockSpec((group, L, D_MODEL), ffn_tile),
            pl.BlockSpec((group, M_HEADS, M_HEAD_DIM, M_HEAD_DIM), lambda s: (mixed(s) // steps, 0, 0, 0)),
            pl.BlockSpec((group, M_HEADS, M_HEAD_DIM), seq3),
            pl.BlockSpec((group, SUBLANES, LANES), seq3),
        ],
        out_shape=[
            jax.ShapeDtypeStruct((batch, T, D_MODEL), F32),
            jax.ShapeDtypeStruct((batch, M_HEADS, M_HEAD_DIM, M_HEAD_DIM), F32),
            jax.ShapeDtypeStruct((batch, M_HEADS, M_HEAD_DIM), F32),
            jax.ShapeDtypeStruct((batch, SUBLANES, LANES), F32),
        ],
        scratch_shapes=[
            pltpu.VMEM((group, WINDOW + L, LANES), BF16),
            pltpu.VMEM((group, LANES, kv_cols), BF16),
            pltpu.VMEM((group, LANES, kv_cols), BF16),
            pltpu.VMEM((group, M_HEADS, 2 * M_HEAD_DIM, M_HEAD_DIM), F32),
            pltpu.VMEM((group, SUBLANES, LANES), F32),
            pltpu.VMEM((2, group, L, D_MODEL), BF16),
        ],
        compiler_params=pltpu.CompilerParams(
            dimension_semantics=("arbitrary",), vmem_limit_bytes=VMEM_LIMIT),
        name="back_final" if final else "back",
    )(*args)
    return y.reshape(n, D_MODEL), C, n_out, m_out


def _rotary_tables(pos, rows):
    inv = ROPE_THETA ** (-(jnp.arange(ROPE_HALF, dtype=F32) * 2.0 / ROPE_DIM))
    ang = pos[:, None] * inv[None, :]
    cos, sin = jnp.cos(ang), jnp.sin(ang)
    n = pos.shape[0]
    pad = jnp.zeros((n, A_HEAD_DIM - ROPE_DIM), F32)
    zeros = jnp.zeros_like(sin)
    cos_h = jnp.concatenate([cos, cos, pad + 1.0], axis=1)
    sa_h = jnp.concatenate([-sin, zeros, pad], axis=1)
    sb_h = jnp.concatenate([zeros, sin, pad], axis=1)
    rep = LANES // A_HEAD_DIM
    tile = lambda a: jnp.tile(a, (rows // n, rep))
    return tile(cos_h), tile(sa_h), tile(sb_h)


def _gate_weight_tiles(w):
    gate_pad = jnp.zeros(w.shape[:2] + (LANES - M_HEADS,), w.dtype)
    return jnp.concatenate(
        [w[..., COL_G:COL_G + M_HEADS], gate_pad, w[..., COL_G + M_HEADS:], gate_pad],
        axis=-1).astype(BF16)


def _pad_gate_bias(gb):
    pad = jnp.zeros((gb.shape[0], LANES - M_HEADS), gb.dtype)
    return jnp.concatenate([gb[:, :M_HEADS], pad, gb[:, M_HEADS:], pad], axis=1)[:, None, :]


def _m_rows(m):
    m_row = jnp.pad(m, [(0, 0)] * (m.ndim - 1) + [(0, LANES - M_HEADS)])
    return jnp.broadcast_to(m_row[..., None, :], m.shape[:-1] + (SUBLANES, LANES))


def kernel(x_prompt, x_sample, cache_k, cache_v, state_C, state_n, state_m, norm1_g, w_in, gate_b,
           attn_sink, mnorm_g, w_out, norm2_g, w_up, w_down, final_g):
    bp, T, _ = x_prompt.shape
    bs, S, _ = x_sample.shape
    depth = w_in.shape[0]
    assert T % PROMPT_MLSTM_CHUNK == 0 and S == CHUNK
    assert T % PROJ_TILE == 0 and PROJ_TILE % S == 0 and (bs * S) % PROJ_TILE == 0
    assert (bs * S) % TOKEN_TILE == 0

    tabs_p = _rotary_tables(jnp.arange(T, dtype=F32), max(T, PROJ_TILE))
    tabs_s = _rotary_tables(jnp.arange(S, dtype=F32) + float(PAST_LEN), max(S, PROJ_TILE))

    g1 = norm1_g[:, None, :]
    g2 = norm2_g[:, None, :]
    mg = mnorm_g[:, None, :]
    w1 = w_in.astype(BF16)
    wg = _gate_weight_tiles(w_in)
    gb = _pad_gate_bias(gate_b)
    wo = w_out.astype(BF16)
    wu = w_up.astype(BF16)
    wd = w_down.astype(BF16)
    halo_k = cache_k.reshape(depth, bs, WINDOW, A_KV)
    halo_v = cache_v.reshape(depth, bs, WINDOW, A_KV)
    m0 = _m_rows(state_m)

    yp = x_prompt.reshape(bp * T, D_MODEL)
    ys = x_sample.reshape(bs * S, D_MODEL)
    outs = {name: [] for name in ("pk", "pv", "pC", "pn", "pm", "sk", "sv", "sC", "sn", "sm")}
    fg = final_g.reshape(1, D_MODEL)
    for l in range(depth):
        last = fg if l == depth - 1 else None

        q, k, v, m4, gt = _inproj(yp, g1, w1, wg, tabs_p, gb, PROJ_TILE, l)
        yp, C, n, m_rows = _back(yp, q, k, v, m4, gt, attn_sink, mg, wo, g2, wu, wd, last,
                                 batch=bp, group=SEQ_GROUP, L=PROMPT_MLSTM_CHUNK, layer=l)
        outs["pk"].append(k.reshape(bp, T, A_KV)[:, T - WINDOW:])
        outs["pv"].append(v.reshape(bp, T, A_KV)[:, T - WINDOW:])
        outs["pC"].append(C)
        outs["pn"].append(n)
        outs["pm"].append(m_rows)

        q, k, v, m4, gt = _inproj(ys, g1, w1, wg, tabs_s, gb, PROJ_TILE, l)
        mix, C, n, m_rows = _mixer(q, k, v, m4, gt, halo_k, halo_v, state_C, state_n, m0,
                                   attn_sink, mg, batch=bs, group=SEQ_GROUP, L=S, halo_valid=True,
                                   layer=l, state_layer=l)
        ys = _ffn(ys, mix, wo, g2, wu, wd, last, TOKEN_TILE, l)
        outs["sk"].append(k.reshape(bs, S, A_KV))
        outs["sv"].append(v.reshape(bs, S, A_KV))
        outs["sC"].append(C)
        outs["sn"].append(n)
        outs["sm"].append(m_rows)

    st = lambda name: jnp.stack(outs[name])
    kv = lambda name: (lambda a: a.reshape(a.shape[:-1] + (A_KV_HEADS, A_HEAD_DIM)))(st(name))
    m_of = lambda name: st(name)[:, :, 0, :M_HEADS]
    return (yp.reshape(bp, T, D_MODEL), ys.reshape(bs, S, D_MODEL),
            kv("pk"), kv("pv"), st("pC"), st("pn"), m_of("pm"),
            kv("sk"), kv("sv"), st("sC"), st("sn"), m_of("sm"))
```

```python
import functools

import jax
import jax.numpy as jnp
from jax import lax
from jax.experimental import pallas as pl
from jax.experimental.pallas import tpu as pltpu

D_MODEL = 1024
CHUNK = 64
A_HEADS = 8
A_KV_HEADS = 2
A_HEAD_DIM = 64
WINDOW = 128
ROPE_THETA = 500000.0
ROPE_DIM = A_HEAD_DIM // 4
ROPE_HALF = ROPE_DIM // 2
M_HEADS = 4
M_HEAD_DIM = 128
D_FF = 4 * D_MODEL
NORM_EPS = 1e-6
NEG = -1e30
LOG2E = 1.4426950408889634
PAST_LEN = 4096
A_Q = A_HEADS * A_HEAD_DIM
A_KV = A_KV_HEADS * A_HEAD_DIM
M_W = M_HEADS * M_HEAD_DIM

LANES = 128
SUBLANES = 8
Q_PAD = A_HEADS * LANES
GATE_COLS = 2 * LANES
COL_KV = A_Q
COL_M = A_Q + 2 * A_KV
COL_G = COL_M + 4 * M_W
Q_HEAD_ORDER = (0, 2, 5, 7, 1, 3, 4, 6)

TOKEN_TILE = 512
PROJ_TILE = 1024
PROMPT_MLSTM_CHUNK = 256
SEQ_GROUP = 2
SAMPLE_SEQ_GROUP = 8
FF_CHUNK = 1024
VMEM_LIMIT = 56 * 1024 * 1024

BF16 = jnp.bfloat16
F32 = jnp.float32


def _dot(a, b):
    return jnp.dot(a, b, preferred_element_type=F32)


def _dot_nt(a, b):
    return lax.dot_general(a, b, (((1,), (1,)), ((), ())), preferred_element_type=F32)


def _rms(x, g):
    r = lax.rsqrt(jnp.mean(x * x, axis=-1, keepdims=True) + NORM_EPS)
    return x * r * g


def _projection_parts(x_ref, g_ref, w_ref, wg_ref, cos_ref, sa_ref, sb_ref, gb_ref,
                      q_ref, k_ref, v_ref, m4_ref, gt_ref):
    h = _rms(x_ref[...], g_ref[...]).astype(BF16)
    low = lax.broadcasted_iota(jnp.int32, (1, LANES), 1) < A_HEAD_DIM

    def rotary(z):
        left = pltpu.roll(z, LANES - ROPE_HALF, axis=1)
        right = pltpu.roll(z, ROPE_HALF, axis=1)
        return z * cos_ref[...] + left * sa_ref[...] + right * sb_ref[...]

    def attention_part():
        zq_all = _dot(h, w_ref[:, :A_Q])
        zkv = _dot(h, w_ref[:, COL_KV:COL_KV + 2 * A_KV])
        for pair in range(A_Q // LANES):
            zq = rotary(zq_all[:, pair * LANES:(pair + 1) * LANES]) * (A_HEAD_DIM ** -0.5 * LOG2E)
            zsw = pltpu.roll(zq, A_HEAD_DIM, axis=1)
            zero = jnp.zeros_like(zq)
            if pair < A_HEADS // (2 * A_KV_HEADS):
                even, odd = jnp.where(low, zq, zero), jnp.where(low, zsw, zero)
            else:
                even, odd = jnp.where(low, zero, zsw), jnp.where(low, zero, zq)
            for head, val in ((2 * pair, even), (2 * pair + 1, odd)):
                j = Q_HEAD_ORDER.index(head)
                q_ref[:, j * LANES:(j + 1) * LANES] = val.astype(BF16)
        k_ref[...] = rotary(zkv[:, :A_KV])
        v_ref[...] = zkv[:, A_KV:]

    def mlstm_qk_part():
        for j in range(2):
            zm = _dot(h, w_ref[:, COL_M + j * M_W:COL_M + (j + 1) * M_W])
            if j == 1:
                zm = zm * (M_HEAD_DIM ** -0.5)
            m4_ref[:, j * M_W:(j + 1) * M_W] = zm.astype(BF16)

    def mlstm_vo_part():
        for j in range(2, 4):
            zm = _dot(h, w_ref[:, COL_M + j * M_W:COL_M + (j + 1) * M_W])
            m4_ref[:, j * M_W:(j + 1) * M_W] = zm.astype(BF16)
        zg = _dot(h, wg_ref[...]) + gb_ref[...]
        zf = zg[:, LANES:]
        gt_ref[:, :LANES] = zg[:, :LANES]
        gt_ref[:, LANES:] = jnp.minimum(zf, 0.0) - jnp.log1p(jnp.exp(-jnp.abs(zf)))

    return attention_part, mlstm_qk_part, mlstm_vo_part


def _inproj_kernel(*refs):
    for part in _projection_parts(*refs):
        part()


def _inproj(x, g1, w, wg, tabs, gb, tm, layer):
    n = x.shape[0]
    cos, sa, sb = tabs
    nt = cos.shape[0] // tm
    row = lambda i: (i, 0)
    of_layer = lambda i: (layer, 0, 0)
    tab = lambda i: (i % nt, 0)
    return pl.pallas_call(
        _inproj_kernel,
        grid=(n // tm,),
        in_specs=[
            pl.BlockSpec((tm, D_MODEL), row),
            pl.BlockSpec((None, 1, D_MODEL), of_layer),
            pl.BlockSpec((None, D_MODEL, COL_G), of_layer),
            pl.BlockSpec((None, D_MODEL, GATE_COLS), of_layer),
            pl.BlockSpec((tm, LANES), tab),
            pl.BlockSpec((tm, LANES), tab),
            pl.BlockSpec((tm, LANES), tab),
            pl.BlockSpec((None, 1, GATE_COLS), of_layer),
        ],
        out_specs=[
            pl.BlockSpec((tm, Q_PAD), row),
            pl.BlockSpec((tm, A_KV), row),
            pl.BlockSpec((tm, A_KV), row),
            pl.BlockSpec((tm, 4 * M_W), row),
            pl.BlockSpec((tm, GATE_COLS), row),
        ],
        out_shape=[
            jax.ShapeDtypeStruct((n, Q_PAD), BF16),
            jax.ShapeDtypeStruct((n, A_KV), F32),
            jax.ShapeDtypeStruct((n, A_KV), F32),
            jax.ShapeDtypeStruct((n, 4 * M_W), BF16),
            jax.ShapeDtypeStruct((n, GATE_COLS), F32),
        ],
        compiler_params=pltpu.CompilerParams(
            dimension_semantics=("arbitrary",), vmem_limit_bytes=VMEM_LIMIT),
        name="inproj",
    )(x, g1, w, wg, cos, sa, sb, gb)


def _split3(x):
    hi = x.astype(BF16)
    r1 = x - hi.astype(F32)
    mid = r1.astype(BF16)
    lo = (r1 - mid.astype(F32)).astype(BF16)
    return hi, mid, lo


def _mixer_stages(t, sink_ref, q_ref, k_ref, v_ref, m4_ref, gt_ref, init_state, mg_ref,
                  mix_ref, c_ref, n_ref, m_ref, kbuf, vt, vs, cst,
                  *, L, steps, halo_valid, layer, m_out_ref=None):
    n_chunks = L // CHUNK
    n_keys = WINDOW + CHUNK
    half = A_HEAD_DIM

    def put_vt(cols, v):
        v_t = v.T
        vt[:, cols] = v_t.astype(BF16)
        vs[:, cols] = jnp.concatenate([v_t[half:], v_t[:half]], axis=0).astype(BF16)

    @pl.when(t == 0)
    def _():
        vt[...] = jnp.zeros_like(vt)
        vs[...] = jnp.zeros_like(vs)
        init_state(put_vt)

    yield

    kbuf[WINDOW:WINDOW + L, :] = k_ref[...].astype(BF16)
    put_vt(pl.ds(WINDOW, L), v_ref[...])

    heads = range(M_HEADS)
    m_slice = lambda j, h: m4_ref[:, j * M_W + h * M_HEAD_DIM:j * M_W + (h + 1) * M_HEAD_DIM]

    scores = []
    for ci in range(n_chunks):
        r0 = ci * CHUNK
        q8 = jnp.concatenate([q_ref[r0:r0 + CHUNK, j * LANES:(j + 1) * LANES]
                              for j in range(A_HEADS)], axis=0)
        scores.append(_dot_nt(kbuf[r0:r0 + n_keys, :], q8))

    gates_i = gt_ref[:, :LANES]
    gates_f = gt_ref[:, LANES:]
    ri = lax.broadcasted_iota(jnp.int32, (L, L), 0)
    ci_ = lax.broadcasted_iota(jnp.int32, (L, L), 1)
    causal = ci_ <= ri
    b3 = _dot(causal.astype(BF16), jnp.concatenate(_split3(gates_f), axis=1))
    b = b3[:, :LANES] + b3[:, LANES:2 * LANES] + b3[:, 2 * LANES:]
    c_old = [cst[h] for h in heads]
    qk = [_dot_nt(m_slice(0, h), m_slice(1, h)) for h in heads]
    qc = [_dot_nt(m_slice(0, h), c_old[h].astype(BF16)) for h in heads]
    yield

    lane_q = lax.broadcasted_iota(jnp.int32, (1, A_HEADS * CHUNK), 1) // CHUNK
    sink_row = jnp.zeros((1, A_HEADS * CHUNK), F32)
    for j, head in enumerate(Q_HEAD_ORDER):
        sink_row = jnp.where(lane_q == j, sink_ref[layer, head] * LOG2E, sink_row)
    key_row = lax.broadcasted_iota(jnp.int32, (n_keys, A_HEADS * CHUNK), 0)
    probs, rdens = [], []
    for ci in range(n_chunks):
        s = scores[ci]
        if not halo_valid and ci < WINDOW // CHUNK:
            first_key = (t * n_chunks + ci - WINDOW // CHUNK) * CHUNK
            s = jnp.where(key_row + first_key >= 0, s, NEG)
        mx = jnp.maximum(jnp.max(s, axis=0, keepdims=True), sink_row)
        p = jnp.exp2(s - mx)
        rdens.append(1.0 / (jnp.sum(p, axis=0, keepdims=True) + jnp.exp2(sink_row - mx)))
        pad = jnp.zeros((CHUNK, A_HEADS * CHUNK), BF16)
        pb = p.astype(BF16)
        probs.append(jnp.concatenate([pb, pad] if ci % 2 == 0 else [pad, pb], axis=0))

    a = gates_i - b
    row_l = lax.broadcasted_iota(jnp.int32, (L, LANES), 0)
    cm = a
    shift = 1
    while shift < L:
        cm = jnp.maximum(cm, jnp.where(row_l >= shift, pltpu.roll(cm, shift, axis=0), NEG))
        shift *= 2
    m_prev = m_ref[0, 0:1, :]
    c = jnp.maximum(cm, m_prev)
    mt = c + b
    w_inter = jnp.exp(m_prev - c)
    e_neg_mt = jnp.exp(-mt)
    m_last = mt[L - 1:L, :]
    b_last = b[L - 1:L, :]
    delta = b_last - m_last
    decay = jnp.exp(b_last + m_prev - m_last)
    a_t = (a * LOG2E).T
    c2 = c * LOG2E
    delta2 = delta * LOG2E
    blk = min(L, LANES)
    n_blk = L // blk
    tri = (lax.broadcasted_iota(jnp.int32, (blk, blk), 1)
           <= lax.broadcasted_iota(jnp.int32, (blk, blk), 0))
    ones = jnp.ones((L, M_HEAD_DIM), BF16)
    out_gate = [jax.nn.sigmoid(m_slice(3, h).astype(F32)) for h in heads]
    s_bf, upd_lhs = [], []
    for h in heads:
        a_row = a_t[h:h + 1, :]
        row_blocks = []
        for rb in range(n_blk):
            rows = slice(rb * blk, (rb + 1) * blk)
            c_col = c2[rows, h:h + 1]
            e = []
            for cb in range(rb + 1):
                d = a_row[:, cb * blk:(cb + 1) * blk] - c_col
                e.append(jnp.exp2(jnp.where(tri, d, NEG) if cb == rb else d))
            s = (qk[h][rows, :(rb + 1) * blk] * jnp.concatenate(e, axis=1)).astype(BF16)
            if rb + 1 < n_blk:
                s = jnp.concatenate([s, jnp.zeros((blk, L - (rb + 1) * blk), BF16)], axis=1)
            row_blocks.append(s)
        s_bf.append(jnp.concatenate(row_blocks, axis=0))
        ws_row = jnp.exp2(a_row + delta2[:, h:h + 1])
        v_t = m_slice(2, h).astype(F32).T
        upd_lhs.append(jnp.concatenate(
            [v_t * ws_row, jnp.broadcast_to(ws_row, (M_HEAD_DIM, L))], axis=0).astype(BF16))

    yield

    pv = []
    for ci in range(n_chunks):
        c0 = ci * CHUNK if ci % 2 == 0 else (ci - 1) * CHUNK
        win = slice(c0, c0 + 2 * LANES)
        pv.append((_dot(vt[:, win], probs[ci][:, :2 * LANES]),
                   _dot(vs[:, win], probs[ci][:, 2 * LANES:])))
    sv = [_dot(s_bf[h], jnp.concatenate([m_slice(2, h), ones], axis=1)) for h in heads]
    c_upd = [_dot(upd_lhs[h], m_slice(1, h)) for h in heads]
    yield

    for ci in range(n_chunks):
        r0 = ci * CHUNK
        ra = pv[ci][0] * rdens[ci][:, :2 * LANES]
        rb = pv[ci][1] * rdens[ci][:, 2 * LANES:]
        o0 = jnp.concatenate([ra[:half, :LANES], rb[half:, :LANES]], axis=0).T.astype(BF16)
        o1 = jnp.concatenate([rb[:half, LANES:], ra[half:, LANES:]], axis=0).T.astype(BF16)
        mix_ref[r0:r0 + CHUNK, 0 * LANES:1 * LANES] = o0[:CHUNK]
        mix_ref[r0:r0 + CHUNK, 1 * LANES:2 * LANES] = o0[CHUNK:]
        mix_ref[r0:r0 + CHUNK, 2 * LANES:3 * LANES] = o1[:CHUNK]
        mix_ref[r0:r0 + CHUNK, 3 * LANES:4 * LANES] = o1[CHUNK:]
    for h in heads:
        tot = sv[h] + w_inter[:, h:h + 1] * qc[h]
        hh = tot[:, :M_HEAD_DIM] / jnp.maximum(jnp.abs(tot[:, M_HEAD_DIM:]), e_neg_mt[:, h:h + 1])
        hn = _rms(hh, mg_ref[:, h * M_HEAD_DIM:(h + 1) * M_HEAD_DIM])
        out = hn * out_gate[h]
        mix_ref[:, A_Q + h * M_HEAD_DIM:A_Q + (h + 1) * M_HEAD_DIM] = out.astype(BF16)
        cst[h] = decay[:, h:h + 1] * c_old[h] + c_upd[h]
    m_ref[0] = jnp.broadcast_to(m_last, (SUBLANES, LANES))
    yield

    def write_state():
        for h in heads:
            c_ref[0, h] = cst[h, :M_HEAD_DIM, :]
            n_ref[0, h:h + 1, :] = cst[h, M_HEAD_DIM:M_HEAD_DIM + 1, :]
        if m_out_ref is not None:
            m_out_ref[...] = m_ref[...]

    if steps == 1:
        write_state()
    else:
        pl.when(t == steps - 1)(write_state)

    if steps > 1:
        kbuf[0:WINDOW, :] = kbuf[L:L + WINDOW, :]
        vt[:, 0:WINDOW] = vt[:, L:L + WINDOW]
        vs[:, 0:WINDOW] = vs[:, L:L + WINDOW]


def _interleave(gens):
    done = object()
    live = list(gens)
    while live:
        live = [g for g in live if next(g, done) is not done]


def _mixer_kernel(sink_ref, q_ref, k_ref, v_ref, m4_ref, gt_ref, hk_ref, hv_ref, c0_ref, n0_ref,
                  m0_ref, mg_ref, mix_ref, c_ref, n_ref, m_ref, kbuf, vt, vs, cst,
                  *, group, **static):
    def stages(i):
        one = lambda ref: ref.at[pl.ds(i, 1)]

        def init_state(put_vt):
            kbuf[i, 0:WINDOW, :] = hk_ref[i].astype(BF16)
            put_vt(pl.ds(0, WINDOW), hv_ref[i])
            for h in range(M_HEADS):
                cst[i, h, :M_HEAD_DIM, :] = c0_ref[i, h]
                cst[i, h, M_HEAD_DIM:, :] = jnp.broadcast_to(
                    n0_ref[i, h:h + 1, :], (M_HEAD_DIM, M_HEAD_DIM))
            m_ref[i] = m0_ref[i]

        return _mixer_stages(
            pl.program_id(1), sink_ref, q_ref.at[i], k_ref.at[i], v_ref.at[i], m4_ref.at[i],
            gt_ref.at[i], init_state, mg_ref, mix_ref.at[i], one(c_ref), one(n_ref), one(m_ref),
            kbuf.at[i], vt.at[i], vs.at[i], cst.at[i], **static)

    _interleave([stages(i) for i in range(group)])


def _mixer(q, k, v, m4, gt, halo_k, halo_v, c0, n0, m0, sink, mg,
           *, batch, group, L, halo_valid, layer, state_layer):
    n = q.shape[0]
    T = n // batch
    steps = T // L
    assert steps == 1 or L % LANES == 0
    assert batch % group == 0
    tile = lambda b, t: (b, t, 0)
    per_b3 = lambda b, t: (b, 0, 0)
    per_b4 = lambda b, t: (b, 0, 0, 0)
    init3 = lambda b, t: (state_layer, b, 0, 0)
    init4 = lambda b, t: (state_layer, b, 0, 0, 0)
    kv_cols = max(WINDOW + L, 2 * LANES)
    by_seq = lambda a: a.reshape(batch, T, a.shape[-1])
    mix, C, n_out, m_out = pl.pallas_call(
        functools.partial(_mixer_kernel, group=group, L=L, steps=steps, halo_valid=halo_valid,
                          layer=layer),
        grid=(batch // group, steps),
        in_specs=[
            pl.BlockSpec(memory_space=pltpu.SMEM),
            pl.BlockSpec((group, L, Q_PAD), tile),
            pl.BlockSpec((group, L, A_KV), tile),
            pl.BlockSpec((group, L, A_KV), tile),
            pl.BlockSpec((group, L, 4 * M_W), tile),
            pl.BlockSpec((group, L, GATE_COLS), tile),
            pl.BlockSpec((None, group, WINDOW, A_KV), init3),
            pl.BlockSpec((None, group, WINDOW, A_KV), init3),
            pl.BlockSpec((None, group, M_HEADS, M_HEAD_DIM, M_HEAD_DIM), init4),
            pl.BlockSpec((None, group, M_HEADS, M_HEAD_DIM), init3),
            pl.BlockSpec((None, group, SUBLANES, LANES), init3),
            pl.BlockSpec((None, 1, M_W), lambda b, t: (layer, 0, 0)),
        ],
        out_specs=[
            pl.BlockSpec((group, L, D_MODEL), tile),
            pl.BlockSpec((group, M_HEADS, M_HEAD_DIM, M_HEAD_DIM), per_b4),
            pl.BlockSpec((group, M_HEADS, M_HEAD_DIM), per_b3),
            pl.BlockSpec((group, SUBLANES, LANES), per_b3),
        ],
        out_shape=[
            jax.ShapeDtypeStruct((batch, T, D_MODEL), BF16),
            jax.ShapeDtypeStruct((batch, M_HEADS, M_HEAD_DIM, M_HEAD_DIM), F32),
            jax.ShapeDtypeStruct((batch, M_HEADS, M_HEAD_DIM), F32),
            jax.ShapeDtypeStruct((batch, SUBLANES, LANES), F32),
        ],
        scratch_shapes=[
            pltpu.VMEM((group, WINDOW + L, LANES), BF16),
            pltpu.VMEM((group, LANES, kv_cols), BF16),
            pltpu.VMEM((group, LANES, kv_cols), BF16),
            pltpu.VMEM((group, M_HEADS, 2 * M_HEAD_DIM, M_HEAD_DIM), F32),
        ],
        compiler_params=pltpu.CompilerParams(
            dimension_semantics=("arbitrary", "arbitrary"), vmem_limit_bytes=VMEM_LIMIT),
        name="mixer",
    )(sink, by_seq(q), by_seq(k), by_seq(v), by_seq(m4), by_seq(gt), halo_k, halo_v, c0, n0, m0, mg)
    return mix.reshape(n, D_MODEL), C, n_out, m_out


def _ffn_stages(x, mix, wo_ref, g2_ref, wu_ref, wd_ref, fg_ref, store):
    x1 = x + _dot(mix, wo_ref[...])
    xn = _rms(x1, g2_ref[...]).astype(BF16)
    acc = x1
    yield
    for c in range(D_FF // FF_CHUNK):
        u = _dot(xn, wu_ref[:, c * FF_CHUNK:(c + 1) * FF_CHUNK])
        a = jnp.square(jnp.maximum(u, 0.0)).astype(BF16)
        acc = acc + _dot(a, wd_ref[c * FF_CHUNK:(c + 1) * FF_CHUNK, :])
        if c + 1 < D_FF // FF_CHUNK:
            yield
    if fg_ref is not None:
        acc = _rms(acc, fg_ref[...])
    store(acc)


def _ffn_kernel(*refs, final):
    if final:
        x_ref, mix_ref, wo_ref, g2_ref, wu_ref, wd_ref, fg_ref, out_ref = refs
    else:
        x_ref, mix_ref, wo_ref, g2_ref, wu_ref, wd_ref, out_ref = refs
        fg_ref = None

    def store(acc):
        out_ref[...] = acc

    _interleave([_ffn_stages(x_ref[...], mix_ref[...], wo_ref, g2_ref, wu_ref, wd_ref, fg_ref, store)])


def _back_kernel(*refs, group, L, steps, n_steps, layer, final):
    (sink_ref, q_ref, k_ref, v_ref, m4_ref, gt_ref, mg_ref, x_ref, wo_ref, g2_ref, wu_ref,
     wd_ref) = refs[:12]
    fg_ref = refs[12] if final else None
    y_ref, c_ref, n_ref, m_ref, kbuf, vt, vs, cst, mst, mix_z = refs[12 + final:]
    s = pl.program_id(0)
    wslot = s % 2
    rslot = 1 - wslot
    t = s % steps

    def mixer(i):
        one = lambda ref: ref.at[pl.ds(i, 1)]

        def init_state(put_vt):
            kbuf[i, 0:WINDOW, :] = jnp.zeros((WINDOW, LANES), BF16)
            cst[i] = jnp.zeros(cst.shape[1:], cst.dtype)
            mst[i] = jnp.zeros(mst.shape[1:], mst.dtype)

        return _mixer_stages(
            t, sink_ref, q_ref.at[i], k_ref.at[i], v_ref.at[i], m4_ref.at[i], gt_ref.at[i],
            init_state, mg_ref, mix_z.at[wslot, i], one(c_ref), one(n_ref), one(mst),
            kbuf.at[i], vt.at[i], vs.at[i], cst.at[i],
            L=L, steps=steps, halo_valid=False, layer=layer, m_out_ref=one(m_ref))

    def store(acc):
        y_ref[...] = acc.reshape(group, L, D_MODEL)

    def body(mix, feed):
        rows = group * L
        ffn = iter(()) if not feed else _ffn_stages(
            x_ref[...].reshape(rows, D_MODEL), mix_z[rslot].reshape(rows, D_MODEL),
            wo_ref, g2_ref, wu_ref, wd_ref, fg_ref, store)
        mixers = [mixer(i) for i in range(group)] if mix else []
        every = lambda: [next(m) for m in mixers]
        quarter = lambda: [next(ffn, None) for _ in range(D_FF // FF_CHUNK // 4)]
        every()
        next(ffn, None)
        every()
        quarter()
        every()
        every()
        quarter()
        every()
        _interleave([ffn])
        _interleave(mixers)

    pl.when(s == 0)(lambda: body(True, False))
    pl.when(jnp.logical_and(s > 0, s < n_steps))(lambda: body(True, True))
    pl.when(s == n_steps)(lambda: body(False, True))


def _ffn(x, mix, wo, g2, wu, wd, fg, tm, layer):
    n = x.shape[0]
    final = fg is not None
    row = lambda i: (i, 0)
    fixed = lambda i: (0, 0)
    of_layer = lambda i: (layer, 0, 0)
    single = pl.Buffered(1)
    in_specs = [
        pl.BlockSpec((tm, D_MODEL), row),
        pl.BlockSpec((tm, D_MODEL), row),
        pl.BlockSpec((None, D_MODEL, D_MODEL), of_layer, pipeline_mode=single),
        pl.BlockSpec((None, 1, D_MODEL), of_layer),
        pl.BlockSpec((None, D_MODEL, D_FF), of_layer, pipeline_mode=single),
        pl.BlockSpec((None, D_FF, D_MODEL), of_layer, pipeline_mode=single),
    ]
    args = [x, mix, wo, g2, wu, wd]
    if final:
        in_specs.append(pl.BlockSpec((1, D_MODEL), fixed))
        args.append(fg)
    return pl.pallas_call(
        functools.partial(_ffn_kernel, final=final),
        grid=(n // tm,),
        in_specs=in_specs,
        out_specs=pl.BlockSpec((tm, D_MODEL), row),
        out_shape=jax.ShapeDtypeStruct((n, D_MODEL), F32),
        compiler_params=pltpu.CompilerParams(
            dimension_semantics=("arbitrary",), vmem_limit_bytes=VMEM_LIMIT),
        name="ffn_final" if final else "ffn",
    )(*args)


def _back(x, q, k, v, m4, gt, sink, mg, wo, g2, wu, wd, fg, *, batch, group, L, layer):
    n = x.shape[0]
    T = n // batch
    steps = T // L
    n_steps = (batch // group) * steps
    assert steps > 1 and L % LANES == 0 and batch % group == 0
    final = fg is not None
    mixed = lambda s: jnp.minimum(s, n_steps - 1)
    fed = lambda s: jnp.maximum(s - 1, 0)
    mix_tile = lambda s: (mixed(s) // steps, mixed(s) % steps, 0)
    ffn_tile = lambda s: (fed(s) // steps, fed(s) % steps, 0)
    seq3 = lambda s: (mixed(s) // steps, 0, 0)
    of_layer = lambda s: (layer, 0, 0)
    single = pl.Buffered(1)
    by_seq = lambda a: a.reshape(batch, T, a.shape[-1])
    in_specs = [
        pl.BlockSpec(memory_space=pltpu.SMEM),
        pl.BlockSpec((group, L, Q_PAD), mix_tile),
        pl.BlockSpec((group, L, A_KV), mix_tile),
        pl.BlockSpec((group, L, A_KV), mix_tile),
        pl.BlockSpec((group, L, 4 * M_W), mix_tile),
        pl.BlockSpec((group, L, GATE_COLS), mix_tile),
        pl.BlockSpec((None, 1, M_W), of_layer),
        pl.BlockSpec((group, L, D_MODEL), ffn_tile),
        pl.BlockSpec((None, D_MODEL, D_MODEL), of_layer, pipeline_mode=single),
        pl.BlockSpec((None, 1, D_MODEL), of_layer),
        pl.BlockSpec((None, D_MODEL, D_FF), of_layer, pipeline_mode=single),
        pl.BlockSpec((None, D_FF, D_MODEL), of_layer, pipeline_mode=single),
    ]
    args = [sink, by_seq(q), by_seq(k), by_seq(v), by_seq(m4), by_seq(gt), mg, by_seq(x),
            wo, g2, wu, wd]
    if final:
        in_specs.append(pl.BlockSpec((1, D_MODEL), lambda s: (0, 0)))
        args.append(fg)
    kv_cols = WINDOW + L
    y, C, n_out, m_out = pl.pallas_call(
        functools.partial(_back_kernel, group=group, L=L, steps=steps, n_steps=n_steps,
                          layer=layer, final=final),
        grid=(n_steps + 1,),
        in_specs=in_specs,
        out_specs=[
            pl.BlockSpec((group, L, D_MODEL), ffn_tile),
            pl.BlockSpec((group, M_HEADS, M_HEAD_DIM, M_HEAD_DIM), lambda s: (mixed(s) // steps, 0, 0, 0)),
            pl.BlockSpec((group, M_HEADS, M_HEAD_DIM), seq3),
            pl.BlockSpec((group, SUBLANES, LANES), seq3),
        ],
        out_shape=[
            jax.ShapeDtypeStruct((batch, T, D_MODEL), F32),
            jax.ShapeDtypeStruct((batch, M_HEADS, M_HEAD_DIM, M_HEAD_DIM), F32),
            jax.ShapeDtypeStruct((batch, M_HEADS, M_HEAD_DIM), F32),
            jax.ShapeDtypeStruct((batch, SUBLANES, LANES), F32),
        ],
        scratch_shapes=[
            pltpu.VMEM((group, WINDOW + L, LANES), BF16),
            pltpu.VMEM((group, LANES, kv_cols), BF16),
            pltpu.VMEM((group, LANES, kv_cols), BF16),
            pltpu.VMEM((group, M_HEADS, 2 * M_HEAD_DIM, M_HEAD_DIM), F32),
            pltpu.VMEM((group, SUBLANES, LANES), F32),
            pltpu.VMEM((2, group, L, D_MODEL), BF16),
        ],
        compiler_params=pltpu.CompilerParams(
            dimension_semantics=("arbitrary",), vmem_limit_bytes=VMEM_LIMIT),
        name="back_final" if final else "back",
    )(*args)
    return y.reshape(n, D_MODEL), C, n_out, m_out


def _rotary_tables(pos, rows):
    inv = ROPE_THETA ** (-(jnp.arange(ROPE_HALF, dtype=F32) * 2.0 / ROPE_DIM))
    ang = pos[:, None] * inv[None, :]
    cos, sin = jnp.cos(ang), jnp.sin(ang)
    n = pos.shape[0]
    pad = jnp.zeros((n, A_HEAD_DIM - ROPE_DIM), F32)
    zeros = jnp.zeros_like(sin)
    cos_h = jnp.concatenate([cos, cos, pad + 1.0], axis=1)
    sa_h = jnp.concatenate([-sin, zeros, pad], axis=1)
    sb_h = jnp.concatenate([zeros, sin, pad], axis=1)
    rep = LANES // A_HEAD_DIM
    tile = lambda a: jnp.tile(a, (rows // n, rep))
    return tile(cos_h), tile(sa_h), tile(sb_h)


def _gate_weight_tiles(w):
    gate_pad = jnp.zeros(w.shape[:2] + (LANES - M_HEADS,), w.dtype)
    return jnp.concatenate(
        [w[..., COL_G:COL_G + M_HEADS], gate_pad, w[..., COL_G + M_HEADS:], gate_pad],
        axis=-1).astype(BF16)


def _pad_gate_bias(gb):
    pad = jnp.zeros((gb.shape[0], LANES - M_HEADS), gb.dtype)
    return jnp.concatenate([gb[:, :M_HEADS], pad, gb[:, M_HEADS:], pad], axis=1)[:, None, :]


def _m_rows(m):
    m_row = jnp.pad(m, [(0, 0)] * (m.ndim - 1) + [(0, LANES - M_HEADS)])
    return jnp.broadcast_to(m_row[..., None, :], m.shape[:-1] + (SUBLANES, LANES))


def kernel(x_prompt, x_sample, cache_k, cache_v, state_C, state_n, state_m, norm1_g, w_in, gate_b,
           attn_sink, mnorm_g, w_out, norm2_g, w_up, w_down, final_g):
    bp, T, _ = x_prompt.shape
    bs, S, _ = x_sample.shape
    depth = w_in.shape[0]
    assert T % PROMPT_MLSTM_CHUNK == 0 and S == CHUNK
    assert T % PROJ_TILE == 0 and PROJ_TILE % S == 0 and (bs * S) % PROJ_TILE == 0
    assert (bs * S) % TOKEN_TILE == 0

    tabs_p = _rotary_tables(jnp.arange(T, dtype=F32), max(T, PROJ_TILE))
    tabs_s = _rotary_tables(jnp.arange(S, dtype=F32) + float(PAST_LEN), max(S, PROJ_TILE))

    g1 = norm1_g[:, None, :]
    g2 = norm2_g[:, None, :]
    mg = mnorm_g[:, None, :]
    w1 = w_in.astype(BF16)
    wg = _gate_weight_tiles(w_in)
    gb = _pad_gate_bias(gate_b)
    wo = w_out.astype(BF16)
    wu = w_up.astype(BF16)
    wd = w_down.astype(BF16)
    halo_k = cache_k.reshape(depth, bs, WINDOW, A_KV)
    halo_v = cache_v.reshape(depth, bs, WINDOW, A_KV)
    m0 = _m_rows(state_m)

    yp = x_prompt.reshape(bp * T, D_MODEL)
    ys = x_sample.reshape(bs * S, D_MODEL)
    outs = {name: [] for name in ("pk", "pv", "pC", "pn", "pm", "sk", "sv", "sC", "sn", "sm")}
    fg = final_g.reshape(1, D_MODEL)
    for l in range(depth):
        last = fg if l == depth - 1 else None

        q, k, v, m4, gt = _inproj(yp, g1, w1, wg, tabs_p, gb, PROJ_TILE, l)
        yp, C, n, m_rows = _back(yp, q, k, v, m4, gt, attn_sink, mg, wo, g2, wu, wd, last,
                                 batch=bp, group=SEQ_GROUP, L=PROMPT_MLSTM_CHUNK, layer=l)
        outs["pk"].append(k.reshape(bp, T, A_KV)[:, T - WINDOW:])
        outs["pv"].append(v.reshape(bp, T, A_KV)[:, T - WINDOW:])
        outs["pC"].append(C)
        outs["pn"].append(n)
        outs["pm"].append(m_rows)

        q, k, v, m4, gt = _inproj(ys, g1, w1, wg, tabs_s, gb, PROJ_TILE, l)
        mix, C, n, m_rows = _mixer(q, k, v, m4, gt, halo_k, halo_v, state_C, state_n, m0,
                                   attn_sink, mg, batch=bs, group=SAMPLE_SEQ_GROUP, L=S,
                                   halo_valid=True,
                                   layer=l, state_layer=l)
        ys = _ffn(ys, mix, wo, g2, wu, wd, last, TOKEN_TILE, l)
        outs["sk"].append(k.reshape(bs, S, A_KV))
        outs["sv"].append(v.reshape(bs, S, A_KV))
        outs["sC"].append(C)
        outs["sn"].append(n)
        outs["sm"].append(m_rows)

    st = lambda name: jnp.stack(outs[name])
    kv = lambda name: (lambda a: a.reshape(a.shape[:-1] + (A_KV_HEADS, A_HEAD_DIM)))(st(name))
    m_of = lambda name: st(name)[:, :, 0, :M_HEADS]
    return (yp.reshape(bp, T, D_MODEL), ys.reshape(bs, S, D_MODEL),
            kv("pk"), kv("pv"), st("pC"), st("pn"), m_of("pm"),
            kv("sk"), kv("sv"), st("sC"), st("sn"), m_of("sm"))
```

```python
import functools

import jax
import jax.numpy as jnp
from jax import lax
from jax.experimental import pallas as pl
from jax.experimental.pallas import tpu as pltpu

D_MODEL = 1024
CHUNK = 64
A_HEADS = 8
A_KV_HEADS = 2
A_HEAD_DIM = 64
WINDOW = 128
ROPE_THETA = 500000.0
ROPE_DIM = A_HEAD_DIM // 4
ROPE_HALF = ROPE_DIM // 2
M_HEADS = 4
M_HEAD_DIM = 128
D_FF = 4 * D_MODEL
NORM_EPS = 1e-6
NEG = -1e30
LOG2E = 1.4426950408889634
PAST_LEN = 4096
A_Q = A_HEADS * A_HEAD_DIM
A_KV = A_KV_HEADS * A_HEAD_DIM
M_W = M_HEADS * M_HEAD_DIM

LANES = 128
SUBLANES = 8
Q_PAD = A_HEADS * LANES
GATE_COLS = 2 * LANES
COL_KV = A_Q
COL_M = A_Q + 2 * A_KV
COL_G = COL_M + 4 * M_W
Q_HEAD_ORDER = (0, 2, 5, 7, 1, 3, 4, 6)

TOKEN_TILE = 512
PROJ_TILE = 1024
PROMPT_MLSTM_CHUNK = 256
SEQ_GROUP = 2
SAMPLE_SEQ_GROUP = 8
FF_CHUNK = 1024
VMEM_LIMIT = 56 * 1024 * 1024

BF16 = jnp.bfloat16
F32 = jnp.float32


def _dot(a, b):
    return jnp.dot(a, b, preferred_element_type=F32)


def _dot_nt(a, b):
    return lax.dot_general(a, b, (((1,), (1,)), ((), ())), preferred_element_type=F32)


def _rms(x, g):
    r = lax.rsqrt(jnp.mean(x * x, axis=-1, keepdims=True) + NORM_EPS)
    return x * r * g


def _inproj_kernel(x_ref, g_ref, w_ref, wg_ref, cos_ref, sa_ref, sb_ref, gb_ref,
                   q_ref, k_ref, v_ref, m4_ref, gt_ref):
    h = _rms(x_ref[...], g_ref[...]).astype(BF16)
    low = lax.broadcasted_iota(jnp.int32, (1, LANES), 1) < A_HEAD_DIM

    def rotary(z):
        left = pltpu.roll(z, LANES - ROPE_HALF, axis=1)
        right = pltpu.roll(z, ROPE_HALF, axis=1)
        return z * cos_ref[...] + left * sa_ref[...] + right * sb_ref[...]

    zq_all = _dot(h, w_ref[:, :A_Q])
    zkv = _dot(h, w_ref[:, COL_KV:COL_KV + 2 * A_KV])
    for pair in range(A_Q // LANES):
        zq = rotary(zq_all[:, pair * LANES:(pair + 1) * LANES]) * (A_HEAD_DIM ** -0.5 * LOG2E)
        zsw = pltpu.roll(zq, A_HEAD_DIM, axis=1)
        zero = jnp.zeros_like(zq)
        if pair < A_HEADS // (2 * A_KV_HEADS):
            even, odd = jnp.where(low, zq, zero), jnp.where(low, zsw, zero)
        else:
            even, odd = jnp.where(low, zero, zsw), jnp.where(low, zero, zq)
        for head, val in ((2 * pair, even), (2 * pair + 1, odd)):
            j = Q_HEAD_ORDER.index(head)
            q_ref[:, j * LANES:(j + 1) * LANES] = val.astype(BF16)
    k_ref[...] = rotary(zkv[:, :A_KV])
    v_ref[...] = zkv[:, A_KV:]
    for j in range(4):
        zm = _dot(h, w_ref[:, COL_M + j * M_W:COL_M + (j + 1) * M_W])
        if j == 1:
            zm = zm * (M_HEAD_DIM ** -0.5)
        m4_ref[:, j * M_W:(j + 1) * M_W] = zm.astype(BF16)
    zg = _dot(h, wg_ref[...]) + gb_ref[...]
    zf = zg[:, LANES:]
    gt_ref[:, :LANES] = zg[:, :LANES]
    gt_ref[:, LANES:] = jnp.minimum(zf, 0.0) - jnp.log1p(jnp.exp(-jnp.abs(zf)))


def _inproj(x, g1, w, wg, tabs, gb, tm, layer):
    n = x.shape[0]
    cos, sa, sb = tabs
    nt = cos.shape[0] // tm
    row = lambda i: (i, 0)
    of_layer = lambda i: (layer, 0, 0)
    tab = lambda i: (i % nt, 0)
    return pl.pallas_call(
        _inproj_kernel,
        grid=(n // tm,),
        in_specs=[
            pl.BlockSpec((tm, D_MODEL), row),
            pl.BlockSpec((None, 1, D_MODEL), of_layer),
            pl.BlockSpec((None, D_MODEL, COL_G), of_layer),
            pl.BlockSpec((None, D_MODEL, GATE_COLS), of_layer),
            pl.BlockSpec((tm, LANES), tab),
            pl.BlockSpec((tm, LANES), tab),
            pl.BlockSpec((tm, LANES), tab),
            pl.BlockSpec((None, 1, GATE_COLS), of_layer),
        ],
        out_specs=[
            pl.BlockSpec((tm, Q_PAD), row),
            pl.BlockSpec((tm, A_KV), row),
            pl.BlockSpec((tm, A_KV), row),
            pl.BlockSpec((tm, 4 * M_W), row),
            pl.BlockSpec((tm, GATE_COLS), row),
        ],
        out_shape=[
            jax.ShapeDtypeStruct((n, Q_PAD), BF16),
            jax.ShapeDtypeStruct((n, A_KV), F32),
            jax.ShapeDtypeStruct((n, A_KV), F32),
            jax.ShapeDtypeStruct((n, 4 * M_W), BF16),
            jax.ShapeDtypeStruct((n, GATE_COLS), F32),
        ],
        compiler_params=pltpu.CompilerParams(
            dimension_semantics=("arbitrary",), vmem_limit_bytes=VMEM_LIMIT),
        name="inproj",
    )(x, g1, w, wg, cos, sa, sb, gb)


def _split3(x):
    hi = x.astype(BF16)
    r1 = x - hi.astype(F32)
    mid = r1.astype(BF16)
    lo = (r1 - mid.astype(F32)).astype(BF16)
    return hi, mid, lo


def _mixer_stages(t, sink_ref, q_ref, k_ref, v_ref, m4_ref, gt_ref, init_state, mg_ref,
                  mix_ref, c_ref, n_ref, m_ref, kbuf, vt, vs, cst,
                  *, L, steps, halo_valid, layer, m_out_ref=None):
    n_chunks = L // CHUNK
    n_keys = WINDOW + CHUNK
    half = A_HEAD_DIM

    def put_vt(cols, v):
        v_t = v.T
        vt[:, cols] = v_t.astype(BF16)
        vs[:, cols] = jnp.concatenate([v_t[half:], v_t[:half]], axis=0).astype(BF16)

    @pl.when(t == 0)
    def _():
        vt[...] = jnp.zeros_like(vt)
        vs[...] = jnp.zeros_like(vs)
        init_state(put_vt)

    yield

    kbuf[WINDOW:WINDOW + L, :] = k_ref[...].astype(BF16)
    put_vt(pl.ds(WINDOW, L), v_ref[...])

    heads = range(M_HEADS)
    m_slice = lambda j, h: m4_ref[:, j * M_W + h * M_HEAD_DIM:j * M_W + (h + 1) * M_HEAD_DIM]

    scores = []
    for ci in range(n_chunks):
        r0 = ci * CHUNK
        q8 = jnp.concatenate([q_ref[r0:r0 + CHUNK, j * LANES:(j + 1) * LANES]
                              for j in range(A_HEADS)], axis=0)
        scores.append(_dot_nt(kbuf[r0:r0 + n_keys, :], q8))

    gates_i = gt_ref[:, :LANES]
    gates_f = gt_ref[:, LANES:]
    ri = lax.broadcasted_iota(jnp.int32, (L, L), 0)
    ci_ = lax.broadcasted_iota(jnp.int32, (L, L), 1)
    causal = ci_ <= ri
    b3 = _dot(causal.astype(BF16), jnp.concatenate(_split3(gates_f), axis=1))
    b = b3[:, :LANES] + b3[:, LANES:2 * LANES] + b3[:, 2 * LANES:]
    c_old = [cst[h] for h in heads]
    qk = [_dot_nt(m_slice(0, h), m_slice(1, h)) for h in heads]
    qc = [_dot_nt(m_slice(0, h), c_old[h].astype(BF16)) for h in heads]
    yield

    lane_q = lax.broadcasted_iota(jnp.int32, (1, A_HEADS * CHUNK), 1) // CHUNK
    sink_row = jnp.zeros((1, A_HEADS * CHUNK), F32)
    for j, head in enumerate(Q_HEAD_ORDER):
        sink_row = jnp.where(lane_q == j, sink_ref[layer, head] * LOG2E, sink_row)
    key_row = lax.broadcasted_iota(jnp.int32, (n_keys, A_HEADS * CHUNK), 0)
    probs, rdens = [], []
    for ci in range(n_chunks):
        s = scores[ci]
        if not halo_valid and ci < WINDOW // CHUNK:
            first_key = (t * n_chunks + ci - WINDOW // CHUNK) * CHUNK
            s = jnp.where(key_row + first_key >= 0, s, NEG)
        mx = jnp.maximum(jnp.max(s, axis=0, keepdims=True), sink_row)
        p = jnp.exp2(s - mx)
        rdens.append(1.0 / (jnp.sum(p, axis=0, keepdims=True) + jnp.exp2(sink_row - mx)))
        pad = jnp.zeros((CHUNK, A_HEADS * CHUNK), BF16)
        pb = p.astype(BF16)
        probs.append(jnp.concatenate([pb, pad] if ci % 2 == 0 else [pad, pb], axis=0))

    a = gates_i - b
    row_l = lax.broadcasted_iota(jnp.int32, (L, LANES), 0)
    cm = a
    shift = 1
    while shift < L:
        cm = jnp.maximum(cm, jnp.where(row_l >= shift, pltpu.roll(cm, shift, axis=0), NEG))
        shift *= 2
    m_prev = m_ref[0, 0:1, :]
    c = jnp.maximum(cm, m_prev)
    mt = c + b
    w_inter = jnp.exp(m_prev - c)
    e_neg_mt = jnp.exp(-mt)
    m_last = mt[L - 1:L, :]
    b_last = b[L - 1:L, :]
    delta = b_last - m_last
    decay = jnp.exp(b_last + m_prev - m_last)
    a_t = (a * LOG2E).T
    c2 = c * LOG2E
    delta2 = delta * LOG2E
    blk = min(L, LANES)
    n_blk = L // blk
    tri = (lax.broadcasted_iota(jnp.int32, (blk, blk), 1)
           <= lax.broadcasted_iota(jnp.int32, (blk, blk), 0))
    ones = jnp.ones((L, M_HEAD_DIM), BF16)
    out_gate = [jax.nn.sigmoid(m_slice(3, h).astype(F32)) for h in heads]
    s_bf, upd_lhs = [], []
    for h in heads:
        a_row = a_t[h:h + 1, :]
        row_blocks = []
        for rb in range(n_blk):
            rows = slice(rb * blk, (rb + 1) * blk)
            c_col = c2[rows, h:h + 1]
            e = []
            for cb in range(rb + 1):
                d = a_row[:, cb * blk:(cb + 1) * blk] - c_col
                e.append(jnp.exp2(jnp.where(tri, d, NEG) if cb == rb else d))
            s = (qk[h][rows, :(rb + 1) * blk] * jnp.concatenate(e, axis=1)).astype(BF16)
            if rb + 1 < n_blk:
                s = jnp.concatenate([s, jnp.zeros((blk, L - (rb + 1) * blk), BF16)], axis=1)
            row_blocks.append(s)
        s_bf.append(jnp.concatenate(row_blocks, axis=0))
        ws_row = jnp.exp2(a_row + delta2[:, h:h + 1])
        v_t = m_slice(2, h).astype(F32).T
        upd_lhs.append(jnp.concatenate(
            [v_t * ws_row, jnp.broadcast_to(ws_row, (M_HEAD_DIM, L))], axis=0).astype(BF16))

    yield

    pv = []
    for ci in range(n_chunks):
        c0 = ci * CHUNK if ci % 2 == 0 else (ci - 1) * CHUNK
        win = slice(c0, c0 + 2 * LANES)
        pv.append((_dot(vt[:, win], probs[ci][:, :2 * LANES]),
                   _dot(vs[:, win], probs[ci][:, 2 * LANES:])))
    sv = [_dot(s_bf[h], jnp.concatenate([m_slice(2, h), ones], axis=1)) for h in heads]
    c_upd = [_dot(upd_lhs[h], m_slice(1, h)) for h in heads]
    yield

    for ci in range(n_chunks):
        r0 = ci * CHUNK
        ra = pv[ci][0] * rdens[ci][:, :2 * LANES]
        rb = pv[ci][1] * rdens[ci][:, 2 * LANES:]
        o0 = jnp.concatenate([ra[:half, :LANES], rb[half:, :LANES]], axis=0).T.astype(BF16)
        o1 = jnp.concatenate([rb[:half, LANES:], ra[half:, LANES:]], axis=0).T.astype(BF16)
        mix_ref[r0:r0 + CHUNK, 0 * LANES:1 * LANES] = o0[:CHUNK]
        mix_ref[r0:r0 + CHUNK, 1 * LANES:2 * LANES] = o0[CHUNK:]
        mix_ref[r0:r0 + CHUNK, 2 * LANES:3 * LANES] = o1[:CHUNK]
        mix_ref[r0:r0 + CHUNK, 3 * LANES:4 * LANES] = o1[CHUNK:]
    for h in heads:
        tot = sv[h] + w_inter[:, h:h + 1] * qc[h]
        hh = tot[:, :M_HEAD_DIM] / jnp.maximum(jnp.abs(tot[:, M_HEAD_DIM:]), e_neg_mt[:, h:h + 1])
        hn = _rms(hh, mg_ref[:, h * M_HEAD_DIM:(h + 1) * M_HEAD_DIM])
        out = hn * out_gate[h]
        mix_ref[:, A_Q + h * M_HEAD_DIM:A_Q + (h + 1) * M_HEAD_DIM] = out.astype(BF16)
        cst[h] = decay[:, h:h + 1] * c_old[h] + c_upd[h]
    m_ref[0] = jnp.broadcast_to(m_last, (SUBLANES, LANES))
    yield

    def write_state():
        for h in heads:
            c_ref[0, h] = cst[h, :M_HEAD_DIM, :]
            n_ref[0, h:h + 1, :] = cst[h, M_HEAD_DIM:M_HEAD_DIM + 1, :]
        if m_out_ref is not None:
            m_out_ref[...] = m_ref[...]

    if steps == 1:
        write_state()
    else:
        pl.when(t == steps - 1)(write_state)

    if steps > 1:
        kbuf[0:WINDOW, :] = kbuf[L:L + WINDOW, :]
        vt[:, 0:WINDOW] = vt[:, L:L + WINDOW]
        vs[:, 0:WINDOW] = vs[:, L:L + WINDOW]


def _interleave(gens):
    done = object()
    live = list(gens)
    while live:
        live = [g for g in live if next(g, done) is not done]


def _mixer_kernel(sink_ref, q_ref, k_ref, v_ref, m4_ref, gt_ref, hk_ref, hv_ref, c0_ref, n0_ref,
                  m0_ref, mg_ref, *rest, group, n_alias, **static):
    mix_ref, c_ref, n_ref, m_ref, kbuf, vt, vs, cst = rest[n_alias:]
    def stages(i):
        one = lambda ref: ref.at[pl.ds(i, 1)]

        def init_state(put_vt):
            kbuf[i, 0:WINDOW, :] = hk_ref[i].astype(BF16)
            put_vt(pl.ds(0, WINDOW), hv_ref[i])
            for h in range(M_HEADS):
                cst[i, h, :M_HEAD_DIM, :] = c0_ref[i, h]
                cst[i, h, M_HEAD_DIM:, :] = jnp.broadcast_to(
                    n0_ref[i, h:h + 1, :], (M_HEAD_DIM, M_HEAD_DIM))
            m_ref[i] = m0_ref[i]

        return _mixer_stages(
            pl.program_id(1), sink_ref, q_ref.at[i], k_ref.at[i], v_ref.at[i], m4_ref.at[i],
            gt_ref.at[i], init_state, mg_ref, mix_ref.at[i], one(c_ref), one(n_ref), one(m_ref),
            kbuf.at[i], vt.at[i], vs.at[i], cst.at[i], **static)

    _interleave([stages(i) for i in range(group)])


def _mixer(q, k, v, m4, gt, halo_k, halo_v, c0, n0, m0, sink, mg,
           *, batch, group, L, halo_valid, layer, state_layer, c_stack):
    n = q.shape[0]
    T = n // batch
    steps = T // L
    assert steps == 1 or L % LANES == 0
    assert batch % group == 0
    tile = lambda b, t: (b, t, 0)
    per_b3 = lambda b, t: (b, 0, 0)
    per_b4 = lambda b, t: (b, 0, 0, 0)
    init3 = lambda b, t: (state_layer, b, 0, 0)
    init4 = lambda b, t: (state_layer, b, 0, 0, 0)
    kv_cols = max(WINDOW + L, 2 * LANES)
    by_seq = lambda a: a.reshape(batch, T, a.shape[-1])
    chained = not isinstance(c_stack, jax.ShapeDtypeStruct)
    mix, C, n_out, m_out = pl.pallas_call(
        functools.partial(_mixer_kernel, group=group, n_alias=int(chained), L=L, steps=steps,
                          halo_valid=halo_valid, layer=layer),
        grid=(batch // group, steps),
        input_output_aliases={12: 1} if chained else {},
        in_specs=[
            pl.BlockSpec(memory_space=pltpu.SMEM),
            pl.BlockSpec((group, L, Q_PAD), tile),
            pl.BlockSpec((group, L, A_KV), tile),
            pl.BlockSpec((group, L, A_KV), tile),
            pl.BlockSpec((group, L, 4 * M_W), tile),
            pl.BlockSpec((group, L, GATE_COLS), tile),
            pl.BlockSpec((None, group, WINDOW, A_KV), init3),
            pl.BlockSpec((None, group, WINDOW, A_KV), init3),
            pl.BlockSpec((None, group, M_HEADS, M_HEAD_DIM, M_HEAD_DIM), init4),
            pl.BlockSpec((None, group, M_HEADS, M_HEAD_DIM), init3),
            pl.BlockSpec((None, group, SUBLANES, LANES), init3),
            pl.BlockSpec((None, 1, M_W), lambda b, t: (layer, 0, 0)),
        ] + [pl.BlockSpec(memory_space=pl.ANY)] * chained,
        out_specs=[
            pl.BlockSpec((group, L, D_MODEL), tile),
            pl.BlockSpec((None, group, M_HEADS, M_HEAD_DIM, M_HEAD_DIM),
                         lambda b, t: (layer, b, 0, 0, 0)),
            pl.BlockSpec((group, M_HEADS, M_HEAD_DIM), per_b3),
            pl.BlockSpec((group, SUBLANES, LANES), per_b3),
        ],
        out_shape=[
            jax.ShapeDtypeStruct((batch, T, D_MODEL), BF16),
            jax.ShapeDtypeStruct(c_stack.shape, F32),
            jax.ShapeDtypeStruct((batch, M_HEADS, M_HEAD_DIM), F32),
            jax.ShapeDtypeStruct((batch, SUBLANES, LANES), F32),
        ],
        scratch_shapes=[
            pltpu.VMEM((group, WINDOW + L, LANES), BF16),
            pltpu.VMEM((group, LANES, kv_cols), BF16),
            pltpu.VMEM((group, LANES, kv_cols), BF16),
            pltpu.VMEM((group, M_HEADS, 2 * M_HEAD_DIM, M_HEAD_DIM), F32),
        ],
        compiler_params=pltpu.CompilerParams(
            dimension_semantics=("arbitrary", "arbitrary"), vmem_limit_bytes=VMEM_LIMIT),
        name="mixer",
    )(sink, by_seq(q), by_seq(k), by_seq(v), by_seq(m4), by_seq(gt), halo_k, halo_v, c0, n0, m0, mg,
      *([c_stack] if chained else []))
    return mix.reshape(n, D_MODEL), C, n_out, m_out


def _ffn_stages(x, mix, wo_ref, g2_ref, wu_ref, wd_ref, fg_ref, store):
    x1 = x + _dot(mix, wo_ref[...])
    xn = _rms(x1, g2_ref[...]).astype(BF16)
    acc = x1
    yield
    for c in range(D_FF // FF_CHUNK):
        u = _dot(xn, wu_ref[:, c * FF_CHUNK:(c + 1) * FF_CHUNK])
        a = jnp.square(jnp.maximum(u, 0.0)).astype(BF16)
        acc = acc + _dot(a, wd_ref[c * FF_CHUNK:(c + 1) * FF_CHUNK, :])
        if c + 1 < D_FF // FF_CHUNK:
            yield
    if fg_ref is not None:
        acc = _rms(acc, fg_ref[...])
    store(acc)


def _ffn_kernel(*refs, final):
    if final:
        x_ref, mix_ref, wo_ref, g2_ref, wu_ref, wd_ref, fg_ref, out_ref = refs
    else:
        x_ref, mix_ref, wo_ref, g2_ref, wu_ref, wd_ref, out_ref = refs
        fg_ref = None

    def store(acc):
        out_ref[...] = acc

    _interleave([_ffn_stages(x_ref[...], mix_ref[...], wo_ref, g2_ref, wu_ref, wd_ref, fg_ref, store)])


def _back_kernel(*refs, group, L, steps, n_steps, layer, final, n_alias):
    (sink_ref, q_ref, k_ref, v_ref, m4_ref, gt_ref, mg_ref, x_ref, wo_ref, g2_ref, wu_ref,
     wd_ref) = refs[:12]
    fg_ref = refs[12] if final else None
    y_ref, c_ref, n_ref, m_ref, kbuf, vt, vs, cst, mst, mix_z = refs[12 + final + n_alias:]
    s = pl.program_id(0)
    wslot = s % 2
    rslot = 1 - wslot
    t = s % steps

    def mixer(i):
        one = lambda ref: ref.at[pl.ds(i, 1)]

        def init_state(put_vt):
            kbuf[i, 0:WINDOW, :] = jnp.zeros((WINDOW, LANES), BF16)
            cst[i] = jnp.zeros(cst.shape[1:], cst.dtype)
            mst[i] = jnp.zeros(mst.shape[1:], mst.dtype)

        return _mixer_stages(
            t, sink_ref, q_ref.at[i], k_ref.at[i], v_ref.at[i], m4_ref.at[i], gt_ref.at[i],
            init_state, mg_ref, mix_z.at[wslot, i], one(c_ref), one(n_ref), one(mst),
            kbuf.at[i], vt.at[i], vs.at[i], cst.at[i],
            L=L, steps=steps, halo_valid=False, layer=layer, m_out_ref=one(m_ref))

    def store(acc):
        y_ref[...] = acc.reshape(group, L, D_MODEL)

    def body(mix, feed):
        rows = group * L
        ffn = iter(()) if not feed else _ffn_stages(
            x_ref[...].reshape(rows, D_MODEL), mix_z[rslot].reshape(rows, D_MODEL),
            wo_ref, g2_ref, wu_ref, wd_ref, fg_ref, store)
        mixers = [mixer(i) for i in range(group)] if mix else []
        every = lambda: [next(m) for m in mixers]
        quarter = lambda: [next(ffn, None) for _ in range(D_FF // FF_CHUNK // 4)]
        every()
        next(ffn, None)
        every()
        quarter()
        every()
        every()
        quarter()
        every()
        _interleave([ffn])
        _interleave(mixers)

    pl.when(s == 0)(lambda: body(True, False))
    pl.when(jnp.logical_and(s > 0, s < n_steps))(lambda: body(True, True))
    pl.when(s == n_steps)(lambda: body(False, True))


def _ffn(x, mix, wo, g2, wu, wd, fg, tm, layer):
    n = x.shape[0]
    final = fg is not None
    row = lambda i: (i, 0)
    fixed = lambda i: (0, 0)
    of_layer = lambda i: (layer, 0, 0)
    single = pl.Buffered(1)
    in_specs = [
        pl.BlockSpec((tm, D_MODEL), row),
        pl.BlockSpec((tm, D_MODEL), row),
        pl.BlockSpec((None, D_MODEL, D_MODEL), of_layer, pipeline_mode=single),
        pl.BlockSpec((None, 1, D_MODEL), of_layer),
        pl.BlockSpec((None, D_MODEL, D_FF), of_layer, pipeline_mode=single),
        pl.BlockSpec((None, D_FF, D_MODEL), of_layer, pipeline_mode=single),
    ]
    args = [x, mix, wo, g2, wu, wd]
    if final:
        in_specs.append(pl.BlockSpec((1, D_MODEL), fixed))
        args.append(fg)
    return pl.pallas_call(
        functools.partial(_ffn_kernel, final=final),
        grid=(n // tm,),
        in_specs=in_specs,
        out_specs=pl.BlockSpec((tm, D_MODEL), row),
        out_shape=jax.ShapeDtypeStruct((n, D_MODEL), F32),
        compiler_params=pltpu.CompilerParams(
            dimension_semantics=("arbitrary",), vmem_limit_bytes=VMEM_LIMIT),
        name="ffn_final" if final else "ffn",
    )(*args)


def _back(x, q, k, v, m4, gt, sink, mg, wo, g2, wu, wd, fg, *, batch, group, L, layer, c_stack):
    n = x.shape[0]
    T = n // batch
    steps = T // L
    n_steps = (batch // group) * steps
    assert steps > 1 and L % LANES == 0 and batch % group == 0
    final = fg is not None
    mixed = lambda s: jnp.minimum(s, n_steps - 1)
    fed = lambda s: jnp.maximum(s - 1, 0)
    mix_tile = lambda s: (mixed(s) // steps, mixed(s) % steps, 0)
    ffn_tile = lambda s: (fed(s) // steps, fed(s) % steps, 0)
    seq3 = lambda s: (mixed(s) // steps, 0, 0)
    of_layer = lambda s: (layer, 0, 0)
    single = pl.Buffered(1)
    by_seq = lambda a: a.reshape(batch, T, a.shape[-1])
    in_specs = [
        pl.BlockSpec(memory_space=pltpu.SMEM),
        pl.BlockSpec((group, L, Q_PAD), mix_tile),
        pl.BlockSpec((group, L, A_KV), mix_tile),
        pl.BlockSpec((group, L, A_KV), mix_tile),
        pl.BlockSpec((group, L, 4 * M_W), mix_tile),
        pl.BlockSpec((group, L, GATE_COLS), mix_tile),
        pl.BlockSpec((None, 1, M_W), of_layer),
        pl.BlockSpec((group, L, D_MODEL), ffn_tile),
        pl.BlockSpec((None, D_MODEL, D_MODEL), of_layer, pipeline_mode=single),
        pl.BlockSpec((None, 1, D_MODEL), of_layer),
        pl.BlockSpec((None, D_MODEL, D_FF), of_layer, pipeline_mode=single),
        pl.BlockSpec((None, D_FF, D_MODEL), of_layer, pipeline_mode=single),
    ]
    args = [sink, by_seq(q), by_seq(k), by_seq(v), by_seq(m4), by_seq(gt), mg, by_seq(x),
            wo, g2, wu, wd]
    if final:
        in_specs.append(pl.BlockSpec((1, D_MODEL), lambda s: (0, 0)))
        args.append(fg)
    chained = not isinstance(c_stack, jax.ShapeDtypeStruct)
    if chained:
        in_specs.append(pl.BlockSpec(memory_space=pl.ANY))
        args.append(c_stack)
    kv_cols = WINDOW + L
    y, C, n_out, m_out = pl.pallas_call(
        functools.partial(_back_kernel, group=group, L=L, steps=steps, n_steps=n_steps,
                          layer=layer, final=final, n_alias=int(chained)),
        grid=(n_steps + 1,),
        input_output_aliases={len(args) - 1: 1} if chained else {},
        in_specs=in_specs,
        out_specs=[
            pl.BlockSpec((group, L, D_MODEL), ffn_tile),
            pl.BlockSpec((None, group, M_HEADS, M_HEAD_DIM, M_HEAD_DIM),
                         lambda s: (layer, mixed(s) // steps, 0, 0, 0)),
            pl.BlockSpec((group, M_HEADS, M_HEAD_DIM), seq3),
            pl.BlockSpec((group, SUBLANES, LANES), seq3),
        ],
        out_shape=[
            jax.ShapeDtypeStruct((batch, T, D_MODEL), F32),
            jax.ShapeDtypeStruct(c_stack.shape, F32),
            jax.ShapeDtypeStruct((batch, M_HEADS, M_HEAD_DIM), F32),
            jax.ShapeDtypeStruct((batch, SUBLANES, LANES), F32),
        ],
        scratch_shapes=[
            pltpu.VMEM((group, WINDOW + L, LANES), BF16),
            pltpu.VMEM((group, LANES, kv_cols), BF16),
            pltpu.VMEM((group, LANES, kv_cols), BF16),
            pltpu.VMEM((group, M_HEADS, 2 * M_HEAD_DIM, M_HEAD_DIM), F32),
            pltpu.VMEM((group, SUBLANES, LANES), F32),
            pltpu.VMEM((2, group, L, D_MODEL), BF16),
        ],
        compiler_params=pltpu.CompilerParams(
            dimension_semantics=("arbitrary",), vmem_limit_bytes=VMEM_LIMIT),
        name="back_final" if final else "back",
    )(*args)
    return y.reshape(n, D_MODEL), C, n_out, m_out


def _rotary_tables(pos, rows):
    inv = ROPE_THETA ** (-(jnp.arange(ROPE_HALF, dtype=F32) * 2.0 / ROPE_DIM))
    ang = pos[:, None] * inv[None, :]
    cos, sin = jnp.cos(ang), jnp.sin(ang)
    n = pos.shape[0]
    pad = jnp.zeros((n, A_HEAD_DIM - ROPE_DIM), F32)
    zeros = jnp.zeros_like(sin)
    cos_h = jnp.concatenate([cos, cos, pad + 1.0], axis=1)
    sa_h = jnp.concatenate([-sin, zeros, pad], axis=1)
    sb_h = jnp.concatenate([zeros, sin, pad], axis=1)
    rep = LANES // A_HEAD_DIM
    tile = lambda a: jnp.tile(a, (rows // n, rep))
    return tile(cos_h), tile(sa_h), tile(sb_h)


def _gate_weight_tiles(w):
    gate_pad = jnp.zeros(w.shape[:2] + (LANES - M_HEADS,), w.dtype)
    return jnp.concatenate(
        [w[..., COL_G:COL_G + M_HEADS], gate_pad, w[..., COL_G + M_HEADS:], gate_pad],
        axis=-1).astype(BF16)


def _pad_gate_bias(gb):
    pad = jnp.zeros((gb.shape[0], LANES - M_HEADS), gb.dtype)
    return jnp.concatenate([gb[:, :M_HEADS], pad, gb[:, M_HEADS:], pad], axis=1)[:, None, :]


def _m_rows(m):
    m_row = jnp.pad(m, [(0, 0)] * (m.ndim - 1) + [(0, LANES - M_HEADS)])
    return jnp.broadcast_to(m_row[..., None, :], m.shape[:-1] + (SUBLANES, LANES))


def kernel(x_prompt, x_sample, cache_k, cache_v, state_C, state_n, state_m, norm1_g, w_in, gate_b,
           attn_sink, mnorm_g, w_out, norm2_g, w_up, w_down, final_g):
    bp, T, _ = x_prompt.shape
    bs, S, _ = x_sample.shape
    depth = w_in.shape[0]
    assert T % PROMPT_MLSTM_CHUNK == 0 and S == CHUNK
    assert T % PROJ_TILE == 0 and PROJ_TILE % S == 0 and (bs * S) % PROJ_TILE == 0
    assert (bs * S) % TOKEN_TILE == 0

    tabs_p = _rotary_tables(jnp.arange(T, dtype=F32), max(T, PROJ_TILE))
    tabs_s = _rotary_tables(jnp.arange(S, dtype=F32) + float(PAST_LEN), max(S, PROJ_TILE))

    g1 = norm1_g[:, None, :]
    g2 = norm2_g[:, None, :]
    mg = mnorm_g[:, None, :]
    w1 = w_in.astype(BF16)
    wg = _gate_weight_tiles(w_in)
    gb = _pad_gate_bias(gate_b)
    wo = w_out.astype(BF16)
    wu = w_up.astype(BF16)
    wd = w_down.astype(BF16)
    halo_k = cache_k.reshape(depth, bs, WINDOW, A_KV)
    halo_v = cache_v.reshape(depth, bs, WINDOW, A_KV)
    m0 = _m_rows(state_m)

    yp = x_prompt.reshape(bp * T, D_MODEL)
    ys = x_sample.reshape(bs * S, D_MODEL)
    outs = {name: [] for name in ("pk", "pv", "pn", "pm", "sk", "sv", "sn", "sm")}
    p_c = jax.ShapeDtypeStruct((depth, bp, M_HEADS, M_HEAD_DIM, M_HEAD_DIM), F32)
    s_c = jax.ShapeDtypeStruct((depth, bs, M_HEADS, M_HEAD_DIM, M_HEAD_DIM), F32)
    fg = final_g.reshape(1, D_MODEL)
    for l in range(depth):
        last = fg if l == depth - 1 else None

        q, k, v, m4, gt = _inproj(yp, g1, w1, wg, tabs_p, gb, PROJ_TILE, l)
        yp, p_c, n, m_rows = _back(yp, q, k, v, m4, gt, attn_sink, mg, wo, g2, wu, wd, last,
                                   batch=bp, group=SEQ_GROUP, L=PROMPT_MLSTM_CHUNK, layer=l,
                                   c_stack=p_c)
        outs["pk"].append(k.reshape(bp, T, A_KV)[:, T - WINDOW:])
        outs["pv"].append(v.reshape(bp, T, A_KV)[:, T - WINDOW:])
        outs["pn"].append(n)
        outs["pm"].append(m_rows)

        q, k, v, m4, gt = _inproj(ys, g1, w1, wg, tabs_s, gb, PROJ_TILE, l)
        mix, s_c, n, m_rows = _mixer(q, k, v, m4, gt, halo_k, halo_v, state_C, state_n, m0,
                                   attn_sink, mg, batch=bs, group=SAMPLE_SEQ_GROUP, L=S,
                                   halo_valid=True,
                                   layer=l, state_layer=l, c_stack=s_c)
        ys = _ffn(ys, mix, wo, g2, wu, wd, last, TOKEN_TILE, l)
        outs["sk"].append(k.reshape(bs, S, A_KV))
        outs["sv"].append(v.reshape(bs, S, A_KV))
        outs["sn"].append(n)
        outs["sm"].append(m_rows)

    st = lambda name: jnp.stack(outs[name])
    kv = lambda name: (lambda a: a.reshape(a.shape[:-1] + (A_KV_HEADS, A_HEAD_DIM)))(st(name))
    m_of = lambda name: st(name)[:, :, 0, :M_HEADS]
    return (yp.reshape(bp, T, D_MODEL), ys.reshape(bs, S, D_MODEL),
            kv("pk"), kv("pv"), p_c, st("pn"), m_of("pm"),
            kv("sk"), kv("sv"), s_c, st("sn"), m_of("sm"))
```
